```python
import math
import jax, jax.numpy as jnp
from jax import lax
import numpy as np

D_MODEL = 1024
BATCH = 8
SEQ = 2048
DEPTH = 2

N_META = 16
EPS = 1e-6
ROPE_THETA = 10000.0
NEG_INF = -1e30
GLA_HEADS = 4
GLA_DK = 64
GLA_DV = 128
GLA_LOWRANK = 16
GLA_TAU = 16.0
GLA_CHUNK = 64
SWA_Q_HEADS = 8
SWA_KV_HEADS = 2
SWA_HEAD_DIM = 64
SWA_WINDOW = 128
SWA_BLOCK = 128
CONV_WIDTH = 31
CONV_CH = D_MODEL
D_FF = 3584
N_EXPERTS = 8
TOP_K = 2

GLA_QK = GLA_HEADS * GLA_DK
GLA_V = GLA_HEADS * GLA_DV
SWA_Q = SWA_Q_HEADS * SWA_HEAD_DIM
SWA_KV = SWA_KV_HEADS * SWA_HEAD_DIM
A_IN_WIDTHS = (GLA_QK, GLA_QK, GLA_V, GLA_V, GLA_LOWRANK, SWA_Q, SWA_KV, SWA_KV)
A_IN_COLS = sum(A_IN_WIDTHS)
A_SPLITS = tuple(int(s) for s in np.cumsum(A_IN_WIDTHS)[:-1])
MIX_OUT = GLA_V + SWA_Q
N_EVEN = (DEPTH + 1) // 2
N_ODD = DEPTH // 2

kernel_name = 'hybrid_gla_swa_conformer_moe'


def rms_norm(x, g):
    xf = x.astype(jnp.float32)
    y = xf * lax.rsqrt(jnp.mean(xf * xf, axis=-1, keepdims=True) + EPS)
    return (y * g.astype(jnp.float32)).astype(x.dtype)


def layer_norm(x, g, b):
    xf = x.astype(jnp.float32)
    mu = jnp.mean(xf, axis=-1, keepdims=True)
    var = jnp.mean(jnp.square(xf - mu), axis=-1, keepdims=True)
    y = (xf - mu) * lax.rsqrt(var + EPS)
    return (y * g.astype(jnp.float32) + b.astype(jnp.float32)).astype(x.dtype)


def rope_tables(length):
    inv_freq = 1.0 / (ROPE_THETA ** (jnp.arange(0, SWA_HEAD_DIM, 2, dtype=jnp.float32) / SWA_HEAD_DIM))
    ang = jnp.arange(length, dtype=jnp.float32)[:, None] * inv_freq[None, :]
    return jnp.cos(ang)[:, None, :], jnp.sin(ang)[:, None, :]


def apply_rope(x, cos, sin):
    half = x.shape[-1] // 2
    x1, x2 = x[..., :half], x[..., half:]
    cos = cos.astype(x.dtype)
    sin = sin.astype(x.dtype)
    return jnp.concatenate([x1 * cos - x2 * sin, x2 * cos + x1 * sin], axis=-1)


def gla_chunked(q, k, v, g):
    B, L, H, DK = q.shape
    DV = v.shape[-1]
    C = GLA_CHUNK
    pad = GLA_CHUNK - N_META
    n = (L + pad) // C

    def prep(t):
        t = jnp.pad(t.astype(jnp.float32), ((0, 0), (pad, 0), (0, 0), (0, 0)))
        return t.reshape(B, n, C, H, t.shape[-1]).transpose(0, 3, 1, 2, 4)

    q, k, v, g = prep(q) * (DK ** -0.5), prep(k), prep(v), prep(g)
    b = jnp.cumsum(g, axis=3)
    b_last = b[:, :, :, -1:, :]
    q_t = q * jnp.exp(b)
    k_t = k * jnp.exp(-b)
    causal = jnp.tril(jnp.ones((C, C), dtype=bool))
    att = jnp.where(causal, jnp.einsum('bhncd,bhnsd->bhncs', q_t, k_t), 0.0)
    o_intra = jnp.einsum('bhncs,bhnse->bhnce', att, v)
    kv = jnp.einsum('bhncd,bhnce->bhnde', k * jnp.exp(b_last - b), v)
    decay = jnp.exp(b_last[:, :, :, 0, :])

    def step(S, inp):
        dec_n, kv_n = inp
        return dec_n[..., None] * S + kv_n, S

    S0 = jnp.zeros((B, H, DK, DV), jnp.float32)
    _, S_in = lax.scan(step, S0, (jnp.moveaxis(decay, 2, 0), jnp.moveaxis(kv, 2, 0)))
    S_in = jnp.moveaxis(S_in, 0, 2)
    o = o_intra + jnp.einsum('bhncd,bhnde->bhnce', q_t, S_in)
    o = o.transpose(0, 2, 3, 1, 4).reshape(B, n * C, H, DV)
    return o[:, pad:]


def swa_sinks(q, k, v, sinks):
    B, L, HQ, D = q.shape
    HKV = k.shape[2]
    G = HQ // HKV
    W = SWA_BLOCK
    nb = -(-L // W)
    pad = nb * W - L

    def prep(t):
        return jnp.pad(t, ((0, 0), (pad, 0), (0, 0), (0, 0))).reshape(B, nb, W, t.shape[2], D)

    def with_prev(t):
        prev = jnp.pad(t, ((0, 0), (1, 0), (0, 0), (0, 0), (0, 0)))[:, :-1]
        return jnp.concatenate([prev, t], axis=2)

    qb = prep(q).reshape(B, nb, W, HKV, G, D)
    kk = with_prev(prep(k))
    vv = with_prev(prep(v))
    s = jnp.einsum('bnqhgd,bnkhd->bhgnqk', qb, kk).astype(jnp.float32) * (D ** -0.5)
    qi = jnp.arange(nb)[:, None] * W + jnp.arange(W)[None, :]
    ki = (jnp.arange(nb)[:, None] - 1) * W + jnp.arange(2 * W)[None, :]
    diff = qi[:, :, None] - ki[:, None, :]
    allowed = (diff >= 0) & (diff < SWA_WINDOW) & (ki[:, None, :] >= pad)
    s = jnp.where(allowed, s, NEG_INF)
    sink = jnp.broadcast_to(sinks.astype(jnp.float32).reshape(1, HKV, G, 1, 1, 1), s.shape[:-1] + (1,))
    p = jax.nn.softmax(jnp.concatenate([s, sink], axis=-1), axis=-1)[..., :-1]
    o = jnp.einsum('bhgnqk,bnkhd->bnqhgd', p.astype(v.dtype), vv)
    return o.reshape(B, nb * W, HQ, D)[:, pad:]


def mixer_gla_swa(h, w_in, w_gate2, b_gate, q_norm, k_norm, sinks, o_norm, w_out, cos, sin):
    B, L, _ = h.shape
    z = h @ w_in
    gq, gk, gv, gr, glr, sq, sk, sv = jnp.split(z, A_SPLITS, axis=-1)
    g = jax.nn.log_sigmoid((glr @ w_gate2 + b_gate).astype(jnp.float32)) / GLA_TAU
    o_a = gla_chunked(gq.reshape(B, L, GLA_HEADS, GLA_DK), gk.reshape(B, L, GLA_HEADS, GLA_DK),
                      gv.reshape(B, L, GLA_HEADS, GLA_DV), g.reshape(B, L, GLA_HEADS, GLA_DK))
    o_a = rms_norm(o_a.astype(h.dtype), o_norm).reshape(B, L, GLA_V) * jax.nn.silu(gr)
    q = apply_rope(rms_norm(sq.reshape(B, L, SWA_Q_HEADS, SWA_HEAD_DIM), q_norm), cos, sin)
    k = apply_rope(rms_norm(sk.reshape(B, L, SWA_KV_HEADS, SWA_HEAD_DIM), k_norm), cos, sin)
    o_b = swa_sinks(q, k, sv.reshape(B, L, SWA_KV_HEADS, SWA_HEAD_DIM), sinks).reshape(B, L, SWA_Q)
    return jnp.concatenate([o_a, o_b], axis=-1) @ w_out


def conformer_conv(h, w_pw1, w_dw, b_dw, ln_g, ln_b, w_pw2):
    a, gate = jnp.split(h @ w_pw1, 2, axis=-1)
    u = a * jax.nn.sigmoid(gate)
    u = lax.conv_general_dilated(u, w_dw[:, None, :].astype(u.dtype), window_strides=(1,),
                                 padding=[(CONV_WIDTH - 1, 0)],
                                 dimension_numbers=('NWC', 'WIO', 'NWC'),
                                 feature_group_count=CONV_CH) + b_dw
    u = jax.nn.silu(layer_norm(u, ln_g, ln_b))
    return u @ w_pw2


def swiglu(h, w1, w3, w2):
    return (jax.nn.silu(h @ w1) * (h @ w3)) @ w2


def moe_swiglu(h, w_router, w1, w3, w2):
    B, L, D = h.shape
    t = h.reshape(B * L, D)
    logits = (t @ w_router).astype(jnp.float32)
    top_v, top_i = lax.top_k(logits, TOP_K)
    gates = jax.nn.softmax(top_v, axis=-1)
    y = jnp.zeros_like(t)
    for e in range(N_EXPERTS):
        g_e = jnp.sum(jnp.where(top_i == e, gates, 0.0), axis=-1).astype(t.dtype)
        y = y + g_e[:, None] * swiglu(t, w1[e], w3[e], w2[e])
    return y.reshape(B, L, D)


def setup_inputs(seed: int = 0) -> dict:
    key = jax.random.key(seed)
    keys = list(jax.random.split(key, 32))

    def nrm(shape, scale):
        return jax.random.normal(keys.pop(), shape, jnp.float32) * scale

    def gain(shape):
        return 1.0 + 0.05 * jax.random.normal(keys.pop(), shape, jnp.float32)

    NE, NO, D, E = N_EVEN, N_ODD, D_MODEL, N_EXPERTS
    return {
        'x': nrm((BATCH, SEQ, D), 1.0),
        'meta': nrm((N_META, D), 1.0),
        'a_norm': gain((NE, D)),
        'a_w_in': nrm((NE, D, A_IN_COLS), D ** -0.5),
        'a_w_gate2': nrm((NE, GLA_LOWRANK, GLA_QK), GLA_LOWRANK ** -0.5),
        'a_b_gate': nrm((NE, GLA_QK), 0.1),
        'a_q_norm': gain((NE, SWA_HEAD_DIM)),
        'a_k_norm': gain((NE, SWA_HEAD_DIM)),
        'a_sinks': nrm((NE, SWA_Q_HEADS), 0.5),
        'a_o_norm': gain((NE, GLA_DV)),
        'a_w_out': nrm((NE, MIX_OUT, D), MIX_OUT ** -0.5),
        'f_norm': gain((NE, D)),
        'f_w1': nrm((NE, D, D_FF), D ** -0.5),
        'f_w3': nrm((NE, D, D_FF), D ** -0.5),
        'f_w2': nrm((NE, D_FF, D), D_FF ** -0.5),
        'c_norm': gain((NO, D)),
        'c_w_pw1': nrm((NO, D, 2 * CONV_CH), D ** -0.5),
        'c_w_dw': nrm((NO, CONV_WIDTH, CONV_CH), CONV_WIDTH ** -0.5),
        'c_b_dw': nrm((NO, CONV_CH), 0.02),
        'c_ln_g': gain((NO, CONV_CH)),
        'c_ln_b': nrm((NO, CONV_CH), 0.02),
        'c_w_pw2': nrm((NO, CONV_CH, D), CONV_CH ** -0.5),
        'm_norm': gain((NO, D)),
        'm_w_router': nrm((NO, D, E), D ** -0.5),
        'm_w1': nrm((NO, E, D, D_FF), D ** -0.5),
        'm_w3': nrm((NO, E, D, D_FF), D ** -0.5),
        'm_w2': nrm((NO, E, D_FF, D), D_FF ** -0.5),
    }


def reference(x, meta, a_norm, a_w_in, a_w_gate2, a_b_gate, a_q_norm, a_k_norm, a_sinks,
              a_o_norm, a_w_out, f_norm, f_w1, f_w3, f_w2, c_norm, c_w_pw1, c_w_dw, c_b_dw,
              c_ln_g, c_ln_b, c_w_pw2, m_norm, m_w_router, m_w1, m_w3, m_w2):
    B = x.shape[0]
    L = N_META + x.shape[1]
    h = jnp.concatenate([jnp.broadcast_to(meta[None].astype(x.dtype), (B, N_META, D_MODEL)), x], axis=1)
    cos, sin = rope_tables(L)
    for layer in range(DEPTH):
        j = layer // 2
        if layer % 2 == 0:
            h = h + mixer_gla_swa(rms_norm(h, a_norm[j]), a_w_in[j], a_w_gate2[j], a_b_gate[j],
                                  a_q_norm[j], a_k_norm[j], a_sinks[j], a_o_norm[j], a_w_out[j], cos, sin)
            h = h + swiglu(rms_norm(h, f_norm[j]), f_w1[j], f_w3[j], f_w2[j])
        else:
            h = h + conformer_conv(rms_norm(h, c_norm[j]), c_w_pw1[j], c_w_dw[j], c_b_dw[j],
                                   c_ln_g[j], c_ln_b[j], c_w_pw2[j])
            h = h + moe_swiglu(rms_norm(h, m_norm[j]), m_w_router[j], m_w1[j], m_w3[j], m_w2[j])
    return h[:, N_META:]
```

```python
import functools
import math

import jax
import jax.numpy as jnp
from jax import lax
from jax.experimental import pallas as pl
from jax.experimental.pallas import tpu as pltpu

F32 = jnp.float32
BF16 = jnp.bfloat16

D_MODEL = 1024
N_META = 16
EPS = 1e-6
ROPE_THETA = 10000.0
NEG_INF = -1e30
GLA_HEADS = 4
GLA_DK = 64
GLA_DV = 128
GLA_LOWRANK = 16
GLA_TAU = 16.0
GLA_CHUNK = 64
SWA_Q_HEADS = 8
SWA_KV_HEADS = 2
SWA_HEAD_DIM = 64
SWA_WINDOW = 128
CONV_WIDTH = 31
D_FF = 3584
N_EXPERTS = 8

GLA_QK = GLA_HEADS * GLA_DK
GLA_V = GLA_HEADS * GLA_DV
SWA_Q = SWA_Q_HEADS * SWA_HEAD_DIM
SWA_KV = SWA_KV_HEADS * SWA_HEAD_DIM
SWA_GROUP = SWA_Q_HEADS // SWA_KV_HEADS

LANES = 128
BLOCK = 128
FRONT = BLOCK - N_META
VMEM_LIMIT = 56 * 1024 * 1024

Z_Q, Z_K, Z_V, Z_R = 0, 256, 512, 1024
Z_SQ, Z_SK, Z_SV, Z_LR = 1536, 2048, 2176, 2304
Z_COLS = 2432


def _rms(x, g):
    return x * lax.rsqrt(jnp.mean(x * x, axis=-1, keepdims=True) + EPS) * g


def _dot(a, b):
    return jnp.dot(a, b, preferred_element_type=F32)


def _dot_nt(a, b):
    return lax.dot_general(a, b, (((1,), (1,)), ((), ())), preferred_element_type=F32)


def _dot_tn(a, b):
    return lax.dot_general(a, b, (((0,), (0,)), ((), ())), preferred_element_type=F32)


def _params(*sem):
    return pltpu.CompilerParams(dimension_semantics=sem, vmem_limit_bytes=VMEM_LIMIT)


def _in_proj_kernel(h_ref, g_ref, w_ref, z_ref):
    xn = _rms(h_ref[...], g_ref[...]).astype(BF16)
    z_ref[...] = _dot(xn, w_ref[...]).astype(z_ref.dtype)


def _in_proj(h, g, w, tm):
    rows = h.shape[0]
    n = w.shape[1]
    return pl.pallas_call(
        _in_proj_kernel,
        out_shape=jax.ShapeDtypeStruct((rows, n), BF16),
        grid=(rows // tm,),
        in_specs=[pl.BlockSpec((tm, D_MODEL), lambda i: (i, 0)),
                  pl.BlockSpec((1, D_MODEL), lambda i: (0, 0)),
                  pl.BlockSpec((D_MODEL, n), lambda i: (0, 0))],
        out_specs=pl.BlockSpec((tm, n), lambda i: (i, 0)),
        compiler_params=_params("parallel"),
        name="in_proj",
    )(h, g, w)


def _gla_kernel(q_ref, k_ref, v_ref, r_ref, lr_ref, wg_ref, bg_ref, on_ref, o_ref, st_ref):
    n = pl.program_id(1)

    @pl.when(n == 0)
    def _():
        st_ref[...] = jnp.zeros_like(st_ref)

    c = GLA_CHUNK
    gp = _dot(lr_ref[...], wg_ref[...]) + bg_ref[...]
    g = (jnp.minimum(gp, 0.0) - jnp.log(1.0 + jnp.exp(-jnp.abs(gp)))) * (1.0 / GLA_TAU)
    row = lax.broadcasted_iota(jnp.int32, g.shape, 0) + n * BLOCK
    g = jnp.where(row >= FRONT, g, 0.0)

    ri = lax.broadcasted_iota(jnp.int32, (c, c), 0)
    ci = lax.broadcasted_iota(jnp.int32, (c, c), 1)
    causal = ri >= ci
    tri = jnp.where(causal, 1.0, 0.0).astype(BF16)
    on = on_ref[...]

    for ch in range(BLOCK // c):
        rs = slice(ch * c, (ch + 1) * c)
        gc = g[rs]
        g_hi = gc.astype(BF16)
        g_lo = (gc - g_hi.astype(F32)).astype(BF16)
        b = _dot(tri, g_hi) + _dot(tri, g_lo)
        b_last = b[c - 1:c, :]
        qf = q_ref[rs, :].astype(F32)
        kf = k_ref[rs, :].astype(F32)
        q_t = (qf * jnp.exp(b) * (GLA_DK ** -0.5)).astype(BF16)
        k_t = (kf * jnp.exp(-b)).astype(BF16)
        k_d = (kf * jnp.exp(b_last - b)).astype(BF16)
        dec = jnp.exp(b_last)
        for hd in range(GLA_HEADS):
            ks = slice(hd * GLA_DK, (hd + 1) * GLA_DK)
            vs = slice(hd * GLA_DV, (hd + 1) * GLA_DV)
            qh = q_t[:, ks]
            att = jnp.where(causal, _dot_nt(qh, k_t[:, ks]), 0.0)
            vh = v_ref[rs, vs]
            st = st_ref[hd]
            o = _dot(att.astype(BF16), vh) + _dot_nt(qh, st.astype(BF16))
            st_ref[hd] = st * dec[:, ks] + _dot_tn(vh, k_d[:, ks])
            o = _rms(o, on)
            r = r_ref[rs, vs].astype(F32)
            o_ref[rs, vs] = (o * (r * jax.nn.sigmoid(r))).astype(o_ref.dtype)


def _gla(z, wg, bg, on, batch, nb):
    rows = z.shape[0]

    def col(width, start):
        return pl.BlockSpec((BLOCK, width), lambda b, n: (b * nb + n, start // width))

    const = lambda shape: pl.BlockSpec(shape, lambda b, n: (0,) * len(shape))
    return pl.pallas_call(
        _gla_kernel,
        out_shape=jax.ShapeDtypeStruct((rows, GLA_V), BF16),
        grid=(batch, nb),
        in_specs=[col(GLA_QK, Z_Q), col(GLA_QK, Z_K), col(GLA_V, Z_V), col(GLA_V, Z_R),
                  col(LANES, Z_LR), const((LANES, GLA_QK)), const((1, GLA_QK)),
                  const((1, GLA_DV))],
        out_specs=pl.BlockSpec((BLOCK, GLA_V), lambda b, n: (b * nb + n, 0)),
        scratch_shapes=[pltpu.VMEM((GLA_HEADS, GLA_DV, GLA_DK), F32)],
        compiler_params=_params("parallel", "arbitrary"),
        name="gla",
    )(z, z, z, z, z, wg, bg, on)


def _group_mean_sq(x, bd):
    sq = x * x
    hi = sq.astype(BF16)
    lo = (sq - hi.astype(F32)).astype(BF16)
    return _dot(hi, bd) + _dot(lo, bd)


def _norm_rope(x, gain, cos, sin_a, sin_b, bd):
    width = x.shape[-1]
    parts = []
    for s in range(width // LANES):
        xs = x[:, s * LANES:(s + 1) * LANES]
        ms = _group_mean_sq(xs, bd) * (1.0 / SWA_HEAD_DIM)
        parts.append(xs * lax.rsqrt(ms + EPS))
    xn = (parts[0] if len(parts) == 1 else jnp.concatenate(parts, axis=-1)) * gain
    half = SWA_HEAD_DIM // 2
    return (xn * cos + pltpu.roll(xn, half, 1) * sin_a
            + pltpu.roll(xn, width - half, 1) * sin_b)


def _swa_kernel(sinks_ref, q_ref, kc_ref, kp_ref, vc_ref, vp_ref, cosc_ref, sac_ref, sbc_ref,
                cosp_ref, sap_ref, sbp_ref, qn_ref, kn_ref, bd_ref, o_ref):
    n = pl.program_id(1)
    bd = bd_ref[...]
    rep = SWA_Q // LANES
    tile = lambda t: jnp.concatenate([t] * rep, axis=-1)
    cos_c, sa_c, sb_c = cosc_ref[...], sac_ref[...], sbc_ref[...]
    q = _norm_rope(q_ref[...].astype(F32), qn_ref[...], tile(cos_c), tile(sa_c), tile(sb_c), bd)
    q = (q * (SWA_HEAD_DIM ** -0.5)).astype(BF16)
    k_c = _norm_rope(kc_ref[...].astype(F32), kn_ref[...], cos_c, sa_c, sb_c, bd)
    k_p = _norm_rope(kp_ref[...].astype(F32), kn_ref[...], cosp_ref[...], sap_ref[...],
                     sbp_ref[...], bd)
    k = jnp.concatenate([k_p, k_c], axis=0).astype(BF16)
    v = jnp.concatenate([vp_ref[...], vc_ref[...]], axis=0)

    gq = SWA_GROUP * BLOCK
    r = lax.broadcasted_iota(jnp.int32, (gq, 2 * BLOCK), 0) & (BLOCK - 1)
    cidx = lax.broadcasted_iota(jnp.int32, (gq, 2 * BLOCK), 1)
    allowed = (cidx > r) & (cidx <= r + SWA_WINDOW) & ((n - 1) * BLOCK + cidx >= FRONT)
    rgroup = lax.broadcasted_iota(jnp.int32, (gq, 1), 0) // BLOCK

    for hk in range(SWA_KV_HEADS):
        ks = slice(hk * SWA_HEAD_DIM, (hk + 1) * SWA_HEAD_DIM)
        qg = jnp.concatenate(
            [q[:, (hk * SWA_GROUP + g) * SWA_HEAD_DIM:(hk * SWA_GROUP + g + 1) * SWA_HEAD_DIM]
             for g in range(SWA_GROUP)], axis=0)
        s = jnp.where(allowed, _dot_nt(qg, k[:, ks]), NEG_INF)
        sink = jnp.zeros((gq, 1), F32)
        for g in range(SWA_GROUP):
            sink = jnp.where(rgroup == g, sinks_ref[hk * SWA_GROUP + g], sink)
        m = jnp.maximum(jnp.max(s, axis=-1, keepdims=True), sink)
        p = jnp.exp(s - m)
        denom = jnp.sum(p, axis=-1, keepdims=True) + jnp.exp(sink - m)
        o = _dot(p.astype(BF16), v[:, ks]) / denom
        for g in range(SWA_GROUP):
            hq = hk * SWA_GROUP + g
            o_ref[:, hq * SWA_HEAD_DIM:(hq + 1) * SWA_HEAD_DIM] = (
                o[g * BLOCK:(g + 1) * BLOCK].astype(o_ref.dtype))


def _swa(z, sinks, cos, sin_a, sin_b, qn, kn, bd, batch, nb):
    rows = z.shape[0]

    def cur(width, start):
        return pl.BlockSpec((BLOCK, width), lambda b, n, s: (b * nb + n, start // width))

    def prev(width, start):
        return pl.BlockSpec((BLOCK, width),
                            lambda b, n, s: (b * nb + jnp.maximum(n - 1, 0), start // width))

    tab_c = pl.BlockSpec((BLOCK, LANES), lambda b, n, s: (n, 0))
    tab_p = pl.BlockSpec((BLOCK, LANES), lambda b, n, s: (jnp.maximum(n - 1, 0), 0))
    const = lambda shape: pl.BlockSpec(shape, lambda b, n, s: (0,) * len(shape))
    grid_spec = pltpu.PrefetchScalarGridSpec(
        num_scalar_prefetch=1,
        grid=(batch, nb),
        in_specs=[cur(SWA_Q, Z_SQ), cur(SWA_KV, Z_SK), prev(SWA_KV, Z_SK),
                  cur(SWA_KV, Z_SV), prev(SWA_KV, Z_SV),
                  tab_c, tab_c, tab_c, tab_p, tab_p, tab_p,
                  const((1, SWA_Q)), const((1, SWA_KV)), const((LANES, LANES))],
        out_specs=pl.BlockSpec((BLOCK, SWA_Q), lambda b, n, s: (b * nb + n, 0)),
    )
    return pl.pallas_call(
        _swa_kernel,
        out_shape=jax.ShapeDtypeStruct((rows, SWA_Q), BF16),
        grid_spec=grid_spec,
        compiler_params=_params("parallel", "arbitrary"),
        name="swa",
    )(sinks, z, z, z, z, z, cos, sin_a, sin_b, cos, sin_a, sin_b, qn, kn, bd)


def _ffn_kernel(h_ref, oa_ref, ob_ref, woa_ref, wob_ref, g_ref, w1_ref, w3_ref, w2_ref,
                out_ref, h1_sc, xn_sc, acc_sc):
    j = pl.program_id(1)

    @pl.when(j == 0)
    def _():
        h1 = h_ref[...] + _dot(oa_ref[...], woa_ref[...]) + _dot(ob_ref[...], wob_ref[...])
        h1_sc[...] = h1
        xn_sc[...] = _rms(h1, g_ref[...]).astype(BF16)
        acc_sc[...] = jnp.zeros_like(acc_sc)

    x = xn_sc[...]
    a = _dot(x, w1_ref[...])
    b = _dot(x, w3_ref[...])
    hid = (a * jax.nn.sigmoid(a) * b).astype(BF16)
    acc_sc[...] += _dot(hid, w2_ref[...])

    @pl.when(j == pl.num_programs(1) - 1)
    def _():
        out_ref[...] = h1_sc[...] + acc_sc[...]


def _mix_out_ffn(h, oa, ob, woa, wob, g, w1, w3, w2, tm, tf):
    rows = h.shape[0]
    dff = w1.shape[1]
    return pl.pallas_call(
        _ffn_kernel,
        out_shape=jax.ShapeDtypeStruct((rows, D_MODEL), F32),
        grid=(rows // tm, dff // tf),
        in_specs=[pl.BlockSpec((tm, D_MODEL), lambda i, j: (i, 0)),
                  pl.BlockSpec((tm, GLA_V), lambda i, j: (i, 0)),
                  pl.BlockSpec((tm, SWA_Q), lambda i, j: (i, 0)),
                  pl.BlockSpec((GLA_V, D_MODEL), lambda i, j: (0, 0)),
                  pl.BlockSpec((SWA_Q, D_MODEL), lambda i, j: (0, 0)),
                  pl.BlockSpec((1, D_MODEL), lambda i, j: (0, 0)),
                  pl.BlockSpec((D_MODEL, tf), lambda i, j: (0, j)),
                  pl.BlockSpec((D_MODEL, tf), lambda i, j: (0, j)),
                  pl.BlockSpec((tf, D_MODEL), lambda i, j: (j, 0))],
        out_specs=pl.BlockSpec((tm, D_MODEL), lambda i, j: (i, 0)),
        scratch_shapes=[pltpu.VMEM((tm, D_MODEL), F32), pltpu.VMEM((tm, D_MODEL), BF16),
                        pltpu.VMEM((tm, D_MODEL), F32)],
        compiler_params=_params("parallel", "arbitrary"),
        name="mix_out_ffn",
    )(h, oa, ob, woa, wob, g, w1, w3, w2)


def _glu_kernel(h_ref, g_ref, wa_ref, wg_ref, u_ref):
    xn = _rms(h_ref[...], g_ref[...]).astype(BF16)
    a = _dot(xn, wa_ref[...])
    gate = _dot(xn, wg_ref[...])
    u_ref[...] = (a * jax.nn.sigmoid(gate)).astype(u_ref.dtype)


def _conv_glu(h, g, w_pw1, tm):
    rows = h.shape[0]
    return pl.pallas_call(
        _glu_kernel,
        out_shape=jax.ShapeDtypeStruct((rows, D_MODEL), BF16),
        grid=(rows // tm,),
        in_specs=[pl.BlockSpec((tm, D_MODEL), lambda i: (i, 0)),
                  pl.BlockSpec((1, D_MODEL), lambda i: (0, 0)),
                  pl.BlockSpec((D_MODEL, D_MODEL), lambda i: (0, 0)),
                  pl.BlockSpec((D_MODEL, D_MODEL), lambda i: (0, 1))],
        out_specs=pl.BlockSpec((tm, D_MODEL), lambda i: (i, 0)),
        compiler_params=_params("parallel"),
        name="conv_glu",
    )(h, g, w_pw1, w_pw1)


CONV_HALO = 32
CONV_ROWS = 32


def _conv_kernel(uc_ref, up_ref, h_ref, wdw_ref, bdw_ref, lg_ref, lb_ref, w2_ref, out_ref,
                 x_sc, y_sc):
    n = pl.program_id(1)
    prev = jnp.where(n > 0, up_ref[...].astype(F32), 0.0)
    x_sc[0:CONV_HALO, :] = prev
    x_sc[CONV_HALO:CONV_HALO + BLOCK, :] = uc_ref[...].astype(F32)
    first = CONV_HALO - (CONV_WIDTH - 1)
    for rb in range(BLOCK // CONV_ROWS):
        acc = jnp.broadcast_to(bdw_ref[...], (CONV_ROWS, D_MODEL))
        for j in range(CONV_WIDTH):
            start = first + j + rb * CONV_ROWS
            acc = acc + x_sc[start:start + CONV_ROWS, :] * wdw_ref[j:j + 1, :]
        mu = jnp.mean(acc, axis=-1, keepdims=True)
        dev = acc - mu
        var = jnp.mean(dev * dev, axis=-1, keepdims=True)
        y = dev * lax.rsqrt(var + EPS) * lg_ref[...] + lb_ref[...]
        y_sc[rb * CONV_ROWS:(rb + 1) * CONV_ROWS, :] = (y * jax.nn.sigmoid(y)).astype(BF16)
    out_ref[...] = h_ref[...] + _dot(y_sc[...], w2_ref[...])


def _conv_module(u, h, wdw, bdw, lg, lb, w2, batch, nb):
    rows = h.shape[0]
    per = BLOCK // CONV_HALO
    blk = lambda b, n: (b * nb + n, 0)
    const = lambda shape: pl.BlockSpec(shape, lambda b, n: (0,) * len(shape))
    return pl.pallas_call(
        _conv_kernel,
        out_shape=jax.ShapeDtypeStruct((rows, D_MODEL), F32),
        grid=(batch, nb),
        in_specs=[pl.BlockSpec((BLOCK, D_MODEL), blk),
                  pl.BlockSpec((CONV_HALO, D_MODEL),
                               lambda b, n: (jnp.maximum((b * nb + n) * per - 1, 0), 0)),
                  pl.BlockSpec((BLOCK, D_MODEL), blk),
                  const((CONV_WIDTH, D_MODEL)), const((1, D_MODEL)), const((1, D_MODEL)),
                  const((1, D_MODEL)), const((D_MODEL, D_MODEL))],
        out_specs=pl.BlockSpec((BLOCK, D_MODEL), blk),
        scratch_shapes=[pltpu.VMEM((CONV_HALO + BLOCK, D_MODEL), F32),
                        pltpu.VMEM((BLOCK, D_MODEL), BF16)],
        compiler_params=_params("parallel", "arbitrary"),
        name="conv_module",
    )(u, u, h, wdw, bdw, lg, lb, w2)


def _router_kernel(h_ref, g_ref, wr_hi_ref, wr_lo_ref, xn_ref, gate_ref):
    xn = _rms(h_ref[...], g_ref[...])
    x_hi = xn.astype(BF16)
    x_lo = (xn - x_hi.astype(F32)).astype(BF16)
    xn_ref[...] = x_hi
    logits = (_dot(x_hi, wr_hi_ref[...]) + _dot(x_lo, wr_hi_ref[...])
              + _dot(x_hi, wr_lo_ref[...]))
    lane = lax.broadcasted_iota(jnp.int32, logits.shape, 1)
    logits = jnp.where(lane < N_EXPERTS, logits, -jnp.inf)
    m1 = jnp.max(logits, axis=-1, keepdims=True)
    i1 = jnp.min(jnp.where(logits == m1, lane, LANES), axis=-1, keepdims=True)
    rest = jnp.where(lane == i1, -jnp.inf, logits)
    m2 = jnp.max(rest, axis=-1, keepdims=True)
    i2 = jnp.min(jnp.where(rest == m2, lane, LANES), axis=-1, keepdims=True)
    e2 = jnp.exp(m2 - m1)
    g1 = 1.0 / (1.0 + e2)
    g2 = e2 / (1.0 + e2)
    gate_ref[...] = jnp.where(lane == i1, g1, jnp.where(lane == i2, g2, 0.0))


def _router(h, g, wr_hi, wr_lo, tm):
    rows = h.shape[0]
    return pl.pallas_call(
        _router_kernel,
        out_shape=(jax.ShapeDtypeStruct((rows, D_MODEL), BF16),
                   jax.ShapeDtypeStruct((rows, LANES), F32)),
        grid=(rows // tm,),
        in_specs=[pl.BlockSpec((tm, D_MODEL), lambda i: (i, 0)),
                  pl.BlockSpec((1, D_MODEL), lambda i: (0, 0)),
                  pl.BlockSpec((D_MODEL, LANES), lambda i: (0, 0)),
                  pl.BlockSpec((D_MODEL, LANES), lambda i: (0, 0))],
        out_specs=(pl.BlockSpec((tm, D_MODEL), lambda i: (i, 0)),
                   pl.BlockSpec((tm, LANES), lambda i: (i, 0))),
        compiler_params=_params("parallel"),
        name="router",
    )(h, g, wr_hi, wr_lo)


def _moe_dense_kernel(h_ref, x_ref, gate_ref, w1_ref, w3_ref, w2_ref, out_ref, acc_sc):
    e = pl.program_id(1)
    j = pl.program_id(2)

    @pl.when((e == 0) & (j == 0))
    def _():
        acc_sc[...] = jnp.zeros_like(acc_sc)

    gates = gate_ref[...]
    lane = lax.broadcasted_iota(jnp.int32, gates.shape, 1)
    ge = jnp.sum(jnp.where(lane == e, gates, 0.0), axis=-1, keepdims=True)
    x = x_ref[...]
    a = _dot(x, w1_ref[0])
    b = _dot(x, w3_ref[0])
    hid = (a * jax.nn.sigmoid(a) * b * ge).astype(BF16)
    acc_sc[...] += _dot(hid, w2_ref[0])

    @pl.when((e == pl.num_programs(1) - 1) & (j == pl.num_programs(2) - 1))
    def _():
        out_ref[...] = h_ref[...] + acc_sc[...]


def _moe_dense(h, xn, gates, w1, w3, w2, tm, tf):
    rows = h.shape[0]
    ne, _, dff = w1.shape
    return pl.pallas_call(
        _moe_dense_kernel,
        out_shape=jax.ShapeDtypeStruct((rows, D_MODEL), F32),
        grid=(rows // tm, ne, dff // tf),
        in_specs=[pl.BlockSpec((tm, D_MODEL), lambda i, e, j: (i, 0)),
                  pl.BlockSpec((tm, D_MODEL), lambda i, e, j: (i, 0)),
                  pl.BlockSpec((tm, LANES), lambda i, e, j: (i, 0)),
                  pl.BlockSpec((1, D_MODEL, tf), lambda i, e, j: (e, 0, j)),
                  pl.BlockSpec((1, D_MODEL, tf), lambda i, e, j: (e, 0, j)),
                  pl.BlockSpec((1, tf, D_MODEL), lambda i, e, j: (e, j, 0))],
        out_specs=pl.BlockSpec((tm, D_MODEL), lambda i, e, j: (i, 0)),
        scratch_shapes=[pltpu.VMEM((tm, D_MODEL), F32)],
        compiler_params=_params("parallel", "arbitrary", "arbitrary"),
        name="moe_dense",
    )(h, xn, gates, w1, w3, w2)


def _row_tile(rows, want):
    tm = want
    while rows % tm:
        tm //= 2
    return tm


def _rope_tables(lp):
    inv_freq = 1.0 / (ROPE_THETA ** (jnp.arange(0, SWA_HEAD_DIM, 2, dtype=F32) / SWA_HEAD_DIM))
    pos = jnp.arange(lp, dtype=F32) - FRONT
    ang = pos[:, None] * inv_freq[None, :]
    cos, sin = jnp.cos(ang), jnp.sin(ang)
    zero = jnp.zeros_like(sin)
    rep = LANES // SWA_HEAD_DIM
    cos_t = jnp.tile(jnp.concatenate([cos, cos], axis=-1), (1, rep))
    sin_a = jnp.tile(jnp.concatenate([zero, sin], axis=-1), (1, rep))
    sin_b = jnp.tile(jnp.concatenate([-sin, zero], axis=-1), (1, rep))
    return cos_t, sin_a, sin_b


def kernel(x, meta, a_norm, a_w_in, a_w_gate2, a_b_gate, a_q_norm, a_k_norm, a_sinks, a_o_norm,
           a_w_out, f_norm, f_w1, f_w3, f_w2, c_norm, c_w_pw1, c_w_dw, c_b_dw, c_ln_g, c_ln_b,
           c_w_pw2, m_norm, m_w_router, m_w1, m_w3, m_w2):
    batch, seq, _ = x.shape
    assert seq % BLOCK == 0
    lp = BLOCK + seq
    nb = lp // BLOCK
    rows = batch * lp

    head = jnp.concatenate([jnp.zeros((FRONT, D_MODEL), x.dtype), meta.astype(x.dtype)], axis=0)
    h = jnp.concatenate([jnp.broadcast_to(head[None], (batch, BLOCK, D_MODEL)), x], axis=1)
    h = h.reshape(rows, D_MODEL)

    w_in = a_w_in[0]
    gq, gk, gv, gr, glr, sq, sk, sv = jnp.split(
        w_in, [256, 512, 1024, 1536, 1552, 2064, 2192], axis=-1)
    w_in_r = jnp.concatenate(
        [gq, gk, gv, gr, sq, sk, sv, glr,
         jnp.zeros((D_MODEL, Z_COLS - Z_LR - GLA_LOWRANK), w_in.dtype)], axis=-1).astype(BF16)
    z = _in_proj(h, a_norm[0][None], w_in_r, _row_tile(rows, 512))

    wg = jnp.zeros((LANES, GLA_QK), F32).at[:GLA_LOWRANK].set(a_w_gate2[0]).astype(BF16)
    o_a = _gla(z, wg, a_b_gate[0][None], a_o_norm[0][None], batch, nb)

    cos_t, sin_a, sin_b = _rope_tables(lp)
    qn = jnp.tile(a_q_norm[0], SWA_Q_HEADS)[None]
    kn = jnp.tile(a_k_norm[0], SWA_KV_HEADS)[None]
    lane = jnp.arange(LANES)
    bd = (lane[:, None] // SWA_HEAD_DIM == lane[None, :] // SWA_HEAD_DIM).astype(BF16)
    o_b = _swa(z, a_sinks[0], cos_t, sin_a, sin_b, qn, kn, bd, batch, nb)

    w_out = a_w_out[0].astype(BF16)
    tm = _row_tile(rows, 1024)
    h = _mix_out_ffn(h, o_a, o_b, w_out[:GLA_V], w_out[GLA_V:], f_norm[0][None],
                     f_w1[0].astype(BF16), f_w3[0].astype(BF16), f_w2[0].astype(BF16), tm, 512)

    u = _conv_glu(h, c_norm[0][None], c_w_pw1[0].astype(BF16), _row_tile(rows, 512))
    h = _conv_module(u, h, c_w_dw[0], c_b_dw[0][None], c_ln_g[0][None], c_ln_b[0][None],
                     c_w_pw2[0].astype(BF16), batch, nb)

    wr = jnp.zeros((D_MODEL, LANES), F32).at[:, :N_EXPERTS].set(m_w_router[0])
    wr_hi = wr.astype(BF16)
    wr_lo = (wr - wr_hi.astype(F32)).astype(BF16)
    xn, gates = _router(h, m_norm[0][None], wr_hi, wr_lo, _row_tile(rows, 512))
    h = _moe_dense(h, xn, gates, m_w1[0].astype(BF16), m_w3[0].astype(BF16),
                   m_w2[0].astype(BF16), tm, 512)
    return h.reshape(batch, lp, D_MODEL)[:, BLOCK:]
```

```python
import functools
import math

import jax
import jax.numpy as jnp
from jax import lax
from jax.experimental import pallas as pl
from jax.experimental.pallas import tpu as pltpu

F32 = jnp.float32
BF16 = jnp.bfloat16

D_MODEL = 1024
N_META = 16
EPS = 1e-6
ROPE_THETA = 10000.0
NEG_INF = -1e30
GLA_HEADS = 4
GLA_DK = 64
GLA_DV = 128
GLA_LOWRANK = 16
GLA_TAU = 16.0
GLA_CHUNK = 64
SWA_Q_HEADS = 8
SWA_KV_HEADS = 2
SWA_HEAD_DIM = 64
SWA_WINDOW = 128
CONV_WIDTH = 31
D_FF = 3584
N_EXPERTS = 8

GLA_QK = GLA_HEADS * GLA_DK
GLA_V = GLA_HEADS * GLA_DV
SWA_Q = SWA_Q_HEADS * SWA_HEAD_DIM
SWA_KV = SWA_KV_HEADS * SWA_HEAD_DIM
SWA_GROUP = SWA_Q_HEADS // SWA_KV_HEADS

LANES = 128
BLOCK = 128
FRONT = BLOCK - N_META
VMEM_LIMIT = 56 * 1024 * 1024

Z_Q, Z_K, Z_V, Z_R = 0, 256, 512, 1024
Z_SQ, Z_SK, Z_SV, Z_LR = 1536, 2048, 2176, 2304
Z_COLS = 2432


def _rms(x, g):
    return x * lax.rsqrt(jnp.mean(x * x, axis=-1, keepdims=True) + EPS) * g


def _dot(a, b):
    return jnp.dot(a, b, preferred_element_type=F32)


def _dot_nt(a, b):
    return lax.dot_general(a, b, (((1,), (1,)), ((), ())), preferred_element_type=F32)


def _dot_tn(a, b):
    return lax.dot_general(a, b, (((0,), (0,)), ((), ())), preferred_element_type=F32)


def _params(*sem):
    return pltpu.CompilerParams(dimension_semantics=sem, vmem_limit_bytes=VMEM_LIMIT)


def _in_proj_kernel(h_ref, g_ref, w_ref, z_ref):
    xn = _rms(h_ref[...], g_ref[...]).astype(BF16)
    z_ref[...] = _dot(xn, w_ref[...]).astype(z_ref.dtype)


def _in_proj(h, g, w, tm):
    rows = h.shape[0]
    n = w.shape[1]
    return pl.pallas_call(
        _in_proj_kernel,
        out_shape=jax.ShapeDtypeStruct((rows, n), BF16),
        grid=(rows // tm,),
        in_specs=[pl.BlockSpec((tm, D_MODEL), lambda i: (i, 0)),
                  pl.BlockSpec((1, D_MODEL), lambda i: (0, 0)),
                  pl.BlockSpec((D_MODEL, n), lambda i: (0, 0))],
        out_specs=pl.BlockSpec((tm, n), lambda i: (i, 0)),
        compiler_params=_params("parallel"),
        name="in_proj",
    )(h, g, w)


def _gla_kernel(q_ref, k_ref, v_ref, r_ref, lr_ref, wg_ref, bg_ref, on_ref, o_ref, st_ref):
    n = pl.program_id(1)

    @pl.when(n == 0)
    def _():
        st_ref[...] = jnp.zeros_like(st_ref)

    c = GLA_CHUNK
    gp = _dot(lr_ref[...], wg_ref[...]) + bg_ref[...]
    g = (jnp.minimum(gp, 0.0) - jnp.log(1.0 + jnp.exp(-jnp.abs(gp)))) * (1.0 / GLA_TAU)
    row = lax.broadcasted_iota(jnp.int32, g.shape, 0) + n * BLOCK
    g = jnp.where(row >= FRONT, g, 0.0)

    ri = lax.broadcasted_iota(jnp.int32, (c, c), 0)
    ci = lax.broadcasted_iota(jnp.int32, (c, c), 1)
    causal = ri >= ci
    tri = jnp.where(causal, 1.0, 0.0).astype(BF16)
    on = on_ref[...]

    for ch in range(BLOCK // c):
        rs = slice(ch * c, (ch + 1) * c)
        gc = g[rs]
        g_hi = gc.astype(BF16)
        g_lo = (gc - g_hi.astype(F32)).astype(BF16)
        b = _dot(tri, g_hi) + _dot(tri, g_lo)
        b_last = b[c - 1:c, :]
        qf = q_ref[rs, :].astype(F32)
        kf = k_ref[rs, :].astype(F32)
        q_t = (qf * jnp.exp(b) * (GLA_DK ** -0.5)).astype(BF16)
        k_t = (kf * jnp.exp(-b)).astype(BF16)
        k_d = (kf * jnp.exp(b_last - b)).astype(BF16)
        dec = jnp.exp(b_last)
        for hd in range(GLA_HEADS):
            ks = slice(hd * GLA_DK, (hd + 1) * GLA_DK)
            vs = slice(hd * GLA_DV, (hd + 1) * GLA_DV)
            qh = q_t[:, ks]
            att = jnp.where(causal, _dot_nt(qh, k_t[:, ks]), 0.0)
            vh = v_ref[rs, vs]
            st = st_ref[hd]
            o = _dot(att.astype(BF16), vh) + _dot_nt(qh, st.astype(BF16))
            st_ref[hd] = st * dec[:, ks] + _dot_tn(vh, k_d[:, ks])
            o = _rms(o, on)
            r = r_ref[rs, vs].astype(F32)
            o_ref[rs, vs] = (o * (r * jax.nn.sigmoid(r))).astype(o_ref.dtype)


def _gla(z, wg, bg, on, batch, nb):
    rows = z.shape[0]

    def col(width, start):
        return pl.BlockSpec((BLOCK, width), lambda b, n: (b * nb + n, start // width))

    const = lambda shape: pl.BlockSpec(shape, lambda b, n: (0,) * len(shape))
    return pl.pallas_call(
        _gla_kernel,
        out_shape=jax.ShapeDtypeStruct((rows, GLA_V), BF16),
        grid=(batch, nb),
        in_specs=[col(GLA_QK, Z_Q), col(GLA_QK, Z_K), col(GLA_V, Z_V), col(GLA_V, Z_R),
                  col(LANES, Z_LR), const((LANES, GLA_QK)), const((1, GLA_QK)),
                  const((1, GLA_DV))],
        out_specs=pl.BlockSpec((BLOCK, GLA_V), lambda b, n: (b * nb + n, 0)),
        scratch_shapes=[pltpu.VMEM((GLA_HEADS, GLA_DV, GLA_DK), F32)],
        compiler_params=_params("parallel", "arbitrary"),
        name="gla",
    )(z, z, z, z, z, wg, bg, on)


def _group_mean_sq(x, bd):
    sq = x * x
    hi = sq.astype(BF16)
    lo = (sq - hi.astype(F32)).astype(BF16)
    return _dot(hi, bd) + _dot(lo, bd)


def _norm_rope(x, gain, cos, sin_a, sin_b, bd):
    width = x.shape[-1]
    parts = []
    for s in range(width // LANES):
        xs = x[:, s * LANES:(s + 1) * LANES]
        ms = _group_mean_sq(xs, bd) * (1.0 / SWA_HEAD_DIM)
        parts.append(xs * lax.rsqrt(ms + EPS))
    xn = (parts[0] if len(parts) == 1 else jnp.concatenate(parts, axis=-1)) * gain
    half = SWA_HEAD_DIM // 2
    return (xn * cos + pltpu.roll(xn, half, 1) * sin_a
            + pltpu.roll(xn, width - half, 1) * sin_b)


def _swa_kernel(sinks_ref, q_ref, kc_ref, kp_ref, vc_ref, vp_ref, cosc_ref, sac_ref, sbc_ref,
                cosp_ref, sap_ref, sbp_ref, qn_ref, kn_ref, bd_ref, o_ref):
    n = pl.program_id(1)
    bd = bd_ref[...]
    rep = SWA_Q // LANES
    tile = lambda t: jnp.concatenate([t] * rep, axis=-1)
    cos_c, sa_c, sb_c = cosc_ref[...], sac_ref[...], sbc_ref[...]
    q = _norm_rope(q_ref[...].astype(F32), qn_ref[...], tile(cos_c), tile(sa_c), tile(sb_c), bd)
    q = (q * (SWA_HEAD_DIM ** -0.5)).astype(BF16)
    k_c = _norm_rope(kc_ref[...].astype(F32), kn_ref[...], cos_c, sa_c, sb_c, bd)
    k_p = _norm_rope(kp_ref[...].astype(F32), kn_ref[...], cosp_ref[...], sap_ref[...],
                     sbp_ref[...], bd)
    k = jnp.concatenate([k_p, k_c], axis=0).astype(BF16)
    v = jnp.concatenate([vp_ref[...], vc_ref[...]], axis=0)

    gq = SWA_GROUP * BLOCK
    r = lax.broadcasted_iota(jnp.int32, (gq, 2 * BLOCK), 0) & (BLOCK - 1)
    cidx = lax.broadcasted_iota(jnp.int32, (gq, 2 * BLOCK), 1)
    allowed = (cidx > r) & (cidx <= r + SWA_WINDOW) & ((n - 1) * BLOCK + cidx >= FRONT)
    rgroup = lax.broadcasted_iota(jnp.int32, (gq, 1), 0) // BLOCK

    for hk in range(SWA_KV_HEADS):
        ks = slice(hk * SWA_HEAD_DIM, (hk + 1) * SWA_HEAD_DIM)
        qg = jnp.concatenate(
            [q[:, (hk * SWA_GROUP + g) * SWA_HEAD_DIM:(hk * SWA_GROUP + g + 1) * SWA_HEAD_DIM]
             for g in range(SWA_GROUP)], axis=0)
        s = jnp.where(allowed, _dot_nt(qg, k[:, ks]), NEG_INF)
        sink = jnp.zeros((gq, 1), F32)
        for g in range(SWA_GROUP):
            sink = jnp.where(rgroup == g, sinks_ref[hk * SWA_GROUP + g], sink)
        m = jnp.maximum(jnp.max(s, axis=-1, keepdims=True), sink)
        p = jnp.exp(s - m)
        denom = jnp.sum(p, axis=-1, keepdims=True) + jnp.exp(sink - m)
        o = _dot(p.astype(BF16), v[:, ks]) / denom
        for g in range(SWA_GROUP):
            hq = hk * SWA_GROUP + g
            o_ref[:, hq * SWA_HEAD_DIM:(hq + 1) * SWA_HEAD_DIM] = (
                o[g * BLOCK:(g + 1) * BLOCK].astype(o_ref.dtype))


def _swa(z, sinks, cos, sin_a, sin_b, qn, kn, bd, batch, nb):
    rows = z.shape[0]

    def cur(width, start):
        return pl.BlockSpec((BLOCK, width), lambda b, n, s: (b * nb + n, start // width))

    def prev(width, start):
        return pl.BlockSpec((BLOCK, width),
                            lambda b, n, s: (b * nb + jnp.maximum(n - 1, 0), start // width))

    tab_c = pl.BlockSpec((BLOCK, LANES), lambda b, n, s: (n, 0))
    tab_p = pl.BlockSpec((BLOCK, LANES), lambda b, n, s: (jnp.maximum(n - 1, 0), 0))
    const = lambda shape: pl.BlockSpec(shape, lambda b, n, s: (0,) * len(shape))
    grid_spec = pltpu.PrefetchScalarGridSpec(
        num_scalar_prefetch=1,
        grid=(batch, nb),
        in_specs=[cur(SWA_Q, Z_SQ), cur(SWA_KV, Z_SK), prev(SWA_KV, Z_SK),
                  cur(SWA_KV, Z_SV), prev(SWA_KV, Z_SV),
                  tab_c, tab_c, tab_c, tab_p, tab_p, tab_p,
                  const((1, SWA_Q)), const((1, SWA_KV)), const((LANES, LANES))],
        out_specs=pl.BlockSpec((BLOCK, SWA_Q), lambda b, n, s: (b * nb + n, 0)),
    )
    return pl.pallas_call(
        _swa_kernel,
        out_shape=jax.ShapeDtypeStruct((rows, SWA_Q), BF16),
        grid_spec=grid_spec,
        compiler_params=_params("parallel", "arbitrary"),
        name="swa",
    )(sinks, z, z, z, z, z, cos, sin_a, sin_b, cos, sin_a, sin_b, qn, kn, bd)


def _ffn_kernel(h_ref, oa_ref, ob_ref, woa_ref, wob_ref, g_ref, w1_ref, w3_ref, w2_ref,
                out_ref, h1_sc, xn_sc, acc_sc):
    j = pl.program_id(1)

    @pl.when(j == 0)
    def _():
        h1 = h_ref[...] + _dot(oa_ref[...], woa_ref[...]) + _dot(ob_ref[...], wob_ref[...])
        h1_sc[...] = h1
        xn_sc[...] = _rms(h1, g_ref[...]).astype(BF16)
        acc_sc[...] = jnp.zeros_like(acc_sc)

    x = xn_sc[...]
    a = _dot(x, w1_ref[...])
    b = _dot(x, w3_ref[...])
    hid = (a * jax.nn.sigmoid(a) * b).astype(BF16)
    acc_sc[...] += _dot(hid, w2_ref[...])

    @pl.when(j == pl.num_programs(1) - 1)
    def _():
        out_ref[...] = h1_sc[...] + acc_sc[...]


def _mix_out_ffn(h, oa, ob, woa, wob, g, w1, w3, w2, tm, tf):
    rows = h.shape[0]
    dff = w1.shape[1]
    return pl.pallas_call(
        _ffn_kernel,
        out_shape=jax.ShapeDtypeStruct((rows, D_MODEL), F32),
        grid=(rows // tm, dff // tf),
        in_specs=[pl.BlockSpec((tm, D_MODEL), lambda i, j: (i, 0)),
                  pl.BlockSpec((tm, GLA_V), lambda i, j: (i, 0)),
                  pl.BlockSpec((tm, SWA_Q), lambda i, j: (i, 0)),
                  pl.BlockSpec((GLA_V, D_MODEL), lambda i, j: (0, 0)),
                  pl.BlockSpec((SWA_Q, D_MODEL), lambda i, j: (0, 0)),
                  pl.BlockSpec((1, D_MODEL), lambda i, j: (0, 0)),
                  pl.BlockSpec((D_MODEL, tf), lambda i, j: (0, j)),
                  pl.BlockSpec((D_MODEL, tf), lambda i, j: (0, j)),
                  pl.BlockSpec((tf, D_MODEL), lambda i, j: (j, 0))],
        out_specs=pl.BlockSpec((tm, D_MODEL), lambda i, j: (i, 0)),
        scratch_shapes=[pltpu.VMEM((tm, D_MODEL), F32), pltpu.VMEM((tm, D_MODEL), BF16),
                        pltpu.VMEM((tm, D_MODEL), F32)],
        compiler_params=_params("parallel", "arbitrary"),
        name="mix_out_ffn",
    )(h, oa, ob, woa, wob, g, w1, w3, w2)


def _glu_kernel(h_ref, g_ref, wa_ref, wg_ref, u_ref):
    xn = _rms(h_ref[...], g_ref[...]).astype(BF16)
    a = _dot(xn, wa_ref[...])
    gate = _dot(xn, wg_ref[...])
    u_ref[...] = (a * jax.nn.sigmoid(gate)).astype(u_ref.dtype)


def _conv_glu(h, g, w_pw1, tm):
    rows = h.shape[0]
    return pl.pallas_call(
        _glu_kernel,
        out_shape=jax.ShapeDtypeStruct((rows, D_MODEL), BF16),
        grid=(rows // tm,),
        in_specs=[pl.BlockSpec((tm, D_MODEL), lambda i: (i, 0)),
                  pl.BlockSpec((1, D_MODEL), lambda i: (0, 0)),
                  pl.BlockSpec((D_MODEL, D_MODEL), lambda i: (0, 0)),
                  pl.BlockSpec((D_MODEL, D_MODEL), lambda i: (0, 1))],
        out_specs=pl.BlockSpec((tm, D_MODEL), lambda i: (i, 0)),
        compiler_params=_params("parallel"),
        name="conv_glu",
    )(h, g, w_pw1, w_pw1)


CONV_HALO = 32
CONV_ROWS = 32


def _conv_kernel(uc_ref, up_ref, h_ref, wdw_ref, bdw_ref, lg_ref, lb_ref, w2_ref, out_ref,
                 x_sc, y_sc):
    @pl.when(pl.program_id(1) > 0)
    def _():
        x_sc[0:CONV_HALO, :] = up_ref[...].astype(F32)
        x_sc[CONV_HALO:CONV_HALO + BLOCK, :] = uc_ref[...].astype(F32)
        first = CONV_HALO - (CONV_WIDTH - 1)
        for rb in range(BLOCK // CONV_ROWS):
            acc = jnp.broadcast_to(bdw_ref[...], (CONV_ROWS, D_MODEL))
            for j in range(CONV_WIDTH):
                start = first + j + rb * CONV_ROWS
                acc = acc + x_sc[start:start + CONV_ROWS, :] * wdw_ref[j:j + 1, :]
            mu = jnp.mean(acc, axis=-1, keepdims=True)
            dev = acc - mu
            var = jnp.mean(dev * dev, axis=-1, keepdims=True)
            y = dev * lax.rsqrt(var + EPS) * lg_ref[...] + lb_ref[...]
            y_sc[rb * CONV_ROWS:(rb + 1) * CONV_ROWS, :] = (y * jax.nn.sigmoid(y)).astype(BF16)
        out_ref[...] = h_ref[...] + _dot(y_sc[...], w2_ref[...])


def _conv_module(u, h, wdw, bdw, lg, lb, w2, batch, nb):
    per = BLOCK // CONV_HALO
    blk = lambda b, n: (b * nb + n, 0)
    const = lambda shape: pl.BlockSpec(shape, lambda b, n: (0,) * len(shape))
    return pl.pallas_call(
        _conv_kernel,
        out_shape=jax.ShapeDtypeStruct((batch * (nb - 1) * BLOCK, D_MODEL), F32),
        grid=(batch, nb),
        in_specs=[pl.BlockSpec((BLOCK, D_MODEL), blk),
                  pl.BlockSpec((CONV_HALO, D_MODEL),
                               lambda b, n: (jnp.maximum((b * nb + n) * per - 1, 0), 0)),
                  pl.BlockSpec((BLOCK, D_MODEL), blk),
                  const((CONV_WIDTH, D_MODEL)), const((1, D_MODEL)), const((1, D_MODEL)),
                  const((1, D_MODEL)), const((D_MODEL, D_MODEL))],
        out_specs=pl.BlockSpec((BLOCK, D_MODEL),
                               lambda b, n: (b * (nb - 1) + jnp.maximum(n - 1, 0), 0)),
        scratch_shapes=[pltpu.VMEM((CONV_HALO + BLOCK, D_MODEL), F32),
                        pltpu.VMEM((BLOCK, D_MODEL), BF16)],
        compiler_params=_params("parallel", "arbitrary"),
        name="conv_module",
    )(u, u, h, wdw, bdw, lg, lb, w2)


SORT_TOKENS = 512
GROUP = 16
SORT_CAP = 2 * SORT_TOKENS + N_EXPERTS * GROUP
SORT_COLS = D_MODEL + LANES
MOE_ROWS = 512
GROUPS_PER_TILE = MOE_ROWS // GROUP


def _one_hot_rows(pos1, pos2, width):
    r = lax.broadcasted_iota(jnp.int32, (pos1.shape[0], width), 1).astype(F32)
    return jnp.where(r == pos1, 1.0, jnp.where(r == pos2, 1.0, 0.0)).astype(BF16)


def _route_sort_kernel(h_ref, g_ref, wr_hi_ref, wr_lo_ref, xs_ref, pos_ref, ng_ref):
    s = SORT_TOKENS
    xn = _rms(h_ref[...], g_ref[...])
    x_hi = xn.astype(BF16)
    x_lo = (xn - x_hi.astype(F32)).astype(BF16)
    logits = (_dot(x_hi, wr_hi_ref[...]) + _dot(x_lo, wr_hi_ref[...])
              + _dot(x_hi, wr_lo_ref[...]))
    lane = lax.broadcasted_iota(jnp.int32, logits.shape, 1)
    logits = jnp.where(lane < N_EXPERTS, logits, -jnp.inf)
    m1 = jnp.max(logits, axis=-1, keepdims=True)
    i1 = jnp.min(jnp.where(logits == m1, lane, LANES), axis=-1, keepdims=True)
    rest = jnp.where(lane == i1, -jnp.inf, logits)
    m2 = jnp.max(rest, axis=-1, keepdims=True)
    i2 = jnp.min(jnp.where(rest == m2, lane, LANES), axis=-1, keepdims=True)
    e2 = jnp.exp(m2 - m1)
    g1 = 1.0 / (1.0 + e2)
    g2 = e2 / (1.0 + e2)

    oh1 = lane == i1
    oh2 = lane == i2
    oh = jnp.where(oh1, 1.0, jnp.where(oh2, 1.0, 0.0))
    cnt = jnp.sum(oh, axis=0, keepdims=True)
    ngr = jnp.floor((cnt + (GROUP - 1)) * (1.0 / GROUP))
    ri = lax.broadcasted_iota(jnp.int32, (s, s), 0)
    ci = lax.broadcasted_iota(jnp.int32, (s, s), 1)
    rank = _dot(jnp.where(ri > ci, 1.0, 0.0).astype(BF16), oh.astype(BF16))
    li = lax.broadcasted_iota(jnp.int32, (LANES, LANES), 0)
    lj = lax.broadcasted_iota(jnp.int32, (LANES, LANES), 1)
    before = _dot(jnp.broadcast_to(ngr, (8, LANES)).astype(BF16),
                  jnp.where(li < lj, 1.0, 0.0).astype(BF16))[0:1]
    base = before * GROUP + rank
    pos1 = jnp.sum(jnp.where(oh1, base, 0.0), axis=-1, keepdims=True)
    pos2 = jnp.sum(jnp.where(oh2, base, 0.0), axis=-1, keepdims=True)

    r = lax.broadcasted_iota(jnp.int32, (s, SORT_CAP), 1).astype(F32)
    p1 = jnp.where(r == pos1, 1.0, 0.0).astype(BF16)
    p2 = jnp.where(r == pos2, 1.0, 0.0).astype(BF16)
    xs_ref[:, 0:D_MODEL] = _dot_tn(p1 + p2, x_hi).astype(BF16)

    def gate_lanes(gv):
        hi = gv.astype(BF16).astype(F32)
        return jnp.where(lane == 0, hi, jnp.where(lane == 1, gv - hi, 0.0)).astype(BF16)

    xs_ref[:, D_MODEL:SORT_COLS] = (_dot_tn(p1, gate_lanes(g1))
                                    + _dot_tn(p2, gate_lanes(g2))).astype(BF16)
    pos_ref[...] = jnp.where(lane == 0, pos1, jnp.where(lane == 1, pos2, 0.0))
    ng_ref[0] = jnp.broadcast_to(ngr, (8, LANES)).astype(jnp.int32)


def _route_sort(h, g, wr_hi, wr_lo):
    tokens = h.shape[0]
    nt = tokens // SORT_TOKENS
    return pl.pallas_call(
        _route_sort_kernel,
        out_shape=(jax.ShapeDtypeStruct((nt * SORT_CAP, SORT_COLS), BF16),
                   jax.ShapeDtypeStruct((tokens, LANES), F32),
                   jax.ShapeDtypeStruct((nt, 8, LANES), jnp.int32)),
        grid=(nt,),
        in_specs=[pl.BlockSpec((SORT_TOKENS, D_MODEL), lambda i: (i, 0)),
                  pl.BlockSpec((1, D_MODEL), lambda i: (0, 0)),
                  pl.BlockSpec((D_MODEL, LANES), lambda i: (0, 0)),
                  pl.BlockSpec((D_MODEL, LANES), lambda i: (0, 0))],
        out_specs=(pl.BlockSpec((SORT_CAP, SORT_COLS), lambda i: (i, 0)),
                   pl.BlockSpec((SORT_TOKENS, LANES), lambda i: (i, 0)),
                   pl.BlockSpec((1, 8, LANES), lambda i: (i, 0, 0))),
        compiler_params=_params("parallel"),
        name="route_sort",
    )(h, g, wr_hi, wr_lo)


def _group_tables(ng, n_row_tiles):
    nt, ne = ng.shape
    gpt = GROUPS_PER_TILE
    n_e = jnp.sum(ng, axis=0)
    rt_e = (n_e + gpt - 1) // gpt
    rt_end = jnp.cumsum(rt_e)
    n_rt = rt_end[-1]
    r = jnp.arange(n_row_tiles, dtype=jnp.int32)
    valid_r = r < n_rt
    e_r = jnp.minimum(jnp.sum((r[:, None] >= rt_end[None, :]).astype(jnp.int32), axis=1), ne - 1)
    e_last = jnp.sum(jnp.where(r == n_rt - 1, e_r, 0))
    e_r = jnp.where(valid_r, e_r, e_last)
    sel = (e_r[:, None] == jnp.arange(ne, dtype=jnp.int32)[None, :]).astype(jnp.int32)
    pick = lambda v: jnp.sum(sel * v[None, :], axis=1)
    k0 = (r - pick(rt_end - rt_e)) * gpt
    gidx = k0[:, None] + jnp.arange(gpt, dtype=jnp.int32)[None, :]
    valid = valid_r[:, None] & (gidx < pick(n_e)[:, None])
    incl = jnp.cumsum(ng, axis=0)
    incl_r = jnp.sum(sel[:, None, :] * incl[None, :, :], axis=2)
    ng_r = jnp.sum(sel[:, None, :] * ng[None, :, :], axis=2)
    lo = jnp.cumsum(ng, axis=1) - ng
    lo_r = jnp.sum(sel[:, None, :] * lo[None, :, :], axis=2)
    ti = jnp.sum((incl_r[:, None, :] <= gidx[:, :, None]).astype(jnp.int32), axis=2)
    ti = jnp.minimum(ti, nt - 1)
    tsel = (ti[:, :, None] == jnp.arange(nt, dtype=jnp.int32)[None, None, :]).astype(jnp.int32)
    tpick = lambda v: jnp.sum(tsel * v[:, None, :], axis=2)
    within = gidx - tpick(incl_r - ng_r)
    src = ti * SORT_CAP + GROUP * (tpick(lo_r) + within)
    src = jnp.where(valid, src, -1).reshape(-1)
    return e_r, valid_r.astype(jnp.int32), src


def _group_copies(src_ref, tile, hbm, buf, slot, sem, to_hbm, start):
    for k in range(GROUPS_PER_TILE):
        off = src_ref[tile * GROUPS_PER_TILE + k]

        @pl.when(off >= 0)
        def _():
            buf_rows = buf.at[slot, pl.ds(k * GROUP, GROUP), :]
            hbm_rows = hbm.at[pl.ds(pl.multiple_of(off, GROUP), GROUP),
                              pl.ds(0, buf_rows.shape[-1])]
            cp = (pltpu.make_async_copy(buf_rows, hbm_rows, sem.at[slot]) if to_hbm
                  else pltpu.make_async_copy(hbm_rows, buf_rows, sem.at[slot]))
            if start:
                cp.start()
            else:
                cp.wait()


def _moe_group_kernel(te_ref, tv_ref, src_ref, xs_hbm, w1_ref, w3_ref, w2_ref, ys_hbm,
                      xbuf, ybuf, acc_sc, xsem, ysem):
    r = pl.program_id(0)
    j = pl.program_id(1)
    nr = pl.num_programs(0)
    nj = pl.num_programs(1)
    slot = lax.rem(r, 2)
    gather = functools.partial(_group_copies, src_ref, hbm=xs_hbm, buf=xbuf, sem=xsem,
                               to_hbm=False)
    scatter = functools.partial(_group_copies, src_ref, hbm=ys_hbm, buf=ybuf, sem=ysem,
                                to_hbm=True)

    @pl.when(j == 0)
    def _():
        @pl.when(r == 0)
        def _():
            xbuf[...] = jnp.zeros_like(xbuf)
            gather(tile=r, slot=slot, start=True)

        gather(tile=r, slot=slot, start=False)

        @pl.when(r + 1 < nr)
        def _():
            gather(tile=r + 1, slot=1 - slot, start=True)

        acc_sc[...] = jnp.zeros_like(acc_sc)

    @pl.when(tv_ref[r] > 0)
    def _():
        x = xbuf[slot, :, 0:D_MODEL]
        a = _dot(x, w1_ref[0])
        b = _dot(x, w3_ref[0])
        hid = (a * jax.nn.sigmoid(a) * b).astype(BF16)
        acc_sc[...] += _dot(hid, w2_ref[0])

    @pl.when(j == nj - 1)
    def _():
        @pl.when(r >= 2)
        def _():
            scatter(tile=r - 2, slot=slot, start=False)

        gl = xbuf[slot, :, D_MODEL:SORT_COLS].astype(F32)
        gate = gl[:, 0:1] + gl[:, 1:2]
        ybuf[slot] = (acc_sc[...] * gate).astype(BF16)
        scatter(tile=r, slot=slot, start=True)

        @pl.when(r == nr - 1)
        def _():
            @pl.when(r >= 1)
            def _():
                scatter(tile=r - 1, slot=1 - slot, start=False)

            scatter(tile=r, slot=slot, start=False)


def _moe_group(xs, tile_expert, tile_valid, src, w1, w3, w2, n_row_tiles, tf):
    ne, _, dff = w1.shape
    nj = dff // tf

    def w_in(r, j, te, tv, s):
        return (te[r], 0, jnp.where(tv[r] > 0, j, nj - 1))

    def w_out(r, j, te, tv, s):
        return (te[r], jnp.where(tv[r] > 0, j, nj - 1), 0)

    grid_spec = pltpu.PrefetchScalarGridSpec(
        num_scalar_prefetch=3,
        grid=(n_row_tiles, nj),
        in_specs=[pl.BlockSpec(memory_space=pl.ANY),
                  pl.BlockSpec((1, D_MODEL, tf), w_in),
                  pl.BlockSpec((1, D_MODEL, tf), w_in),
                  pl.BlockSpec((1, tf, D_MODEL), w_out)],
        out_specs=pl.BlockSpec(memory_space=pl.ANY),
        scratch_shapes=[pltpu.VMEM((2, MOE_ROWS, SORT_COLS), BF16),
                        pltpu.VMEM((2, MOE_ROWS, D_MODEL), BF16),
                        pltpu.VMEM((MOE_ROWS, D_MODEL), F32),
                        pltpu.SemaphoreType.DMA((2,)),
                        pltpu.SemaphoreType.DMA((2,))],
    )
    return pl.pallas_call(
        _moe_group_kernel,
        out_shape=jax.ShapeDtypeStruct(xs.shape, xs.dtype),
        grid_spec=grid_spec,
        input_output_aliases={3: 0},
        compiler_params=_params("arbitrary", "arbitrary"),
        name="moe_group",
    )(tile_expert, tile_valid, src, xs, w1, w3, w2)


def _combine_kernel(h_ref, pos_ref, ys_ref, out_ref):
    pos = pos_ref[...]
    pt = _one_hot_rows(pos[:, 0:1], pos[:, 1:2], SORT_CAP)
    out_ref[...] = h_ref[...] + _dot(pt, ys_ref[...])


def _combine(h, pos, ys):
    tokens = h.shape[0]
    return pl.pallas_call(
        _combine_kernel,
        out_shape=jax.ShapeDtypeStruct((tokens, D_MODEL), F32),
        grid=(tokens // SORT_TOKENS,),
        in_specs=[pl.BlockSpec((SORT_TOKENS, D_MODEL), lambda i: (i, 0)),
                  pl.BlockSpec((SORT_TOKENS, LANES), lambda i: (i, 0)),
                  pl.BlockSpec((SORT_CAP, D_MODEL), lambda i: (i, 0))],
        out_specs=pl.BlockSpec((SORT_TOKENS, D_MODEL), lambda i: (i, 0)),
        compiler_params=_params("parallel"),
        name="combine",
    )(h, pos, ys)


def _row_tile(rows, want):
    tm = want
    while rows % tm:
        tm //= 2
    return tm


def _rope_tables(lp):
    inv_freq = 1.0 / (ROPE_THETA ** (jnp.arange(0, SWA_HEAD_DIM, 2, dtype=F32) / SWA_HEAD_DIM))
    pos = jnp.arange(lp, dtype=F32) - FRONT
    ang = pos[:, None] * inv_freq[None, :]
    cos, sin = jnp.cos(ang), jnp.sin(ang)
    zero = jnp.zeros_like(sin)
    rep = LANES // SWA_HEAD_DIM
    cos_t = jnp.tile(jnp.concatenate([cos, cos], axis=-1), (1, rep))
    sin_a = jnp.tile(jnp.concatenate([zero, sin], axis=-1), (1, rep))
    sin_b = jnp.tile(jnp.concatenate([-sin, zero], axis=-1), (1, rep))
    return cos_t, sin_a, sin_b


def kernel(x, meta, a_norm, a_w_in, a_w_gate2, a_b_gate, a_q_norm, a_k_norm, a_sinks, a_o_norm,
           a_w_out, f_norm, f_w1, f_w3, f_w2, c_norm, c_w_pw1, c_w_dw, c_b_dw, c_ln_g, c_ln_b,
           c_w_pw2, m_norm, m_w_router, m_w1, m_w3, m_w2):
    batch, seq, _ = x.shape
    assert seq % BLOCK == 0
    lp = BLOCK + seq
    nb = lp // BLOCK
    rows = batch * lp

    head = jnp.concatenate([jnp.zeros((FRONT, D_MODEL), x.dtype), meta.astype(x.dtype)], axis=0)
    h = jnp.concatenate([jnp.broadcast_to(head[None], (batch, BLOCK, D_MODEL)), x], axis=1)
    h = h.reshape(rows, D_MODEL)

    w_in = a_w_in[0]
    gq, gk, gv, gr, glr, sq, sk, sv = jnp.split(
        w_in, [256, 512, 1024, 1536, 1552, 2064, 2192], axis=-1)
    w_in_r = jnp.concatenate(
        [gq, gk, gv, gr, sq, sk, sv, glr,
         jnp.zeros((D_MODEL, Z_COLS - Z_LR - GLA_LOWRANK), w_in.dtype)], axis=-1).astype(BF16)
    z = _in_proj(h, a_norm[0][None], w_in_r, _row_tile(rows, 512))

    wg = jnp.zeros((LANES, GLA_QK), F32).at[:GLA_LOWRANK].set(a_w_gate2[0]).astype(BF16)
    o_a = _gla(z, wg, a_b_gate[0][None], a_o_norm[0][None], batch, nb)

    cos_t, sin_a, sin_b = _rope_tables(lp)
    qn = jnp.tile(a_q_norm[0], SWA_Q_HEADS)[None]
    kn = jnp.tile(a_k_norm[0], SWA_KV_HEADS)[None]
    lane = jnp.arange(LANES)
    bd = (lane[:, None] // SWA_HEAD_DIM == lane[None, :] // SWA_HEAD_DIM).astype(BF16)
    o_b = _swa(z, a_sinks[0], cos_t, sin_a, sin_b, qn, kn, bd, batch, nb)

    w_out = a_w_out[0].astype(BF16)
    tm = _row_tile(rows, 1024)
    h = _mix_out_ffn(h, o_a, o_b, w_out[:GLA_V], w_out[GLA_V:], f_norm[0][None],
                     f_w1[0].astype(BF16), f_w3[0].astype(BF16), f_w2[0].astype(BF16), tm, 512)

    u = _conv_glu(h, c_norm[0][None], c_w_pw1[0].astype(BF16), _row_tile(rows, 512))
    h = _conv_module(u, h, c_w_dw[0], c_b_dw[0][None], c_ln_g[0][None], c_ln_b[0][None],
                     c_w_pw2[0].astype(BF16), batch, nb)

    wr = jnp.zeros((D_MODEL, LANES), F32).at[:, :N_EXPERTS].set(m_w_router[0])
    wr_hi = wr.astype(BF16)
    wr_lo = (wr - wr_hi.astype(F32)).astype(BF16)
    tokens = batch * seq
    assert tokens % SORT_TOKENS == 0
    nt = tokens // SORT_TOKENS
    xs, pos, ng = _route_sort(h, m_norm[0][None], wr_hi, wr_lo)
    ng = ng[:, 0, :N_EXPERTS]
    n_row_tiles = (nt * (SORT_CAP // GROUP)) // GROUPS_PER_TILE + N_EXPERTS
    tile_expert, tile_valid, src = _group_tables(ng, n_row_tiles)
    ys = _moe_group(xs, tile_expert, tile_valid, src, m_w1[0].astype(BF16),
                    m_w3[0].astype(BF16), m_w2[0].astype(BF16), n_row_tiles, 512)
    out = _combine(h, pos, ys)
    return out.reshape(batch, seq, D_MODEL)
```

```python
import functools
import math

import jax
import jax.numpy as jnp
from jax import lax
from jax.experimental import pallas as pl
from jax.experimental.pallas import tpu as pltpu

F32 = jnp.float32
BF16 = jnp.bfloat16

D_MODEL = 1024
N_META = 16
EPS = 1e-6
ROPE_THETA = 10000.0
NEG_INF = -1e30
GLA_HEADS = 4
GLA_DK = 64
GLA_DV = 128
GLA_LOWRANK = 16
GLA_TAU = 16.0
GLA_CHUNK = 64
SWA_Q_HEADS = 8
SWA_KV_HEADS = 2
SWA_HEAD_DIM = 64
SWA_WINDOW = 128
CONV_WIDTH = 31
D_FF = 3584
N_EXPERTS = 8

GLA_QK = GLA_HEADS * GLA_DK
GLA_V = GLA_HEADS * GLA_DV
SWA_Q = SWA_Q_HEADS * SWA_HEAD_DIM
SWA_KV = SWA_KV_HEADS * SWA_HEAD_DIM
SWA_GROUP = SWA_Q_HEADS // SWA_KV_HEADS

LANES = 128
BLOCK = 128
FRONT = BLOCK - N_META
VMEM_LIMIT = 56 * 1024 * 1024

Z_Q, Z_K, Z_V, Z_R = 0, 256, 512, 1024
Z_SQ, Z_SK, Z_SV, Z_LR = 1536, 2048, 2176, 2304
Z_COLS = 2432


def _rms(x, g):
    return x * lax.rsqrt(jnp.mean(x * x, axis=-1, keepdims=True) + EPS) * g


def _dot(a, b):
    return jnp.dot(a, b, preferred_element_type=F32)


def _dot_nt(a, b):
    return lax.dot_general(a, b, (((1,), (1,)), ((), ())), preferred_element_type=F32)


def _dot_tn(a, b):
    return lax.dot_general(a, b, (((0,), (0,)), ((), ())), preferred_element_type=F32)


def _params(*sem):
    return pltpu.CompilerParams(dimension_semantics=sem, vmem_limit_bytes=VMEM_LIMIT)


def _in_proj_kernel(h_ref, g_ref, w_ref, z_ref):
    xn = _rms(h_ref[...], g_ref[...]).astype(BF16)
    z_ref[...] = _dot(xn, w_ref[...]).astype(z_ref.dtype)


def _in_proj(h, g, w, tm):
    rows = h.shape[0]
    n = w.shape[1]
    return pl.pallas_call(
        _in_proj_kernel,
        out_shape=jax.ShapeDtypeStruct((rows, n), BF16),
        grid=(rows // tm,),
        in_specs=[pl.BlockSpec((tm, D_MODEL), lambda i: (i, 0)),
                  pl.BlockSpec((1, D_MODEL), lambda i: (0, 0)),
                  pl.BlockSpec((D_MODEL, n), lambda i: (0, 0))],
        out_specs=pl.BlockSpec((tm, n), lambda i: (i, 0)),
        compiler_params=_params("parallel"),
        name="in_proj",
    )(h, g, w)


MIX_BATCH = 4


def _gla_kernel(q_ref, k_ref, v_ref, r_ref, lr_ref, wg_ref, bg_ref, on_ref, o_ref, st_ref):
    n = pl.program_id(1)

    @pl.when(n == 0)
    def _():
        st_ref[...] = jnp.zeros_like(st_ref)

    c = GLA_CHUNK
    items = [(bi, ch) for bi in range(q_ref.shape[0]) for ch in range(BLOCK // c)]
    ri = lax.broadcasted_iota(jnp.int32, (BLOCK, BLOCK), 0)
    ci = lax.broadcasted_iota(jnp.int32, (BLOCK, BLOCK), 1)
    tri = jnp.where((ri >= ci) & (ri // c == ci // c), 1.0, 0.0).astype(BF16)
    causal = (lax.broadcasted_iota(jnp.int32, (c, c), 0)
              >= lax.broadcasted_iota(jnp.int32, (c, c), 1))
    row = lax.broadcasted_iota(jnp.int32, (BLOCK, GLA_QK), 0) + n * BLOCK
    on = on_ref[...]

    b_all = []
    for bi in range(q_ref.shape[0]):
        gp = _dot(lr_ref[bi], wg_ref[...]) + bg_ref[...]
        g = (jnp.minimum(gp, 0.0) - jnp.log(1.0 + jnp.exp(-jnp.abs(gp)))) * (1.0 / GLA_TAU)
        g = jnp.where(row >= FRONT, g, 0.0)
        g_hi = g.astype(BF16)
        g_lo = (g - g_hi.astype(F32)).astype(BF16)
        b_all.append(_dot(tri, g_hi) + _dot(tri, g_lo))

    q_t, k_t, k_d, dec = {}, {}, {}, {}
    for bi, ch in items:
        rs = slice(ch * c, (ch + 1) * c)
        b = b_all[bi][rs]
        b_last = b_all[bi][(ch + 1) * c - 1:(ch + 1) * c, :]
        qf = q_ref[bi, rs, :].astype(F32)
        kf = k_ref[bi, rs, :].astype(F32)
        q_t[bi, ch] = (qf * jnp.exp(b) * (GLA_DK ** -0.5)).astype(BF16)
        k_t[bi, ch] = (kf * jnp.exp(-b)).astype(BF16)
        k_d[bi, ch] = (kf * jnp.exp(b_last - b)).astype(BF16)
        dec[bi, ch] = jnp.exp(b_last)

    att, kv = {}, {}
    for bi, ch in items:
        rs = slice(ch * c, (ch + 1) * c)
        for hd in range(GLA_HEADS):
            ks = slice(hd * GLA_DK, (hd + 1) * GLA_DK)
            vh = v_ref[bi, rs, hd * GLA_DV:(hd + 1) * GLA_DV]
            att[bi, ch, hd] = jnp.where(
                causal, _dot_nt(q_t[bi, ch][:, ks], k_t[bi, ch][:, ks]), 0.0).astype(BF16)
            kv[bi, ch, hd] = _dot_tn(vh, k_d[bi, ch][:, ks])

    for bi in range(q_ref.shape[0]):
        for hd in range(GLA_HEADS):
            ks = slice(hd * GLA_DK, (hd + 1) * GLA_DK)
            vs = slice(hd * GLA_DV, (hd + 1) * GLA_DV)
            st = st_ref[bi, hd]
            for ch in range(BLOCK // c):
                rs = slice(ch * c, (ch + 1) * c)
                o = (_dot(att[bi, ch, hd], v_ref[bi, rs, vs])
                     + _dot_nt(q_t[bi, ch][:, ks], st.astype(BF16)))
                st = st * dec[bi, ch][:, ks] + kv[bi, ch, hd]
                o = _rms(o, on)
                r = r_ref[bi, rs, vs].astype(F32)
                o_ref[bi, rs, vs] = (o * (r * jax.nn.sigmoid(r))).astype(o_ref.dtype)
            st_ref[bi, hd] = st


def _gla(z, wg, bg, on, bb):
    batch, lp, _ = z.shape

    def col(width, start):
        return pl.BlockSpec((bb, BLOCK, width), lambda b, n: (b, n, start // width))

    const = lambda shape: pl.BlockSpec(shape, lambda b, n: (0,) * len(shape))
    return pl.pallas_call(
        _gla_kernel,
        out_shape=jax.ShapeDtypeStruct((batch, lp, GLA_V), BF16),
        grid=(batch // bb, lp // BLOCK),
        in_specs=[col(GLA_QK, Z_Q), col(GLA_QK, Z_K), col(GLA_V, Z_V), col(GLA_V, Z_R),
                  col(LANES, Z_LR), const((LANES, GLA_QK)), const((1, GLA_QK)),
                  const((1, GLA_DV))],
        out_specs=pl.BlockSpec((bb, BLOCK, GLA_V), lambda b, n: (b, n, 0)),
        scratch_shapes=[pltpu.VMEM((bb, GLA_HEADS, GLA_DV, GLA_DK), F32)],
        compiler_params=_params("parallel", "arbitrary"),
        name="gla",
    )(z, z, z, z, z, wg, bg, on)


def _group_mean_sq(x, bd):
    sq = x * x
    hi = sq.astype(BF16)
    lo = (sq - hi.astype(F32)).astype(BF16)
    return _dot(hi, bd) + _dot(lo, bd)


def _norm_rope(x, gain, cos, sin_a, sin_b, bd):
    nbat, rows, width = x.shape
    x2 = x.reshape(nbat * rows, width)
    parts = []
    for s in range(width // LANES):
        xs = x2[:, s * LANES:(s + 1) * LANES]
        ms = _group_mean_sq(xs, bd) * (1.0 / SWA_HEAD_DIM)
        parts.append(xs * lax.rsqrt(ms + EPS))
    xn = (parts[0] if len(parts) == 1 else jnp.concatenate(parts, axis=-1)) * gain
    half = SWA_HEAD_DIM // 2
    back = lambda t: t.reshape(nbat, rows, width)
    return (back(xn) * cos + back(pltpu.roll(xn, half, 1)) * sin_a
            + back(pltpu.roll(xn, width - half, 1)) * sin_b)


def _swa_kernel(sinks_ref, q_ref, kc_ref, kp_ref, vc_ref, vp_ref, cosc_ref, sac_ref, sbc_ref,
                cosp_ref, sap_ref, sbp_ref, qn_ref, kn_ref, bd_ref, o_ref):
    n = pl.program_id(1)
    nbat = q_ref.shape[0]
    bd = bd_ref[...]
    rep = SWA_Q // LANES
    tile = lambda t: jnp.concatenate([t] * rep, axis=-1)
    cos_c, sa_c, sb_c = cosc_ref[...], sac_ref[...], sbc_ref[...]
    q = _norm_rope(q_ref[...].astype(F32), qn_ref[...], tile(cos_c), tile(sa_c), tile(sb_c), bd)
    q = (q * (SWA_HEAD_DIM ** -0.5)).astype(BF16)
    k_c = _norm_rope(kc_ref[...].astype(F32), kn_ref[...], cos_c, sa_c, sb_c, bd).astype(BF16)
    k_p = _norm_rope(kp_ref[...].astype(F32), kn_ref[...], cosp_ref[...], sap_ref[...],
                     sbp_ref[...], bd).astype(BF16)

    gq = SWA_GROUP * BLOCK
    r = lax.broadcasted_iota(jnp.int32, (gq, 2 * BLOCK), 0) & (BLOCK - 1)
    cidx = lax.broadcasted_iota(jnp.int32, (gq, 2 * BLOCK), 1)
    allowed = (cidx > r) & (cidx <= r + SWA_WINDOW) & ((n - 1) * BLOCK + cidx >= FRONT)
    rgroup = lax.broadcasted_iota(jnp.int32, (gq, 1), 0) // BLOCK
    items = [(bi, hk) for bi in range(nbat) for hk in range(SWA_KV_HEADS)]

    s, sink = {}, {}
    for hk in range(SWA_KV_HEADS):
        sk = jnp.zeros((gq, 1), F32)
        for g in range(SWA_GROUP):
            sk = jnp.where(rgroup == g, sinks_ref[hk * SWA_GROUP + g], sk)
        sink[hk] = sk
    for bi, hk in items:
        ks = slice(hk * SWA_HEAD_DIM, (hk + 1) * SWA_HEAD_DIM)
        qg = jnp.concatenate(
            [q[bi][:, (hk * SWA_GROUP + g) * SWA_HEAD_DIM:(hk * SWA_GROUP + g + 1) * SWA_HEAD_DIM]
             for g in range(SWA_GROUP)], axis=0)
        k = jnp.concatenate([k_p[bi][:, ks], k_c[bi][:, ks]], axis=0)
        s[bi, hk] = jnp.where(allowed, _dot_nt(qg, k), NEG_INF)

    p, denom = {}, {}
    for bi, hk in items:
        m = jnp.maximum(jnp.max(s[bi, hk], axis=-1, keepdims=True), sink[hk])
        e = jnp.exp(s[bi, hk] - m)
        denom[bi, hk] = jnp.sum(e, axis=-1, keepdims=True) + jnp.exp(sink[hk] - m)
        p[bi, hk] = e.astype(BF16)

    for bi, hk in items:
        ks = slice(hk * SWA_HEAD_DIM, (hk + 1) * SWA_HEAD_DIM)
        v = jnp.concatenate([vp_ref[bi, :, ks], vc_ref[bi, :, ks]], axis=0)
        o = _dot(p[bi, hk], v) / denom[bi, hk]
        for g in range(SWA_GROUP):
            hq = hk * SWA_GROUP + g
            o_ref[bi, :, hq * SWA_HEAD_DIM:(hq + 1) * SWA_HEAD_DIM] = (
                o[g * BLOCK:(g + 1) * BLOCK].astype(o_ref.dtype))


def _swa(z, sinks, cos, sin_a, sin_b, qn, kn, bd, bb):
    batch, lp, _ = z.shape

    def cur(width, start):
        return pl.BlockSpec((bb, BLOCK, width), lambda b, n, s: (b, n, start // width))

    def prev(width, start):
        return pl.BlockSpec((bb, BLOCK, width),
                            lambda b, n, s: (b, jnp.maximum(n - 1, 0), start // width))

    tab_c = pl.BlockSpec((BLOCK, LANES), lambda b, n, s: (n, 0))
    tab_p = pl.BlockSpec((BLOCK, LANES), lambda b, n, s: (jnp.maximum(n - 1, 0), 0))
    const = lambda shape: pl.BlockSpec(shape, lambda b, n, s: (0,) * len(shape))
    grid_spec = pltpu.PrefetchScalarGridSpec(
        num_scalar_prefetch=1,
        grid=(batch // bb, lp // BLOCK),
        in_specs=[cur(SWA_Q, Z_SQ), cur(SWA_KV, Z_SK), prev(SWA_KV, Z_SK),
                  cur(SWA_KV, Z_SV), prev(SWA_KV, Z_SV),
                  tab_c, tab_c, tab_c, tab_p, tab_p, tab_p,
                  const((1, SWA_Q)), const((1, SWA_KV)), const((LANES, LANES))],
        out_specs=pl.BlockSpec((bb, BLOCK, SWA_Q), lambda b, n, s: (b, n, 0)),
    )
    return pl.pallas_call(
        _swa_kernel,
        out_shape=jax.ShapeDtypeStruct((batch, lp, SWA_Q), BF16),
        grid_spec=grid_spec,
        compiler_params=_params("parallel", "arbitrary"),
        name="swa",
    )(sinks, z, z, z, z, z, cos, sin_a, sin_b, cos, sin_a, sin_b, qn, kn, bd)


def _ffn_kernel(h_ref, oa_ref, ob_ref, woa_ref, wob_ref, g_ref, w1_ref, w3_ref, w2_ref,
                out_ref, h1_sc, xn_sc, acc_sc):
    j = pl.program_id(1)

    @pl.when(j == 0)
    def _():
        h1 = h_ref[...] + _dot(oa_ref[...], woa_ref[...]) + _dot(ob_ref[...], wob_ref[...])
        h1_sc[...] = h1
        xn_sc[...] = _rms(h1, g_ref[...]).astype(BF16)
        acc_sc[...] = jnp.zeros_like(acc_sc)

    x = xn_sc[...]
    a = _dot(x, w1_ref[...])
    b = _dot(x, w3_ref[...])
    hid = (a * jax.nn.sigmoid(a) * b).astype(BF16)
    acc_sc[...] += _dot(hid, w2_ref[...])

    @pl.when(j == pl.num_programs(1) - 1)
    def _():
        out_ref[...] = h1_sc[...] + acc_sc[...]


def _mix_out_ffn(h, oa, ob, woa, wob, g, w1, w3, w2, tm, tf):
    rows = h.shape[0]
    dff = w1.shape[1]
    return pl.pallas_call(
        _ffn_kernel,
        out_shape=jax.ShapeDtypeStruct((rows, D_MODEL), F32),
        grid=(rows // tm, dff // tf),
        in_specs=[pl.BlockSpec((tm, D_MODEL), lambda i, j: (i, 0)),
                  pl.BlockSpec((tm, GLA_V), lambda i, j: (i, 0)),
                  pl.BlockSpec((tm, SWA_Q), lambda i, j: (i, 0)),
                  pl.BlockSpec((GLA_V, D_MODEL), lambda i, j: (0, 0)),
                  pl.BlockSpec((SWA_Q, D_MODEL), lambda i, j: (0, 0)),
                  pl.BlockSpec((1, D_MODEL), lambda i, j: (0, 0)),
                  pl.BlockSpec((D_MODEL, tf), lambda i, j: (0, j)),
                  pl.BlockSpec((D_MODEL, tf), lambda i, j: (0, j)),
                  pl.BlockSpec((tf, D_MODEL), lambda i, j: (j, 0))],
        out_specs=pl.BlockSpec((tm, D_MODEL), lambda i, j: (i, 0)),
        scratch_shapes=[pltpu.VMEM((tm, D_MODEL), F32), pltpu.VMEM((tm, D_MODEL), BF16),
                        pltpu.VMEM((tm, D_MODEL), F32)],
        compiler_params=_params("parallel", "arbitrary"),
        name="mix_out_ffn",
    )(h, oa, ob, woa, wob, g, w1, w3, w2)


CONV_HALO = 32
CONV_SPAN = CONV_HALO + BLOCK
CONV_ROWS = 32
SUBLANES = 8


def _conv_kernel(h_ref, g_ref, wa_ref, wg_ref, shift_ref, wdw_ref, bdw_ref, lg_ref, lb_ref,
                 w2_ref, out_ref, u_sc, xs_sc, c_sc, y_sc):
    h = h_ref[...]
    xn = _rms(h, g_ref[...]).astype(BF16)
    u = _dot(xn, wa_ref[...]) * jax.nn.sigmoid(_dot(xn, wg_ref[...]))
    u_sc[CONV_HALO:CONV_SPAN, :] = u.astype(BF16)

    @pl.when(pl.program_id(1) > 0)
    def _():
        span = u_sc[...]
        xs_sc[0] = span.astype(F32)
        shifted = _dot(shift_ref[...], span)
        for r in range(1, SUBLANES):
            xs_sc[r] = shifted[(r - 1) * CONV_SPAN:r * CONV_SPAN]
        first = CONV_HALO - (CONV_WIDTH - 1)
        for cb in range(D_MODEL // LANES):
            cs = slice(cb * LANES, (cb + 1) * LANES)
            acc = jnp.zeros((BLOCK // SUBLANES, SUBLANES, LANES), F32)
            for j in range(CONV_WIDTH):
                r = (first + j) % SUBLANES
                base = first + j - r
                x = xs_sc[r, base:base + BLOCK, cs].reshape(BLOCK // SUBLANES, SUBLANES, LANES)
                acc = acc + x * wdw_ref[j, :, cs]
            c_sc[:, cs] = acc.reshape(BLOCK, LANES)
        for rb in range(BLOCK // CONV_ROWS):
            acc = c_sc[rb * CONV_ROWS:(rb + 1) * CONV_ROWS, :] + bdw_ref[...]
            mu = jnp.mean(acc, axis=-1, keepdims=True)
            dev = acc - mu
            var = jnp.mean(dev * dev, axis=-1, keepdims=True)
            y = dev * lax.rsqrt(var + EPS) * lg_ref[...] + lb_ref[...]
            y_sc[rb * CONV_ROWS:(rb + 1) * CONV_ROWS, :] = (y * jax.nn.sigmoid(y)).astype(BF16)
        out_ref[...] = h + _dot(y_sc[...], w2_ref[...])

    u_sc[0:CONV_HALO, :] = u_sc[BLOCK:CONV_SPAN, :]


def _conv_module(h, g, w_pw1, shift, wdw, bdw, lg, lb, w2, batch, nb):
    const = lambda shape: pl.BlockSpec(shape, lambda b, n: (0,) * len(shape))
    return pl.pallas_call(
        _conv_kernel,
        out_shape=jax.ShapeDtypeStruct((batch * (nb - 1) * BLOCK, D_MODEL), F32),
        grid=(batch, nb),
        in_specs=[pl.BlockSpec((BLOCK, D_MODEL), lambda b, n: (b * nb + n, 0)),
                  const((1, D_MODEL)),
                  pl.BlockSpec((D_MODEL, D_MODEL), lambda b, n: (0, 0)),
                  pl.BlockSpec((D_MODEL, D_MODEL), lambda b, n: (0, 1)),
                  const(((SUBLANES - 1) * CONV_SPAN, CONV_SPAN)),
                  const((CONV_WIDTH, SUBLANES, D_MODEL)), const((1, D_MODEL)), const((1, D_MODEL)),
                  const((1, D_MODEL)), const((D_MODEL, D_MODEL))],
        out_specs=pl.BlockSpec((BLOCK, D_MODEL),
                               lambda b, n: (b * (nb - 1) + jnp.maximum(n - 1, 0), 0)),
        scratch_shapes=[pltpu.VMEM((CONV_SPAN, D_MODEL), BF16),
                        pltpu.VMEM((SUBLANES, CONV_SPAN, D_MODEL), F32),
                        pltpu.VMEM((BLOCK, D_MODEL), F32),
                        pltpu.VMEM((BLOCK, D_MODEL), BF16)],
        compiler_params=_params("parallel", "arbitrary"),
        name="conv_module",
    )(h, g, w_pw1, w_pw1, shift, wdw, bdw, lg, lb, w2)


SORT_TOKENS = 512
GROUP = 16
SORT_CAP = 2 * SORT_TOKENS + N_EXPERTS * GROUP
SORT_COLS = D_MODEL + LANES
MOE_ROWS = 512
GROUPS_PER_TILE = MOE_ROWS // GROUP


def _one_hot_rows(pos1, pos2, width):
    r = lax.broadcasted_iota(jnp.int32, (pos1.shape[0], width), 1).astype(F32)
    return jnp.where(r == pos1, 1.0, jnp.where(r == pos2, 1.0, 0.0)).astype(BF16)


def _route_sort_kernel(h_ref, g_ref, wr_hi_ref, wr_lo_ref, xs_ref, pos_ref, ng_ref):
    s = SORT_TOKENS
    xn = _rms(h_ref[...], g_ref[...])
    x_hi = xn.astype(BF16)
    x_lo = (xn - x_hi.astype(F32)).astype(BF16)
    logits = (_dot(x_hi, wr_hi_ref[...]) + _dot(x_lo, wr_hi_ref[...])
              + _dot(x_hi, wr_lo_ref[...]))
    lane = lax.broadcasted_iota(jnp.int32, logits.shape, 1)
    logits = jnp.where(lane < N_EXPERTS, logits, -jnp.inf)
    m1 = jnp.max(logits, axis=-1, keepdims=True)
    i1 = jnp.min(jnp.where(logits == m1, lane, LANES), axis=-1, keepdims=True)
    rest = jnp.where(lane == i1, -jnp.inf, logits)
    m2 = jnp.max(rest, axis=-1, keepdims=True)
    i2 = jnp.min(jnp.where(rest == m2, lane, LANES), axis=-1, keepdims=True)
    e2 = jnp.exp(m2 - m1)
    g1 = 1.0 / (1.0 + e2)
    g2 = e2 / (1.0 + e2)

    oh1 = lane == i1
    oh2 = lane == i2
    oh = jnp.where(oh1, 1.0, jnp.where(oh2, 1.0, 0.0))
    cnt = jnp.sum(oh, axis=0, keepdims=True)
    ngr = jnp.floor((cnt + (GROUP - 1)) * (1.0 / GROUP))
    ri = lax.broadcasted_iota(jnp.int32, (s, s), 0)
    ci = lax.broadcasted_iota(jnp.int32, (s, s), 1)
    rank = _dot(jnp.where(ri > ci, 1.0, 0.0).astype(BF16), oh.astype(BF16))
    li = lax.broadcasted_iota(jnp.int32, (LANES, LANES), 0)
    lj = lax.broadcasted_iota(jnp.int32, (LANES, LANES), 1)
    before = _dot(jnp.broadcast_to(ngr, (8, LANES)).astype(BF16),
                  jnp.where(li < lj, 1.0, 0.0).astype(BF16))[0:1]
    base = before * GROUP + rank
    pos1 = jnp.sum(jnp.where(oh1, base, 0.0), axis=-1, keepdims=True)
    pos2 = jnp.sum(jnp.where(oh2, base, 0.0), axis=-1, keepdims=True)

    r = lax.broadcasted_iota(jnp.int32, (s, SORT_CAP), 1).astype(F32)
    p1 = jnp.where(r == pos1, 1.0, 0.0).astype(BF16)
    p2 = jnp.where(r == pos2, 1.0, 0.0).astype(BF16)
    xs_ref[:, 0:D_MODEL] = _dot_tn(p1 + p2, x_hi).astype(BF16)

    def gate_lanes(gv):
        hi = gv.astype(BF16).astype(F32)
        return jnp.where(lane == 0, hi, jnp.where(lane == 1, gv - hi, 0.0)).astype(BF16)

    xs_ref[:, D_MODEL:SORT_COLS] = (_dot_tn(p1, gate_lanes(g1))
                                    + _dot_tn(p2, gate_lanes(g2))).astype(BF16)
    pos_ref[...] = jnp.where(lane == 0, pos1, jnp.where(lane == 1, pos2, 0.0))
    ng_ref[0] = jnp.broadcast_to(ngr, (8, LANES)).astype(jnp.int32)


def _route_sort(h, g, wr_hi, wr_lo):
    tokens = h.shape[0]
    nt = tokens // SORT_TOKENS
    return pl.pallas_call(
        _route_sort_kernel,
        out_shape=(jax.ShapeDtypeStruct((nt * SORT_CAP, SORT_COLS), BF16),
                   jax.ShapeDtypeStruct((tokens, LANES), F32),
                   jax.ShapeDtypeStruct((nt, 8, LANES), jnp.int32)),
        grid=(nt,),
        in_specs=[pl.BlockSpec((SORT_TOKENS, D_MODEL), lambda i: (i, 0)),
                  pl.BlockSpec((1, D_MODEL), lambda i: (0, 0)),
                  pl.BlockSpec((D_MODEL, LANES), lambda i: (0, 0)),
                  pl.BlockSpec((D_MODEL, LANES), lambda i: (0, 0))],
        out_specs=(pl.BlockSpec((SORT_CAP, SORT_COLS), lambda i: (i, 0)),
                   pl.BlockSpec((SORT_TOKENS, LANES), lambda i: (i, 0)),
                   pl.BlockSpec((1, 8, LANES), lambda i: (i, 0, 0))),
        compiler_params=_params("parallel"),
        name="route_sort",
    )(h, g, wr_hi, wr_lo)


def _group_tables(ng, n_row_tiles):
    nt, ne = ng.shape
    gpt = GROUPS_PER_TILE
    n_e = jnp.sum(ng, axis=0)
    rt_e = (n_e + gpt - 1) // gpt
    rt_end = jnp.cumsum(rt_e)
    n_rt = rt_end[-1]
    r = jnp.arange(n_row_tiles, dtype=jnp.int32)
    valid_r = r < n_rt
    e_r = jnp.minimum(jnp.sum((r[:, None] >= rt_end[None, :]).astype(jnp.int32), axis=1), ne - 1)
    e_last = jnp.sum(jnp.where(r == n_rt - 1, e_r, 0))
    e_r = jnp.where(valid_r, e_r, e_last)
    sel = (e_r[:, None] == jnp.arange(ne, dtype=jnp.int32)[None, :]).astype(jnp.int32)
    pick = lambda v: jnp.sum(sel * v[None, :], axis=1)
    k0 = (r - pick(rt_end - rt_e)) * gpt
    gidx = k0[:, None] + jnp.arange(gpt, dtype=jnp.int32)[None, :]
    valid = valid_r[:, None] & (gidx < pick(n_e)[:, None])
    incl = jnp.cumsum(ng, axis=0)
    incl_r = jnp.sum(sel[:, None, :] * incl[None, :, :], axis=2)
    ng_r = jnp.sum(sel[:, None, :] * ng[None, :, :], axis=2)
    lo = jnp.cumsum(ng, axis=1) - ng
    lo_r = jnp.sum(sel[:, None, :] * lo[None, :, :], axis=2)
    ti = jnp.sum((incl_r[:, None, :] <= gidx[:, :, None]).astype(jnp.int32), axis=2)
    ti = jnp.minimum(ti, nt - 1)
    tsel = (ti[:, :, None] == jnp.arange(nt, dtype=jnp.int32)[None, None, :]).astype(jnp.int32)
    tpick = lambda v: jnp.sum(tsel * v[:, None, :], axis=2)
    within = gidx - tpick(incl_r - ng_r)
    src = ti * SORT_CAP + GROUP * (tpick(lo_r) + within)
    src = jnp.where(valid, src, -1).reshape(-1)
    return e_r, valid_r.astype(jnp.int32), src


def _group_copies(src_ref, tile, hbm, buf, slot, sem, to_hbm, start):
    for k in range(GROUPS_PER_TILE):
        off = src_ref[tile * GROUPS_PER_TILE + k]

        @pl.when(off >= 0)
        def _():
            buf_rows = buf.at[slot, pl.ds(k * GROUP, GROUP), :]
            hbm_rows = hbm.at[pl.ds(pl.multiple_of(off, GROUP), GROUP),
                              pl.ds(0, buf_rows.shape[-1])]
            cp = (pltpu.make_async_copy(buf_rows, hbm_rows, sem.at[slot]) if to_hbm
                  else pltpu.make_async_copy(hbm_rows, buf_rows, sem.at[slot]))
            if start:
                cp.start()
            else:
                cp.wait()


def _moe_group_kernel(te_ref, tv_ref, src_ref, xs_hbm, w1_ref, w3_ref, w2_ref, ys_hbm,
                      xbuf, ybuf, acc_sc, xsem, ysem):
    r = pl.program_id(0)
    j = pl.program_id(1)
    nr = pl.num_programs(0)
    nj = pl.num_programs(1)
    slot = lax.rem(r, 2)
    gather = functools.partial(_group_copies, src_ref, hbm=xs_hbm, buf=xbuf, sem=xsem,
                               to_hbm=False)
    scatter = functools.partial(_group_copies, src_ref, hbm=ys_hbm, buf=ybuf, sem=ysem,
                                to_hbm=True)

    @pl.when(j == 0)
    def _():
        @pl.when(r == 0)
        def _():
            xbuf[...] = jnp.zeros_like(xbuf)
            gather(tile=r, slot=slot, start=True)

        gather(tile=r, slot=slot, start=False)

        @pl.when(r + 1 < nr)
        def _():
            gather(tile=r + 1, slot=1 - slot, start=True)

        acc_sc[...] = jnp.zeros_like(acc_sc)

    @pl.when(tv_ref[r] > 0)
    def _():
        x = xbuf[slot, :, 0:D_MODEL]
        a = _dot(x, w1_ref[0])
        b = _dot(x, w3_ref[0])
        hid = (a * jax.nn.sigmoid(a) * b).astype(BF16)
        acc_sc[...] += _dot(hid, w2_ref[0])

    @pl.when(j == nj - 1)
    def _():
        @pl.when(r >= 2)
        def _():
            scatter(tile=r - 2, slot=slot, start=False)

        gl = xbuf[slot, :, D_MODEL:SORT_COLS].astype(F32)
        gate = gl[:, 0:1] + gl[:, 1:2]
        ybuf[slot] = (acc_sc[...] * gate).astype(BF16)
        scatter(tile=r, slot=slot, start=True)

        @pl.when(r == nr - 1)
        def _():
            @pl.when(r >= 1)
            def _():
                scatter(tile=r - 1, slot=1 - slot, start=False)

            scatter(tile=r, slot=slot, start=False)


def _moe_group(xs, tile_expert, tile_valid, src, w1, w3, w2, n_row_tiles, tf):
    ne, _, dff = w1.shape
    nj = dff // tf

    def w_in(r, j, te, tv, s):
        return (te[r], 0, jnp.where(tv[r] > 0, j, nj - 1))

    def w_out(r, j, te, tv, s):
        return (te[r], jnp.where(tv[r] > 0, j, nj - 1), 0)

    grid_spec = pltpu.PrefetchScalarGridSpec(
        num_scalar_prefetch=3,
        grid=(n_row_tiles, nj),
        in_specs=[pl.BlockSpec(memory_space=pl.ANY),
                  pl.BlockSpec((1, D_MODEL, tf), w_in),
                  pl.BlockSpec((1, D_MODEL, tf), w_in),
                  pl.BlockSpec((1, tf, D_MODEL), w_out)],
        out_specs=pl.BlockSpec(memory_space=pl.ANY),
        scratch_shapes=[pltpu.VMEM((2, MOE_ROWS, SORT_COLS), BF16),
                        pltpu.VMEM((2, MOE_ROWS, D_MODEL), BF16),
                        pltpu.VMEM((MOE_ROWS, D_MODEL), F32),
                        pltpu.SemaphoreType.DMA((2,)),
                        pltpu.SemaphoreType.DMA((2,))],
    )
    return pl.pallas_call(
        _moe_group_kernel,
        out_shape=jax.ShapeDtypeStruct(xs.shape, xs.dtype),
        grid_spec=grid_spec,
        input_output_aliases={3: 0},
        compiler_params=_params("arbitrary", "arbitrary"),
        name="moe_group",
    )(tile_expert, tile_valid, src, xs, w1, w3, w2)


def _combine_kernel(h_ref, pos_ref, ys_ref, out_ref):
    pos = pos_ref[...]
    pt = _one_hot_rows(pos[:, 0:1], pos[:, 1:2], SORT_CAP)
    out_ref[...] = h_ref[...] + _dot(pt, ys_ref[...])


def _combine(h, pos, ys):
    tokens = h.shape[0]
    return pl.pallas_call(
        _combine_kernel,
        out_shape=jax.ShapeDtypeStruct((tokens, D_MODEL), F32),
        grid=(tokens // SORT_TOKENS,),
        in_specs=[pl.BlockSpec((SORT_TOKENS, D_MODEL), lambda i: (i, 0)),
                  pl.BlockSpec((SORT_TOKENS, LANES), lambda i: (i, 0)),
                  pl.BlockSpec((SORT_CAP, D_MODEL), lambda i: (i, 0))],
        out_specs=pl.BlockSpec((SORT_TOKENS, D_MODEL), lambda i: (i, 0)),
        compiler_params=_params("parallel"),
        name="combine",
    )(h, pos, ys)


def _row_tile(rows, want):
    tm = want
    while rows % tm:
        tm //= 2
    return tm


def _rope_tables(lp):
    inv_freq = 1.0 / (ROPE_THETA ** (jnp.arange(0, SWA_HEAD_DIM, 2, dtype=F32) / SWA_HEAD_DIM))
    pos = jnp.arange(lp, dtype=F32) - FRONT
    ang = pos[:, None] * inv_freq[None, :]
    cos, sin = jnp.cos(ang), jnp.sin(ang)
    zero = jnp.zeros_like(sin)
    rep = LANES // SWA_HEAD_DIM
    cos_t = jnp.tile(jnp.concatenate([cos, cos], axis=-1), (1, rep))
    sin_a = jnp.tile(jnp.concatenate([zero, sin], axis=-1), (1, rep))
    sin_b = jnp.tile(jnp.concatenate([-sin, zero], axis=-1), (1, rep))
    return cos_t, sin_a, sin_b


def kernel(x, meta, a_norm, a_w_in, a_w_gate2, a_b_gate, a_q_norm, a_k_norm, a_sinks, a_o_norm,
           a_w_out, f_norm, f_w1, f_w3, f_w2, c_norm, c_w_pw1, c_w_dw, c_b_dw, c_ln_g, c_ln_b,
           c_w_pw2, m_norm, m_w_router, m_w1, m_w3, m_w2):
    batch, seq, _ = x.shape
    assert seq % BLOCK == 0
    lp = BLOCK + seq
    nb = lp // BLOCK
    rows = batch * lp

    head = jnp.concatenate([jnp.zeros((FRONT, D_MODEL), x.dtype), meta.astype(x.dtype)], axis=0)
    h = jnp.concatenate([jnp.broadcast_to(head[None], (batch, BLOCK, D_MODEL)), x], axis=1)
    h = h.reshape(rows, D_MODEL)

    w_in = a_w_in[0]
    gq, gk, gv, gr, glr, sq, sk, sv = jnp.split(
        w_in, [256, 512, 1024, 1536, 1552, 2064, 2192], axis=-1)
    w_in_r = jnp.concatenate(
        [gq, gk, gv, gr, sq, sk, sv, glr,
         jnp.zeros((D_MODEL, Z_COLS - Z_LR - GLA_LOWRANK), w_in.dtype)], axis=-1).astype(BF16)
    z = _in_proj(h, a_norm[0][None], w_in_r, _row_tile(rows, 512))

    wg = jnp.zeros((LANES, GLA_QK), F32).at[:GLA_LOWRANK].set(a_w_gate2[0]).astype(BF16)
    z = z.reshape(batch, lp, Z_COLS)
    bb = math.gcd(batch, MIX_BATCH)
    o_a = _gla(z, wg, a_b_gate[0][None], a_o_norm[0][None], bb).reshape(rows, GLA_V)

    cos_t, sin_a, sin_b = _rope_tables(lp)
    qn = jnp.tile(a_q_norm[0], SWA_Q_HEADS)[None]
    kn = jnp.tile(a_k_norm[0], SWA_KV_HEADS)[None]
    lane = jnp.arange(LANES)
    bd = (lane[:, None] // SWA_HEAD_DIM == lane[None, :] // SWA_HEAD_DIM).astype(BF16)
    o_b = _swa(z, a_sinks[0], cos_t, sin_a, sin_b, qn, kn, bd, bb).reshape(rows, SWA_Q)

    w_out = a_w_out[0].astype(BF16)
    tm = _row_tile(rows, 1024)
    h = _mix_out_ffn(h, o_a, o_b, w_out[:GLA_V], w_out[GLA_V:], f_norm[0][None],
                     f_w1[0].astype(BF16), f_w3[0].astype(BF16), f_w2[0].astype(BF16), tm, 512)

    si = jnp.arange((SUBLANES - 1) * CONV_SPAN)
    shift = (jnp.arange(CONV_SPAN)[None, :]
             == (si % CONV_SPAN + si // CONV_SPAN + 1)[:, None]).astype(BF16)
    h = _conv_module(h, c_norm[0][None], c_w_pw1[0].astype(BF16), shift,
                     jnp.broadcast_to(c_w_dw[0][:, None, :], (CONV_WIDTH, SUBLANES, D_MODEL)),
                     c_b_dw[0][None], c_ln_g[0][None], c_ln_b[0][None],
                     c_w_pw2[0].astype(BF16), batch, nb)

    wr = jnp.zeros((D_MODEL, LANES), F32).at[:, :N_EXPERTS].set(m_w_router[0])
    wr_hi = wr.astype(BF16)
    wr_lo = (wr - wr_hi.astype(F32)).astype(BF16)
    tokens = batch * seq
    assert tokens % SORT_TOKENS == 0
    nt = tokens // SORT_TOKENS
    xs, pos, ng = _route_sort(h, m_norm[0][None], wr_hi, wr_lo)
    ng = ng[:, 0, :N_EXPERTS]
    n_row_tiles = (nt * (SORT_CAP // GROUP)) // GROUPS_PER_TILE + N_EXPERTS
    tile_expert, tile_valid, src = _group_tables(ng, n_row_tiles)
    ys = _moe_group(xs, tile_expert, tile_valid, src, m_w1[0].astype(BF16),
                    m_w3[0].astype(BF16), m_w2[0].astype(BF16), n_row_tiles, 512)
    out = _combine(h, pos, ys)
    return out.reshape(batch, seq, D_MODEL)
```

```python
import functools
import math

import jax
import jax.numpy as jnp
from jax import lax
from jax.experimental import pallas as pl
from jax.experimental.pallas import tpu as pltpu

F32 = jnp.float32
BF16 = jnp.bfloat16

D_MODEL = 1024
N_META = 16
EPS = 1e-6
ROPE_THETA = 10000.0
NEG_INF = -1e30
GLA_HEADS = 4
GLA_DK = 64
GLA_DV = 128
GLA_LOWRANK = 16
GLA_TAU = 16.0
GLA_CHUNK = 64
SWA_Q_HEADS = 8
SWA_KV_HEADS = 2
SWA_HEAD_DIM = 64
SWA_WINDOW = 128
CONV_WIDTH = 31
D_FF = 3584
N_EXPERTS = 8

GLA_QK = GLA_HEADS * GLA_DK
GLA_V = GLA_HEADS * GLA_DV
SWA_Q = SWA_Q_HEADS * SWA_HEAD_DIM
SWA_KV = SWA_KV_HEADS * SWA_HEAD_DIM
SWA_GROUP = SWA_Q_HEADS // SWA_KV_HEADS

LANES = 128
FF_TILE = 1792
BLOCK = 128
FRONT = BLOCK - N_META
VMEM_LIMIT = 56 * 1024 * 1024

Z_Q, Z_K, Z_V, Z_R = 0, 256, 512, 1024
Z_SQ, Z_SK, Z_SV, Z_LR = 1536, 2048, 2176, 2304
Z_COLS = 2432


def _rms(x, g):
    return x * lax.rsqrt(jnp.mean(x * x, axis=-1, keepdims=True) + EPS) * g


def _dot(a, b):
    return jnp.dot(a, b, preferred_element_type=F32)


def _dot_nt(a, b):
    return lax.dot_general(a, b, (((1,), (1,)), ((), ())), preferred_element_type=F32)


def _dot_tn(a, b):
    return lax.dot_general(a, b, (((0,), (0,)), ((), ())), preferred_element_type=F32)


def _params(*sem):
    return pltpu.CompilerParams(dimension_semantics=sem, vmem_limit_bytes=VMEM_LIMIT)


def _in_proj_kernel(h_ref, g_ref, w_ref, z_ref):
    xn = _rms(h_ref[...], g_ref[...]).astype(BF16)
    z_ref[...] = _dot(xn, w_ref[...]).astype(z_ref.dtype)


def _in_proj(h, g, w, tm):
    rows = h.shape[0]
    n = w.shape[1]
    return pl.pallas_call(
        _in_proj_kernel,
        out_shape=jax.ShapeDtypeStruct((rows, n), BF16),
        grid=(rows // tm,),
        in_specs=[pl.BlockSpec((tm, D_MODEL), lambda i: (i, 0)),
                  pl.BlockSpec((1, D_MODEL), lambda i: (0, 0)),
                  pl.BlockSpec((D_MODEL, n), lambda i: (0, 0))],
        out_specs=pl.BlockSpec((tm, n), lambda i: (i, 0)),
        compiler_params=_params("parallel"),
        name="in_proj",
    )(h, g, w)


MIX_BATCH = 4


def _gla_kernel(q_ref, k_ref, v_ref, r_ref, lr_ref, wg_ref, bg_ref, on_ref, o_ref, st_ref):
    n = pl.program_id(1)

    @pl.when(n == 0)
    def _():
        st_ref[...] = jnp.zeros_like(st_ref)

    c = GLA_CHUNK
    items = [(bi, ch) for bi in range(q_ref.shape[0]) for ch in range(BLOCK // c)]
    ri = lax.broadcasted_iota(jnp.int32, (BLOCK, BLOCK), 0)
    ci = lax.broadcasted_iota(jnp.int32, (BLOCK, BLOCK), 1)
    tri = jnp.where((ri >= ci) & (ri // c == ci // c), 1.0, 0.0).astype(BF16)
    causal = (lax.broadcasted_iota(jnp.int32, (c, c), 0)
              >= lax.broadcasted_iota(jnp.int32, (c, c), 1))
    row = lax.broadcasted_iota(jnp.int32, (BLOCK, GLA_QK), 0) + n * BLOCK
    on = on_ref[...]

    b_all = []
    for bi in range(q_ref.shape[0]):
        gp = _dot(lr_ref[bi], wg_ref[...]) + bg_ref[...]
        g = (jnp.minimum(gp, 0.0) - jnp.log(1.0 + jnp.exp(-jnp.abs(gp)))) * (1.0 / GLA_TAU)
        g = jnp.where(row >= FRONT, g, 0.0)
        g_hi = g.astype(BF16)
        g_lo = (g - g_hi.astype(F32)).astype(BF16)
        b_all.append(_dot(tri, g_hi) + _dot(tri, g_lo))

    q_t, k_t, k_d, dec = {}, {}, {}, {}
    for bi, ch in items:
        rs = slice(ch * c, (ch + 1) * c)
        b = b_all[bi][rs]
        b_last = b_all[bi][(ch + 1) * c - 1:(ch + 1) * c, :]
        qf = q_ref[bi, rs, :].astype(F32)
        kf = k_ref[bi, rs, :].astype(F32)
        q_t[bi, ch] = (qf * jnp.exp(b) * (GLA_DK ** -0.5)).astype(BF16)
        k_t[bi, ch] = (kf * jnp.exp(-b)).astype(BF16)
        k_d[bi, ch] = (kf * jnp.exp(b_last - b)).astype(BF16)
        dec[bi, ch] = jnp.exp(b_last)

    att, kv = {}, {}
    for bi, ch in items:
        rs = slice(ch * c, (ch + 1) * c)
        for hd in range(GLA_HEADS):
            ks = slice(hd * GLA_DK, (hd + 1) * GLA_DK)
            vh = v_ref[bi, rs, hd * GLA_DV:(hd + 1) * GLA_DV]
            att[bi, ch, hd] = jnp.where(
                causal, _dot_nt(q_t[bi, ch][:, ks], k_t[bi, ch][:, ks]), 0.0).astype(BF16)
            kv[bi, ch, hd] = _dot_tn(vh, k_d[bi, ch][:, ks])

    for bi in range(q_ref.shape[0]):
        for hd in range(GLA_HEADS):
            ks = slice(hd * GLA_DK, (hd + 1) * GLA_DK)
            vs = slice(hd * GLA_DV, (hd + 1) * GLA_DV)
            st = st_ref[bi, hd]
            for ch in range(BLOCK // c):
                rs = slice(ch * c, (ch + 1) * c)
                o = (_dot(att[bi, ch, hd], v_ref[bi, rs, vs])
                     + _dot_nt(q_t[bi, ch][:, ks], st.astype(BF16)))
                st = st * dec[bi, ch][:, ks] + kv[bi, ch, hd]
                o = _rms(o, on)
                r = r_ref[bi, rs, vs].astype(F32)
                o_ref[bi, rs, vs] = (o * (r * jax.nn.sigmoid(r))).astype(o_ref.dtype)
            st_ref[bi, hd] = st


def _gla(z, wg, bg, on, bb):
    batch, lp, _ = z.shape

    def col(width, start):
        return pl.BlockSpec((bb, BLOCK, width), lambda b, n: (b, n, start // width))

    const = lambda shape: pl.BlockSpec(shape, lambda b, n: (0,) * len(shape))
    return pl.pallas_call(
        _gla_kernel,
        out_shape=jax.ShapeDtypeStruct((batch, lp, GLA_V), BF16),
        grid=(batch // bb, lp // BLOCK),
        in_specs=[col(GLA_QK, Z_Q), col(GLA_QK, Z_K), col(GLA_V, Z_V), col(GLA_V, Z_R),
                  col(LANES, Z_LR), const((LANES, GLA_QK)), const((1, GLA_QK)),
                  const((1, GLA_DV))],
        out_specs=pl.BlockSpec((bb, BLOCK, GLA_V), lambda b, n: (b, n, 0)),
        scratch_shapes=[pltpu.VMEM((bb, GLA_HEADS, GLA_DV, GLA_DK), F32)],
        compiler_params=_params("parallel", "arbitrary"),
        name="gla",
    )(z, z, z, z, z, wg, bg, on)


def _group_mean_sq(x, bd):
    sq = x * x
    hi = sq.astype(BF16)
    lo = (sq - hi.astype(F32)).astype(BF16)
    return _dot(hi, bd) + _dot(lo, bd)


def _norm_rope(x, gain, cos, sin_a, sin_b, bd):
    nbat, rows, width = x.shape
    x2 = x.reshape(nbat * rows, width)
    parts = []
    for s in range(width // LANES):
        xs = x2[:, s * LANES:(s + 1) * LANES]
        ms = _group_mean_sq(xs, bd) * (1.0 / SWA_HEAD_DIM)
        parts.append(xs * lax.rsqrt(ms + EPS))
    xn = (parts[0] if len(parts) == 1 else jnp.concatenate(parts, axis=-1)) * gain
    half = SWA_HEAD_DIM // 2
    back = lambda t: t.reshape(nbat, rows, width)
    return (back(xn) * cos + back(pltpu.roll(xn, half, 1)) * sin_a
            + back(pltpu.roll(xn, width - half, 1)) * sin_b)


def _swa_kernel(sinks_ref, q_ref, kc_ref, kp_ref, vc_ref, vp_ref, cosc_ref, sac_ref, sbc_ref,
                cosp_ref, sap_ref, sbp_ref, qn_ref, kn_ref, bd_ref, o_ref):
    n = pl.program_id(1)
    nbat = q_ref.shape[0]
    bd = bd_ref[...]
    rep = SWA_Q // LANES
    tile = lambda t: jnp.concatenate([t] * rep, axis=-1)
    cos_c, sa_c, sb_c = cosc_ref[...], sac_ref[...], sbc_ref[...]
    q = _norm_rope(q_ref[...].astype(F32), qn_ref[...], tile(cos_c), tile(sa_c), tile(sb_c), bd)
    q = (q * (SWA_HEAD_DIM ** -0.5)).astype(BF16)
    k_c = _norm_rope(kc_ref[...].astype(F32), kn_ref[...], cos_c, sa_c, sb_c, bd).astype(BF16)
    k_p = _norm_rope(kp_ref[...].astype(F32), kn_ref[...], cosp_ref[...], sap_ref[...],
                     sbp_ref[...], bd).astype(BF16)

    gq = SWA_GROUP * BLOCK
    r = lax.broadcasted_iota(jnp.int32, (gq, 2 * BLOCK), 0) & (BLOCK - 1)
    cidx = lax.broadcasted_iota(jnp.int32, (gq, 2 * BLOCK), 1)
    allowed = (cidx > r) & (cidx <= r + SWA_WINDOW) & ((n - 1) * BLOCK + cidx >= FRONT)
    rgroup = lax.broadcasted_iota(jnp.int32, (gq, 1), 0) // BLOCK
    items = [(bi, hk) for bi in range(nbat) for hk in range(SWA_KV_HEADS)]

    s, sink = {}, {}
    for hk in range(SWA_KV_HEADS):
        sk = jnp.zeros((gq, 1), F32)
        for g in range(SWA_GROUP):
            sk = jnp.where(rgroup == g, sinks_ref[hk * SWA_GROUP + g], sk)
        sink[hk] = sk
    for bi, hk in items:
        ks = slice(hk * SWA_HEAD_DIM, (hk + 1) * SWA_HEAD_DIM)
        qg = jnp.concatenate(
            [q[bi][:, (hk * SWA_GROUP + g) * SWA_HEAD_DIM:(hk * SWA_GROUP + g + 1) * SWA_HEAD_DIM]
             for g in range(SWA_GROUP)], axis=0)
        k = jnp.concatenate([k_p[bi][:, ks], k_c[bi][:, ks]], axis=0)
        s[bi, hk] = jnp.where(allowed, _dot_nt(qg, k), NEG_INF)

    p, denom = {}, {}
    for bi, hk in items:
        m = jnp.maximum(jnp.max(s[bi, hk], axis=-1, keepdims=True), sink[hk])
        e = jnp.exp(s[bi, hk] - m)
        denom[bi, hk] = jnp.sum(e, axis=-1, keepdims=True) + jnp.exp(sink[hk] - m)
        p[bi, hk] = e.astype(BF16)

    for bi, hk in items:
        ks = slice(hk * SWA_HEAD_DIM, (hk + 1) * SWA_HEAD_DIM)
        v = jnp.concatenate([vp_ref[bi, :, ks], vc_ref[bi, :, ks]], axis=0)
        o = _dot(p[bi, hk], v) / denom[bi, hk]
        for g in range(SWA_GROUP):
            hq = hk * SWA_GROUP + g
            o_ref[bi, :, hq * SWA_HEAD_DIM:(hq + 1) * SWA_HEAD_DIM] = (
                o[g * BLOCK:(g + 1) * BLOCK].astype(o_ref.dtype))


def _swa(z, sinks, cos, sin_a, sin_b, qn, kn, bd, bb):
    batch, lp, _ = z.shape

    def cur(width, start):
        return pl.BlockSpec((bb, BLOCK, width), lambda b, n, s: (b, n, start // width))

    def prev(width, start):
        return pl.BlockSpec((bb, BLOCK, width),
                            lambda b, n, s: (b, jnp.maximum(n - 1, 0), start // width))

    tab_c = pl.BlockSpec((BLOCK, LANES), lambda b, n, s: (n, 0))
    tab_p = pl.BlockSpec((BLOCK, LANES), lambda b, n, s: (jnp.maximum(n - 1, 0), 0))
    const = lambda shape: pl.BlockSpec(shape, lambda b, n, s: (0,) * len(shape))
    grid_spec = pltpu.PrefetchScalarGridSpec(
        num_scalar_prefetch=1,
        grid=(batch // bb, lp // BLOCK),
        in_specs=[cur(SWA_Q, Z_SQ), cur(SWA_KV, Z_SK), prev(SWA_KV, Z_SK),
                  cur(SWA_KV, Z_SV), prev(SWA_KV, Z_SV),
                  tab_c, tab_c, tab_c, tab_p, tab_p, tab_p,
                  const((1, SWA_Q)), const((1, SWA_KV)), const((LANES, LANES))],
        out_specs=pl.BlockSpec((bb, BLOCK, SWA_Q), lambda b, n, s: (b, n, 0)),
    )
    return pl.pallas_call(
        _swa_kernel,
        out_shape=jax.ShapeDtypeStruct((batch, lp, SWA_Q), BF16),
        grid_spec=grid_spec,
        compiler_params=_params("parallel", "arbitrary"),
        name="swa",
    )(sinks, z, z, z, z, z, cos, sin_a, sin_b, cos, sin_a, sin_b, qn, kn, bd)


def _ffn_kernel(h_ref, oa_ref, ob_ref, woa_ref, wob_ref, g_ref, w1_ref, w3_ref, w2_ref,
                out_ref, h1_sc, xn_sc, acc_sc):
    j = pl.program_id(1)

    nj = pl.num_programs(1)

    @pl.when(j == 0)
    def _():
        h1 = h_ref[...] + _dot(oa_ref[...], woa_ref[...]) + _dot(ob_ref[...], wob_ref[...])
        h1_sc[...] = h1
        xn_sc[...] = _rms(h1, g_ref[...]).astype(BF16)

    x = xn_sc[...]
    a = _dot(x, w1_ref[...])
    b = _dot(x, w3_ref[...])
    hid = (a * jax.nn.sigmoid(a) * b).astype(BF16)
    part = _dot(hid, w2_ref[...])

    @pl.when(j == 0)
    def _():
        acc_sc[...] = h1_sc[...] + part

    @pl.when((j > 0) & (j < nj - 1))
    def _():
        acc_sc[...] += part

    @pl.when(j == nj - 1)
    def _():
        out_ref[...] = acc_sc[...] + part


def _mix_out_ffn(h, oa, ob, woa, wob, g, w1, w3, w2, tm, tf):
    rows = h.shape[0]
    dff = w1.shape[1]
    return pl.pallas_call(
        _ffn_kernel,
        out_shape=jax.ShapeDtypeStruct((rows, D_MODEL), F32),
        grid=(rows // tm, dff // tf),
        in_specs=[pl.BlockSpec((tm, D_MODEL), lambda i, j: (i, 0)),
                  pl.BlockSpec((tm, GLA_V), lambda i, j: (i, 0)),
                  pl.BlockSpec((tm, SWA_Q), lambda i, j: (i, 0)),
                  pl.BlockSpec((GLA_V, D_MODEL), lambda i, j: (0, 0)),
                  pl.BlockSpec((SWA_Q, D_MODEL), lambda i, j: (0, 0)),
                  pl.BlockSpec((1, D_MODEL), lambda i, j: (0, 0)),
                  pl.BlockSpec((D_MODEL, tf), lambda i, j: (0, j)),
                  pl.BlockSpec((D_MODEL, tf), lambda i, j: (0, j)),
                  pl.BlockSpec((tf, D_MODEL), lambda i, j: (j, 0))],
        out_specs=pl.BlockSpec((tm, D_MODEL), lambda i, j: (i, 0)),
        scratch_shapes=[pltpu.VMEM((tm, D_MODEL), F32), pltpu.VMEM((tm, D_MODEL), BF16),
                        pltpu.VMEM((tm, D_MODEL), F32)],
        compiler_params=_params("parallel", "arbitrary"),
        name="mix_out_ffn",
    )(h, oa, ob, woa, wob, g, w1, w3, w2)


CONV_HALO = 32
CONV_SPAN = CONV_HALO + BLOCK
CONV_ROWS = 32
SUBLANES = 8


def _conv_kernel(h_ref, g_ref, wa_ref, wg_ref, shift_ref, wdw_ref, bdw_ref, lg_ref, lb_ref,
                 w2_ref, out_ref, u_sc, xs_sc, c_sc, y_sc):
    h = h_ref[...]
    xn = _rms(h, g_ref[...]).astype(BF16)
    u = _dot(xn, wa_ref[...]) * jax.nn.sigmoid(_dot(xn, wg_ref[...]))
    u_sc[CONV_HALO:CONV_SPAN, :] = u.astype(BF16)

    @pl.when(pl.program_id(1) > 0)
    def _():
        span = u_sc[...]
        xs_sc[0] = span.astype(F32)
        shifted = _dot(shift_ref[...], span)
        for r in range(1, SUBLANES):
            xs_sc[r] = shifted[(r - 1) * CONV_SPAN:r * CONV_SPAN]
        first = CONV_HALO - (CONV_WIDTH - 1)
        for cb in range(D_MODEL // LANES):
            cs = slice(cb * LANES, (cb + 1) * LANES)
            acc = jnp.zeros((BLOCK // SUBLANES, SUBLANES, LANES), F32)
            for j in range(CONV_WIDTH):
                r = (first + j) % SUBLANES
                base = first + j - r
                x = xs_sc[r, base:base + BLOCK, cs].reshape(BLOCK // SUBLANES, SUBLANES, LANES)
                acc = acc + x * wdw_ref[j, :, cs]
            c_sc[:, cs] = acc.reshape(BLOCK, LANES)
        for rb in range(BLOCK // CONV_ROWS):
            acc = c_sc[rb * CONV_ROWS:(rb + 1) * CONV_ROWS, :] + bdw_ref[...]
            mu = jnp.mean(acc, axis=-1, keepdims=True)
            dev = acc - mu
            var = jnp.mean(dev * dev, axis=-1, keepdims=True)
            y = dev * lax.rsqrt(var + EPS) * lg_ref[...] + lb_ref[...]
            y_sc[rb * CONV_ROWS:(rb + 1) * CONV_ROWS, :] = (y * jax.nn.sigmoid(y)).astype(BF16)
        out_ref[...] = h + _dot(y_sc[...], w2_ref[...])

    u_sc[0:CONV_HALO, :] = u_sc[BLOCK:CONV_SPAN, :]


def _conv_module(h, g, w_pw1, shift, wdw, bdw, lg, lb, w2, batch, nb):
    const = lambda shape: pl.BlockSpec(shape, lambda b, n: (0,) * len(shape))
    return pl.pallas_call(
        _conv_kernel,
        out_shape=jax.ShapeDtypeStruct((batch * (nb - 1) * BLOCK, D_MODEL), F32),
        grid=(batch, nb),
        in_specs=[pl.BlockSpec((BLOCK, D_MODEL), lambda b, n: (b * nb + n, 0)),
                  const((1, D_MODEL)),
                  pl.BlockSpec((D_MODEL, D_MODEL), lambda b, n: (0, 0)),
                  pl.BlockSpec((D_MODEL, D_MODEL), lambda b, n: (0, 1)),
                  const(((SUBLANES - 1) * CONV_SPAN, CONV_SPAN)),
                  const((CONV_WIDTH, SUBLANES, D_MODEL)), const((1, D_MODEL)), const((1, D_MODEL)),
                  const((1, D_MODEL)), const((D_MODEL, D_MODEL))],
        out_specs=pl.BlockSpec((BLOCK, D_MODEL),
                               lambda b, n: (b * (nb - 1) + jnp.maximum(n - 1, 0), 0)),
        scratch_shapes=[pltpu.VMEM((CONV_SPAN, D_MODEL), BF16),
                        pltpu.VMEM((SUBLANES, CONV_SPAN, D_MODEL), F32),
                        pltpu.VMEM((BLOCK, D_MODEL), F32),
                        pltpu.VMEM((BLOCK, D_MODEL), BF16)],
        compiler_params=_params("parallel", "arbitrary"),
        name="conv_module",
    )(h, g, w_pw1, w_pw1, shift, wdw, bdw, lg, lb, w2)


SORT_TOKENS = 512
GROUP = 16
SORT_CAP = 2 * SORT_TOKENS + N_EXPERTS * GROUP
SORT_COLS = D_MODEL + LANES
MOE_ROWS = 512
GROUPS_PER_TILE = MOE_ROWS // GROUP


def _one_hot_rows(pos1, pos2, width):
    r = lax.broadcasted_iota(jnp.int32, (pos1.shape[0], width), 1).astype(F32)
    return jnp.where(r == pos1, 1.0, jnp.where(r == pos2, 1.0, 0.0)).astype(BF16)


SORT_TILES_PER_STEP = 2
SPARE_ZERO = 0
SPARE_SINK = 64
assert SPARE_SINK + 2 * MOE_ROWS <= SORT_CAP


def _route_sort_kernel(h_ref, g_ref, wr_hi_ref, wr_lo_ref, xs_ref, pos_ref, ng_ref):
    is_spare = pl.program_id(0) == pl.num_programs(0) - 1

    @pl.when(is_spare)
    def _():
        xs_ref[...] = jnp.zeros_like(xs_ref)

    @pl.when(jnp.logical_not(is_spare))
    def _():
        tiles = [_route_sort_tile(h_ref.at[pl.ds(t * SORT_TOKENS, SORT_TOKENS)], g_ref,
                                  wr_hi_ref, wr_lo_ref,
                                  xs_ref.at[pl.ds(t * SORT_CAP, SORT_CAP)],
                                  pos_ref.at[pl.ds(t * SORT_TOKENS, SORT_TOKENS)], ng_ref.at[t])
                 for t in range(h_ref.shape[0] // SORT_TOKENS)]
        while tiles:
            tiles = [t for t in tiles if next(t, "done") != "done"]


def _route_sort_tile(h_ref, g_ref, wr_hi_ref, wr_lo_ref, xs_ref, pos_ref, ng_ref):
    s = SORT_TOKENS
    xn = _rms(h_ref[...], g_ref[...])
    x_hi = xn.astype(BF16)
    x_lo = (xn - x_hi.astype(F32)).astype(BF16)
    logits = (_dot(x_hi, wr_hi_ref[...]) + _dot(x_lo, wr_hi_ref[...])
              + _dot(x_hi, wr_lo_ref[...]))
    yield
    lane = lax.broadcasted_iota(jnp.int32, logits.shape, 1)
    logits = jnp.where(lane < N_EXPERTS, logits, -jnp.inf)
    m1 = jnp.max(logits, axis=-1, keepdims=True)
    i1 = jnp.min(jnp.where(logits == m1, lane, LANES), axis=-1, keepdims=True)
    rest = jnp.where(lane == i1, -jnp.inf, logits)
    m2 = jnp.max(rest, axis=-1, keepdims=True)
    i2 = jnp.min(jnp.where(rest == m2, lane, LANES), axis=-1, keepdims=True)
    e2 = jnp.exp(m2 - m1)
    g1 = 1.0 / (1.0 + e2)
    g2 = e2 / (1.0 + e2)
    yield

    oh1 = lane == i1
    oh2 = lane == i2
    oh = jnp.where(oh1, 1.0, jnp.where(oh2, 1.0, 0.0))
    cnt = jnp.sum(oh, axis=0, keepdims=True)
    ngr = jnp.floor((cnt + (GROUP - 1)) * (1.0 / GROUP))
    ri = lax.broadcasted_iota(jnp.int32, (s, s), 0)
    ci = lax.broadcasted_iota(jnp.int32, (s, s), 1)
    rank = _dot(jnp.where(ri > ci, 1.0, 0.0).astype(BF16), oh.astype(BF16))
    li = lax.broadcasted_iota(jnp.int32, (LANES, LANES), 0)
    lj = lax.broadcasted_iota(jnp.int32, (LANES, LANES), 1)
    before = _dot(jnp.broadcast_to(ngr, (8, LANES)).astype(BF16),
                  jnp.where(li < lj, 1.0, 0.0).astype(BF16))[0:1]
    yield
    base = before * GROUP + rank
    base1 = jnp.where(oh1, base, 0.0)
    base2 = jnp.where(oh2, base, 0.0)
    pos1 = jnp.sum(base1, axis=-1, keepdims=True)
    pos2 = jnp.sum(base2, axis=-1, keepdims=True)

    def as_row(v):
        ones = jnp.ones((8, LANES), BF16)
        hi = jnp.floor(v * (1.0 / 256.0))
        return (_dot_nt(ones, hi.astype(BF16)) * 256.0
                + _dot_nt(ones, (v - hi * 256.0).astype(BF16)))[0:1]

    row1, row2 = as_row(base1), as_row(base2)
    yield
    r = lax.broadcasted_iota(jnp.int32, (SORT_CAP, s), 0).astype(F32)
    p1 = jnp.where(r == row1, 1.0, 0.0).astype(BF16)
    p2 = jnp.where(r == row2, 1.0, 0.0).astype(BF16)
    xs_ref[:, 0:D_MODEL] = _dot(p1 + p2, x_hi).astype(BF16)

    def gate_lanes(gv):
        hi = gv.astype(BF16).astype(F32)
        return jnp.where(lane == 0, hi, jnp.where(lane == 1, gv - hi, 0.0)).astype(BF16)

    xs_ref[:, D_MODEL:SORT_COLS] = (_dot(p1, gate_lanes(g1))
                                    + _dot(p2, gate_lanes(g2))).astype(BF16)
    pos_ref[...] = jnp.where(lane == 0, pos1, jnp.where(lane == 1, pos2, 0.0))
    ng_ref[...] = jnp.broadcast_to(ngr, (8, LANES)).astype(jnp.int32)


def _route_sort(h, g, wr_hi, wr_lo):
    tokens = h.shape[0]
    nt = tokens // SORT_TOKENS
    tps = math.gcd(nt, SORT_TILES_PER_STEP)
    steps = nt // tps
    last = lambda i: jnp.minimum(i, steps - 1)
    return pl.pallas_call(
        _route_sort_kernel,
        out_shape=(jax.ShapeDtypeStruct(((nt + tps) * SORT_CAP, SORT_COLS), BF16),
                   jax.ShapeDtypeStruct((tokens, LANES), F32),
                   jax.ShapeDtypeStruct((nt, 8, LANES), jnp.int32)),
        grid=(steps + 1,),
        in_specs=[pl.BlockSpec((tps * SORT_TOKENS, D_MODEL), lambda i: (last(i), 0)),
                  pl.BlockSpec((1, D_MODEL), lambda i: (0, 0)),
                  pl.BlockSpec((D_MODEL, LANES), lambda i: (0, 0)),
                  pl.BlockSpec((D_MODEL, LANES), lambda i: (0, 0))],
        out_specs=(pl.BlockSpec((tps * SORT_CAP, SORT_COLS), lambda i: (i, 0)),
                   pl.BlockSpec((tps * SORT_TOKENS, LANES), lambda i: (last(i), 0)),
                   pl.BlockSpec((tps, 8, LANES), lambda i: (last(i), 0, 0))),
        compiler_params=_params("arbitrary"),
        name="route_sort",
    )(h, g, wr_hi, wr_lo)


def _group_tables(ng, n_row_tiles):
    nt, ne = ng.shape
    gpt = GROUPS_PER_TILE
    n_e = jnp.sum(ng, axis=0)
    rt_e = (n_e + gpt - 1) // gpt
    rt_end = jnp.cumsum(rt_e)
    n_rt = rt_end[-1]
    r = jnp.arange(n_row_tiles, dtype=jnp.int32)
    valid_r = r < n_rt
    e_r = jnp.minimum(jnp.sum((r[:, None] >= rt_end[None, :]).astype(jnp.int32), axis=1), ne - 1)
    e_last = jnp.sum(jnp.where(r == n_rt - 1, e_r, 0))
    e_r = jnp.where(valid_r, e_r, e_last)
    sel = (e_r[:, None] == jnp.arange(ne, dtype=jnp.int32)[None, :]).astype(jnp.int32)
    pick = lambda v: jnp.sum(sel * v[None, :], axis=1)
    k0 = (r - pick(rt_end - rt_e)) * gpt
    gidx = k0[:, None] + jnp.arange(gpt, dtype=jnp.int32)[None, :]
    valid = valid_r[:, None] & (gidx < pick(n_e)[:, None])
    incl = jnp.cumsum(ng, axis=0)
    incl_r = jnp.sum(sel[:, None, :] * incl[None, :, :], axis=2)
    ng_r = jnp.sum(sel[:, None, :] * ng[None, :, :], axis=2)
    lo = jnp.cumsum(ng, axis=1) - ng
    lo_r = jnp.sum(sel[:, None, :] * lo[None, :, :], axis=2)
    ti = jnp.sum((incl_r[:, None, :] <= gidx[:, :, None]).astype(jnp.int32), axis=2)
    ti = jnp.minimum(ti, nt - 1)
    tsel = (ti[:, :, None] == jnp.arange(nt, dtype=jnp.int32)[None, None, :]).astype(jnp.int32)
    tpick = lambda v: jnp.sum(tsel * v[:, None, :], axis=2)
    within = gidx - tpick(incl_r - ng_r)
    rows = ti * SORT_CAP + GROUP * (tpick(lo_r) + within)
    spare = nt * SORT_CAP
    sink = (spare + SPARE_SINK + (r % 2)[:, None] * MOE_ROWS
            + GROUP * jnp.arange(gpt, dtype=jnp.int32)[None, :])
    src = jnp.where(valid, rows, spare + SPARE_ZERO).reshape(-1)
    dst = jnp.where(valid, rows, sink).reshape(-1)
    return e_r, valid_r.astype(jnp.int32), src, dst


def _group_copies(rows_ref, tile, hbm, buf, slot, sem, to_hbm, start):
    for k in range(GROUPS_PER_TILE):
        off = pl.multiple_of(rows_ref[tile * GROUPS_PER_TILE + k], GROUP)
        buf_rows = buf.at[slot, pl.ds(k * GROUP, GROUP), :]
        hbm_rows = hbm.at[pl.ds(off, GROUP), pl.ds(0, buf_rows.shape[-1])]
        cp = (pltpu.make_async_copy(buf_rows, hbm_rows, sem.at[slot]) if to_hbm
              else pltpu.make_async_copy(hbm_rows, buf_rows, sem.at[slot]))
        if start:
            cp.start()
        else:
            cp.wait()


def _moe_group_kernel(te_ref, tv_ref, src_ref, dst_ref, xs_hbm, w1_ref, w3_ref, w2_ref, ys_hbm,
                      xbuf, ybuf, acc_sc, xsem, ysem):
    r = pl.program_id(0)
    j = pl.program_id(1)
    nr = pl.num_programs(0)
    nj = pl.num_programs(1)
    slot = lax.rem(r, 2)
    valid = lambda t: tv_ref[jnp.clip(t, 0, nr - 1)] > 0
    gather = functools.partial(_group_copies, src_ref, hbm=xs_hbm, buf=xbuf, sem=xsem,
                               to_hbm=False)
    scatter = functools.partial(_group_copies, dst_ref, hbm=ys_hbm, buf=ybuf, sem=ysem,
                                to_hbm=True)

    @pl.when(j == 0)
    def _():
        @pl.when((r == 0) & valid(r))
        def _():
            gather(tile=r, slot=slot, start=True)

        @pl.when(valid(r))
        def _():
            gather(tile=r, slot=slot, start=False)

        @pl.when((r + 1 < nr) & valid(r + 1))
        def _():
            gather(tile=r + 1, slot=1 - slot, start=True)

        @pl.when((r >= 2) & valid(r - 2))
        def _():
            scatter(tile=r - 2, slot=slot, start=False)

    @pl.when(valid(r))
    def _():
        x = xbuf[slot, :, 0:D_MODEL]
        a = _dot(x, w1_ref[0])
        b = _dot(x, w3_ref[0])
        hid = (a * jax.nn.sigmoid(a) * b).astype(BF16)
        part = _dot(hid, w2_ref[0])

        @pl.when(j == 0)
        def _():
            acc_sc[...] = part

        @pl.when((j > 0) & (j < nj - 1))
        def _():
            acc_sc[...] += part

        @pl.when(j == nj - 1)
        def _():
            gl = xbuf[slot, :, D_MODEL:SORT_COLS].astype(F32)
            gate = gl[:, 0:1] + gl[:, 1:2]
            ybuf[slot] = ((acc_sc[...] + part) * gate).astype(BF16)
            scatter(tile=r, slot=slot, start=True)

    @pl.when((r == nr - 1) & (j == nj - 1))
    def _():
        @pl.when((r >= 1) & valid(r - 1))
        def _():
            scatter(tile=r - 1, slot=1 - slot, start=False)

        @pl.when(valid(r))
        def _():
            scatter(tile=r, slot=slot, start=False)


def _moe_group(xs, tile_expert, tile_valid, src, dst, w1, w3, w2, n_row_tiles, tf):
    ne, _, dff = w1.shape
    nj = dff // tf
    assert nj >= 2

    def w_in(r, j, te, tv, s, d):
        return (te[r], 0, jnp.where(tv[r] > 0, j, nj - 1))

    def w_out(r, j, te, tv, s, d):
        return (te[r], jnp.where(tv[r] > 0, j, nj - 1), 0)

    grid_spec = pltpu.PrefetchScalarGridSpec(
        num_scalar_prefetch=4,
        grid=(n_row_tiles, nj),
        in_specs=[pl.BlockSpec(memory_space=pl.ANY),
                  pl.BlockSpec((1, D_MODEL, tf), w_in),
                  pl.BlockSpec((1, D_MODEL, tf), w_in),
                  pl.BlockSpec((1, tf, D_MODEL), w_out)],
        out_specs=pl.BlockSpec(memory_space=pl.ANY),
        scratch_shapes=[pltpu.VMEM((2, MOE_ROWS, SORT_COLS), BF16),
                        pltpu.VMEM((2, MOE_ROWS, D_MODEL), BF16),
                        pltpu.VMEM((MOE_ROWS, D_MODEL), F32),
                        pltpu.SemaphoreType.DMA((2,)),
                        pltpu.SemaphoreType.DMA((2,))],
    )
    return pl.pallas_call(
        _moe_group_kernel,
        out_shape=jax.ShapeDtypeStruct(xs.shape, xs.dtype),
        grid_spec=grid_spec,
        input_output_aliases={4: 0},
        compiler_params=_params("arbitrary", "arbitrary"),
        name="moe_group",
    )(tile_expert, tile_valid, src, dst, xs, w1, w3, w2)


def _combine_kernel(h_ref, pos_ref, ys_ref, out_ref):
    pos = pos_ref[...]
    pt = _one_hot_rows(pos[:, 0:1], pos[:, 1:2], SORT_CAP)
    out_ref[...] = h_ref[...] + _dot(pt, ys_ref[...])


def _combine(h, pos, ys):
    tokens = h.shape[0]
    return pl.pallas_call(
        _combine_kernel,
        out_shape=jax.ShapeDtypeStruct((tokens, D_MODEL), F32),
        grid=(tokens // SORT_TOKENS,),
        in_specs=[pl.BlockSpec((SORT_TOKENS, D_MODEL), lambda i: (i, 0)),
                  pl.BlockSpec((SORT_TOKENS, LANES), lambda i: (i, 0)),
                  pl.BlockSpec((SORT_CAP, D_MODEL), lambda i: (i, 0))],
        out_specs=pl.BlockSpec((SORT_TOKENS, D_MODEL), lambda i: (i, 0)),
        compiler_params=_params("parallel"),
        name="combine",
    )(h, pos, ys)


def _row_tile(rows, want):
    tm = want
    while rows % tm:
        tm //= 2
    return tm


def _rope_tables(lp):
    inv_freq = 1.0 / (ROPE_THETA ** (jnp.arange(0, SWA_HEAD_DIM, 2, dtype=F32) / SWA_HEAD_DIM))
    pos = jnp.arange(lp, dtype=F32) - FRONT
    ang = pos[:, None] * inv_freq[None, :]
    cos, sin = jnp.cos(ang), jnp.sin(ang)
    zero = jnp.zeros_like(sin)
    rep = LANES // SWA_HEAD_DIM
    cos_t = jnp.tile(jnp.concatenate([cos, cos], axis=-1), (1, rep))
    sin_a = jnp.tile(jnp.concatenate([zero, sin], axis=-1), (1, rep))
    sin_b = jnp.tile(jnp.concatenate([-sin, zero], axis=-1), (1, rep))
    return cos_t, sin_a, sin_b


def kernel(x, meta, a_norm, a_w_in, a_w_gate2, a_b_gate, a_q_norm, a_k_norm, a_sinks, a_o_norm,
           a_w_out, f_norm, f_w1, f_w3, f_w2, c_norm, c_w_pw1, c_w_dw, c_b_dw, c_ln_g, c_ln_b,
           c_w_pw2, m_norm, m_w_router, m_w1, m_w3, m_w2):
    batch, seq, _ = x.shape
    assert seq % BLOCK == 0
    lp = BLOCK + seq
    nb = lp // BLOCK
    rows = batch * lp

    head = jnp.concatenate([jnp.zeros((FRONT, D_MODEL), x.dtype), meta.astype(x.dtype)], axis=0)
    h = jnp.concatenate([jnp.broadcast_to(head[None], (batch, BLOCK, D_MODEL)), x], axis=1)
    h = h.reshape(rows, D_MODEL)

    w_in = a_w_in[0]
    gq, gk, gv, gr, glr, sq, sk, sv = jnp.split(
        w_in, [256, 512, 1024, 1536, 1552, 2064, 2192], axis=-1)
    w_in_r = jnp.concatenate(
        [gq, gk, gv, gr, sq, sk, sv, glr,
         jnp.zeros((D_MODEL, Z_COLS - Z_LR - GLA_LOWRANK), w_in.dtype)], axis=-1).astype(BF16)
    z = _in_proj(h, a_norm[0][None], w_in_r, _row_tile(rows, 512))

    wg = jnp.zeros((LANES, GLA_QK), F32).at[:GLA_LOWRANK].set(a_w_gate2[0]).astype(BF16)
    z = z.reshape(batch, lp, Z_COLS)
    bb = math.gcd(batch, MIX_BATCH)
    o_a = _gla(z, wg, a_b_gate[0][None], a_o_norm[0][None], bb).reshape(rows, GLA_V)

    cos_t, sin_a, sin_b = _rope_tables(lp)
    qn = jnp.tile(a_q_norm[0], SWA_Q_HEADS)[None]
    kn = jnp.tile(a_k_norm[0], SWA_KV_HEADS)[None]
    lane = jnp.arange(LANES)
    bd = (lane[:, None] // SWA_HEAD_DIM == lane[None, :] // SWA_HEAD_DIM).astype(BF16)
    o_b = _swa(z, a_sinks[0], cos_t, sin_a, sin_b, qn, kn, bd, bb).reshape(rows, SWA_Q)

    w_out = a_w_out[0].astype(BF16)
    h = _mix_out_ffn(h, o_a, o_b, w_out[:GLA_V], w_out[GLA_V:], f_norm[0][None],
                     f_w1[0].astype(BF16), f_w3[0].astype(BF16), f_w2[0].astype(BF16),
                     _row_tile(rows, 512), FF_TILE)

    si = jnp.arange((SUBLANES - 1) * CONV_SPAN)
    shift = (jnp.arange(CONV_SPAN)[None, :]
             == (si % CONV_SPAN + si // CONV_SPAN + 1)[:, None]).astype(BF16)
    h = _conv_module(h, c_norm[0][None], c_w_pw1[0].astype(BF16), shift,
                     jnp.broadcast_to(c_w_dw[0][:, None, :], (CONV_WIDTH, SUBLANES, D_MODEL)),
                     c_b_dw[0][None], c_ln_g[0][None], c_ln_b[0][None],
                     c_w_pw2[0].astype(BF16), batch, nb)

    wr = jnp.zeros((D_MODEL, LANES), F32).at[:, :N_EXPERTS].set(m_w_router[0])
    wr_hi = wr.astype(BF16)
    wr_lo = (wr - wr_hi.astype(F32)).astype(BF16)
    tokens = batch * seq
    assert tokens % SORT_TOKENS == 0
    nt = tokens // SORT_TOKENS
    xs, pos, ng = _route_sort(h, m_norm[0][None], wr_hi, wr_lo)
    ng = ng[:, 0, :N_EXPERTS]
    n_row_tiles = (nt * (SORT_CAP // GROUP)) // GROUPS_PER_TILE + N_EXPERTS
    tile_expert, tile_valid, src, dst = _group_tables(ng, n_row_tiles)
    ys = _moe_group(xs, tile_expert, tile_valid, src, dst, m_w1[0].astype(BF16),
                    m_w3[0].astype(BF16), m_w2[0].astype(BF16), n_row_tiles, FF_TILE)
    out = _combine(h, pos, ys)
    return out.reshape(batch, seq, D_MODEL)
```

```python
import functools
import math

import jax
import jax.numpy as jnp
from jax import lax
from jax.experimental import pallas as pl
from jax.experimental.pallas import tpu as pltpu

F32 = jnp.float32
BF16 = jnp.bfloat16

D_MODEL = 1024
N_META = 16
EPS = 1e-6
ROPE_THETA = 10000.0
NEG_INF = -1e30
LOG2E = math.log2(math.e)
GLA_HEADS = 4
GLA_DK = 64
GLA_DV = 128
GLA_LOWRANK = 16
GLA_TAU = 16.0
GLA_CHUNK = 64
SWA_Q_HEADS = 8
SWA_KV_HEADS = 2
SWA_HEAD_DIM = 64
SWA_WINDOW = 128
CONV_WIDTH = 31
D_FF = 3584
N_EXPERTS = 8

GLA_QK = GLA_HEADS * GLA_DK
GLA_V = GLA_HEADS * GLA_DV
SWA_Q = SWA_Q_HEADS * SWA_HEAD_DIM
SWA_KV = SWA_KV_HEADS * SWA_HEAD_DIM
SWA_GROUP = SWA_Q_HEADS // SWA_KV_HEADS

LANES = 128
FF_TILE = 1792
BLOCK = 128
FRONT = BLOCK - N_META
VMEM_LIMIT = 56 * 1024 * 1024

Z_Q, Z_K, Z_V, Z_R = 0, 256, 512, 1024
Z_SQ, Z_SK, Z_SV, Z_LR = 1536, 2048, 2176, 2304
Z_COLS = 2432


def _rms(x, g):
    return x * lax.rsqrt(jnp.mean(x * x, axis=-1, keepdims=True) + EPS) * g


def _dot(a, b):
    return jnp.dot(a, b, preferred_element_type=F32)


def _dot_nt(a, b):
    return lax.dot_general(a, b, (((1,), (1,)), ((), ())), preferred_element_type=F32)


def _dot_tn(a, b):
    return lax.dot_general(a, b, (((0,), (0,)), ((), ())), preferred_element_type=F32)


def _params(*sem):
    return pltpu.CompilerParams(dimension_semantics=sem, vmem_limit_bytes=VMEM_LIMIT)


def _in_proj_kernel(h_ref, g_ref, w_ref, z_ref):
    xn = _rms(h_ref[...], g_ref[...]).astype(BF16)
    z_ref[...] = _dot(xn, w_ref[...]).astype(z_ref.dtype)


def _in_proj(h, g, w, tm):
    rows = h.shape[0]
    n = w.shape[1]
    return pl.pallas_call(
        _in_proj_kernel,
        out_shape=jax.ShapeDtypeStruct((rows, n), BF16),
        grid=(rows // tm,),
        in_specs=[pl.BlockSpec((tm, D_MODEL), lambda i: (i, 0)),
                  pl.BlockSpec((1, D_MODEL), lambda i: (0, 0)),
                  pl.BlockSpec((D_MODEL, n), lambda i: (0, 0))],
        out_specs=pl.BlockSpec((tm, n), lambda i: (i, 0)),
        compiler_params=_params("parallel"),
        name="in_proj",
    )(h, g, w)


MIX_BATCH = 4


def _gla_kernel(q_ref, k_ref, v_ref, r_ref, lr_ref, wg_ref, bg_ref, on_ref, o_ref, st_ref):
    n = pl.program_id(1)

    @pl.when(n == 0)
    def _():
        st_ref[...] = jnp.zeros_like(st_ref)

    c = GLA_CHUNK
    items = [(bi, ch) for bi in range(q_ref.shape[0]) for ch in range(BLOCK // c)]
    ri = lax.broadcasted_iota(jnp.int32, (BLOCK, BLOCK), 0)
    ci = lax.broadcasted_iota(jnp.int32, (BLOCK, BLOCK), 1)
    tri = jnp.where((ri >= ci) & (ri // c == ci // c), 1.0, 0.0).astype(BF16)
    causal = (lax.broadcasted_iota(jnp.int32, (c, c), 0)
              >= lax.broadcasted_iota(jnp.int32, (c, c), 1))
    row = lax.broadcasted_iota(jnp.int32, (BLOCK, GLA_QK), 0) + n * BLOCK
    on = on_ref[...]

    b_all = []
    for bi in range(q_ref.shape[0]):
        gp = _dot(lr_ref[bi], wg_ref[...]) + bg_ref[...]
        g = (jnp.minimum(gp, 0.0) - jnp.log(1.0 + jnp.exp(-jnp.abs(gp)))) * (1.0 / GLA_TAU)
        g = jnp.where(row >= FRONT, g, 0.0)
        g_hi = g.astype(BF16)
        g_lo = (g - g_hi.astype(F32)).astype(BF16)
        b_all.append(_dot(tri, g_hi) + _dot(tri, g_lo))

    q_t, k_t, k_d, dec = {}, {}, {}, {}
    for bi, ch in items:
        rs = slice(ch * c, (ch + 1) * c)
        b = b_all[bi][rs]
        b_last = b_all[bi][(ch + 1) * c - 1:(ch + 1) * c, :]
        qf = q_ref[bi, rs, :].astype(F32)
        kf = k_ref[bi, rs, :].astype(F32)
        q_t[bi, ch] = (qf * jnp.exp(b) * (GLA_DK ** -0.5)).astype(BF16)
        k_t[bi, ch] = (kf * jnp.exp(-b)).astype(BF16)
        k_d[bi, ch] = (kf * jnp.exp(b_last - b)).astype(BF16)
        dec[bi, ch] = jnp.exp(b_last)

    att, kv = {}, {}
    for bi, ch in items:
        rs = slice(ch * c, (ch + 1) * c)
        for hd in range(GLA_HEADS):
            ks = slice(hd * GLA_DK, (hd + 1) * GLA_DK)
            vh = v_ref[bi, rs, hd * GLA_DV:(hd + 1) * GLA_DV]
            att[bi, ch, hd] = jnp.where(
                causal, _dot_nt(q_t[bi, ch][:, ks], k_t[bi, ch][:, ks]), 0.0).astype(BF16)
            kv[bi, ch, hd] = _dot_tn(vh, k_d[bi, ch][:, ks])

    for bi in range(q_ref.shape[0]):
        for hd in range(GLA_HEADS):
            ks = slice(hd * GLA_DK, (hd + 1) * GLA_DK)
            vs = slice(hd * GLA_DV, (hd + 1) * GLA_DV)
            st = st_ref[bi, hd]
            for ch in range(BLOCK // c):
                rs = slice(ch * c, (ch + 1) * c)
                o = (_dot(att[bi, ch, hd], v_ref[bi, rs, vs])
                     + _dot_nt(q_t[bi, ch][:, ks], st.astype(BF16)))
                st = st * dec[bi, ch][:, ks] + kv[bi, ch, hd]
                o = _rms(o, on)
                r = r_ref[bi, rs, vs].astype(F32)
                o_ref[bi, rs, vs] = (o * (r * jax.nn.sigmoid(r))).astype(o_ref.dtype)
            st_ref[bi, hd] = st


def _gla(z, wg, bg, on, bb):
    batch, lp, _ = z.shape

    def col(width, start):
        return pl.BlockSpec((bb, BLOCK, width), lambda b, n: (b, n, start // width))

    const = lambda shape: pl.BlockSpec(shape, lambda b, n: (0,) * len(shape))
    return pl.pallas_call(
        _gla_kernel,
        out_shape=jax.ShapeDtypeStruct((batch, lp, GLA_V), BF16),
        grid=(batch // bb, lp // BLOCK),
        in_specs=[col(GLA_QK, Z_Q), col(GLA_QK, Z_K), col(GLA_V, Z_V), col(GLA_V, Z_R),
                  col(LANES, Z_LR), const((LANES, GLA_QK)), const((1, GLA_QK)),
                  const((1, GLA_DV))],
        out_specs=pl.BlockSpec((bb, BLOCK, GLA_V), lambda b, n: (b, n, 0)),
        scratch_shapes=[pltpu.VMEM((bb, GLA_HEADS, GLA_DV, GLA_DK), F32)],
        compiler_params=_params("parallel", "arbitrary"),
        name="gla",
    )(z, z, z, z, z, wg, bg, on)


def _group_mean_sq(x, bd):
    sq = x * x
    hi = sq.astype(BF16)
    lo = (sq - hi.astype(F32)).astype(BF16)
    return _dot(hi, bd) + _dot(lo, bd)


def _norm_rope(x, gain, cos, sin_a, sin_b, bd):
    nbat, rows, width = x.shape
    x2 = x.reshape(nbat * rows, width)
    parts = []
    for s in range(width // LANES):
        xs = x2[:, s * LANES:(s + 1) * LANES]
        ms = _group_mean_sq(xs, bd) * (1.0 / SWA_HEAD_DIM)
        parts.append(xs * lax.rsqrt(ms + EPS))
    xn = (parts[0] if len(parts) == 1 else jnp.concatenate(parts, axis=-1)) * gain
    half = SWA_HEAD_DIM // 2
    back = lambda t: t.reshape(nbat, rows, width)
    return (back(xn) * cos + back(pltpu.roll(xn, half, 1)) * sin_a
            + back(pltpu.roll(xn, width - half, 1)) * sin_b)


def _swa_kernel(sinks_ref, q_ref, kc_ref, kp_ref, vc_ref, vp_ref, cosc_ref, sac_ref, sbc_ref,
                cosp_ref, sap_ref, sbp_ref, qn_ref, kn_ref, bd_ref, f1_ref, f3_ref, f2_ref,
                o_ref, f1b_ref, f3b_ref, f2b_ref, i1_sc, i3_sc, i2_sc, o1_sc, o3_sc, o2_sc,
                in_sem, out_sem, *, cast_steps):
    n = pl.program_id(1)
    nbat = q_ref.shape[0]

    cast_step = pl.program_id(0) * pl.num_programs(1) + n

    @pl.when(cast_step < cast_steps)
    def _():
        _cast_stream(cast_step, cast_steps, (f1_ref, f3_ref, f2_ref),
                     (f1b_ref, f3b_ref, f2b_ref), (i1_sc, i3_sc, i2_sc), (o1_sc, o3_sc, o2_sc),
                     in_sem, out_sem)

    bd = bd_ref[...]
    rep = SWA_Q // LANES
    tile = lambda t: jnp.concatenate([t] * rep, axis=-1)
    cos_c, sa_c, sb_c = cosc_ref[...], sac_ref[...], sbc_ref[...]
    q = _norm_rope(q_ref[...].astype(F32), qn_ref[...], tile(cos_c), tile(sa_c), tile(sb_c), bd)
    q = (q * (SWA_HEAD_DIM ** -0.5 * LOG2E)).astype(BF16)
    k_c = _norm_rope(kc_ref[...].astype(F32), kn_ref[...], cos_c, sa_c, sb_c, bd).astype(BF16)
    k_p = _norm_rope(kp_ref[...].astype(F32), kn_ref[...], cosp_ref[...], sap_ref[...],
                     sbp_ref[...], bd).astype(BF16)

    gq = SWA_GROUP * BLOCK
    r = lax.broadcasted_iota(jnp.int32, (gq, 2 * BLOCK), 0) & (BLOCK - 1)
    cidx = lax.broadcasted_iota(jnp.int32, (gq, 2 * BLOCK), 1)
    allowed = (cidx > r) & (cidx <= r + SWA_WINDOW) & ((n - 1) * BLOCK + cidx >= FRONT)
    rgroup = lax.broadcasted_iota(jnp.int32, (gq, 1), 0) // BLOCK
    items = [(bi, hk) for bi in range(nbat) for hk in range(SWA_KV_HEADS)]

    s, sink = {}, {}
    for hk in range(SWA_KV_HEADS):
        sk = jnp.zeros((gq, 1), F32)
        for g in range(SWA_GROUP):
            sk = jnp.where(rgroup == g, sinks_ref[hk * SWA_GROUP + g] * LOG2E, sk)
        sink[hk] = sk
    for bi, hk in items:
        ks = slice(hk * SWA_HEAD_DIM, (hk + 1) * SWA_HEAD_DIM)
        qg = jnp.concatenate(
            [q[bi][:, (hk * SWA_GROUP + g) * SWA_HEAD_DIM:(hk * SWA_GROUP + g + 1) * SWA_HEAD_DIM]
             for g in range(SWA_GROUP)], axis=0)
        k = jnp.concatenate([k_p[bi][:, ks], k_c[bi][:, ks]], axis=0)
        s[bi, hk] = jnp.where(allowed, _dot_nt(qg, k), NEG_INF)

    p, denom = {}, {}
    for bi, hk in items:
        m = jnp.maximum(jnp.max(s[bi, hk], axis=-1, keepdims=True), sink[hk])
        e = jnp.exp2(s[bi, hk] - m)
        denom[bi, hk] = jnp.sum(e, axis=-1, keepdims=True) + jnp.exp2(sink[hk] - m)
        p[bi, hk] = e.astype(BF16)

    for bi, hk in items:
        ks = slice(hk * SWA_HEAD_DIM, (hk + 1) * SWA_HEAD_DIM)
        v = jnp.concatenate([vp_ref[bi, :, ks], vc_ref[bi, :, ks]], axis=0)
        o = _dot(p[bi, hk], v) / denom[bi, hk]
        for g in range(SWA_GROUP):
            hq = hk * SWA_GROUP + g
            o_ref[bi, :, hq * SWA_HEAD_DIM:(hq + 1) * SWA_HEAD_DIM] = (
                o[g * BLOCK:(g + 1) * BLOCK].astype(o_ref.dtype))


def _swa(z, sinks, cos, sin_a, sin_b, qn, kn, bd, ffn, bb):
    batch, lp, _ = z.shape
    steps = (batch // bb) * (lp // BLOCK)
    cast_steps = 1 << (steps.bit_length() - 1)
    assert all(w.shape[0] % (cast_steps * GROUP) == 0 for w in ffn)
    chunk = lambda w: (2, w.shape[0] // cast_steps, w.shape[1])
    any_spec = pl.BlockSpec(memory_space=pl.ANY)

    def cur(width, start):
        return pl.BlockSpec((bb, BLOCK, width), lambda b, n, s: (b, n, start // width))

    def prev(width, start):
        return pl.BlockSpec((bb, BLOCK, width),
                            lambda b, n, s: (b, jnp.maximum(n - 1, 0), start // width))

    tab_c = pl.BlockSpec((BLOCK, LANES), lambda b, n, s: (n, 0))
    tab_p = pl.BlockSpec((BLOCK, LANES), lambda b, n, s: (jnp.maximum(n - 1, 0), 0))
    const = lambda shape: pl.BlockSpec(shape, lambda b, n, s: (0,) * len(shape))
    grid_spec = pltpu.PrefetchScalarGridSpec(
        num_scalar_prefetch=1,
        grid=(batch // bb, lp // BLOCK),
        in_specs=[cur(SWA_Q, Z_SQ), cur(SWA_KV, Z_SK), prev(SWA_KV, Z_SK),
                  cur(SWA_KV, Z_SV), prev(SWA_KV, Z_SV),
                  tab_c, tab_c, tab_c, tab_p, tab_p, tab_p,
                  const((1, SWA_Q)), const((1, SWA_KV)), const((LANES, LANES)),
                  any_spec, any_spec, any_spec],
        out_specs=(pl.BlockSpec((bb, BLOCK, SWA_Q), lambda b, n, s: (b, n, 0)),
                   any_spec, any_spec, any_spec),
        scratch_shapes=[pltpu.VMEM(chunk(w), F32) for w in ffn]
        + [pltpu.VMEM(chunk(w), BF16) for w in ffn]
        + [pltpu.SemaphoreType.DMA((2,)), pltpu.SemaphoreType.DMA((2,))],
    )
    return pl.pallas_call(
        functools.partial(_swa_kernel, cast_steps=cast_steps),
        out_shape=(jax.ShapeDtypeStruct((batch, lp, SWA_Q), BF16),)
        + tuple(jax.ShapeDtypeStruct(w.shape, BF16) for w in ffn),
        grid_spec=grid_spec,
        compiler_params=_params("arbitrary", "arbitrary"),
        name="swa",
    )(sinks, z, z, z, z, z, cos, sin_a, sin_b, cos, sin_a, sin_b, qn, kn, bd, *ffn)


def _ffn_kernel(h_ref, oa_ref, ob_ref, woa_ref, wob_ref, g_ref, w1_ref, w3_ref, w2_ref,
                out_ref, h1_sc, xn_sc, acc_sc):
    j = pl.program_id(1)

    nj = pl.num_programs(1)

    @pl.when(j == 0)
    def _():
        h1 = h_ref[...] + _dot(oa_ref[...], woa_ref[...]) + _dot(ob_ref[...], wob_ref[...])
        h1_sc[...] = h1
        xn_sc[...] = _rms(h1, g_ref[...]).astype(BF16)

    x = xn_sc[...]
    a = _dot(x, w1_ref[...])
    b = _dot(x, w3_ref[...])
    hid = (a * jax.nn.sigmoid(a) * b).astype(BF16)
    part = _dot(hid, w2_ref[...])

    @pl.when(j == 0)
    def _():
        acc_sc[...] = h1_sc[...] + part

    @pl.when((j > 0) & (j < nj - 1))
    def _():
        acc_sc[...] += part

    @pl.when(j == nj - 1)
    def _():
        out_ref[...] = acc_sc[...] + part


def _mix_out_ffn(h, oa, ob, woa, wob, g, w1, w3, w2, tm, tf):
    rows = h.shape[0]
    dff = w1.shape[1]
    return pl.pallas_call(
        _ffn_kernel,
        out_shape=jax.ShapeDtypeStruct((rows, D_MODEL), F32),
        grid=(rows // tm, dff // tf),
        in_specs=[pl.BlockSpec((tm, D_MODEL), lambda i, j: (i, 0)),
                  pl.BlockSpec((tm, GLA_V), lambda i, j: (i, 0)),
                  pl.BlockSpec((tm, SWA_Q), lambda i, j: (i, 0)),
                  pl.BlockSpec((GLA_V, D_MODEL), lambda i, j: (0, 0)),
                  pl.BlockSpec((SWA_Q, D_MODEL), lambda i, j: (0, 0)),
                  pl.BlockSpec((1, D_MODEL), lambda i, j: (0, 0)),
                  pl.BlockSpec((D_MODEL, tf), lambda i, j: (0, j)),
                  pl.BlockSpec((D_MODEL, tf), lambda i, j: (0, j)),
                  pl.BlockSpec((tf, D_MODEL), lambda i, j: (j, 0))],
        out_specs=pl.BlockSpec((tm, D_MODEL), lambda i, j: (i, 0)),
        scratch_shapes=[pltpu.VMEM((tm, D_MODEL), F32), pltpu.VMEM((tm, D_MODEL), BF16),
                        pltpu.VMEM((tm, D_MODEL), F32)],
        compiler_params=_params("parallel", "arbitrary"),
        name="mix_out_ffn",
    )(h, oa, ob, woa, wob, g, w1, w3, w2)


CONV_HALO = 32
CONV_SPAN = CONV_HALO + BLOCK
CONV_ROWS = 32
SUBLANES = 8


def _cast_stream(step, n_steps, srcs, dsts, in_bufs, out_bufs, in_sem, out_sem):
    slot = lax.rem(step, 2)

    def copies(s, sl, fetch):
        out = []
        for src, dst, ibuf, obuf in zip(srcs, dsts, in_bufs, out_bufs):
            rows = ibuf.shape[1]
            at = pl.ds(pl.multiple_of(s * rows, rows), rows)
            out.append(pltpu.make_async_copy(src.at[at], ibuf.at[sl], in_sem.at[sl]) if fetch
                       else pltpu.make_async_copy(obuf.at[sl], dst.at[at], out_sem.at[sl]))
        return out

    @pl.when(step == 0)
    def _():
        for cp in copies(step, slot, True):
            cp.start()

    for cp in copies(step, slot, True):
        cp.wait()

    @pl.when(step + 1 < n_steps)
    def _():
        for cp in copies(step + 1, 1 - slot, True):
            cp.start()

    @pl.when(step >= 2)
    def _():
        for cp in copies(step - 2, slot, False):
            cp.wait()

    for ibuf, obuf in zip(in_bufs, out_bufs):
        obuf[slot] = ibuf[slot].astype(BF16)
    for cp in copies(step, slot, False):
        cp.start()

    @pl.when(step == n_steps - 1)
    def _():
        @pl.when(step >= 1)
        def _():
            for cp in copies(step - 1, 1 - slot, False):
                cp.wait()

        for cp in copies(step, slot, False):
            cp.wait()


def _conv_kernel(h_ref, g_ref, wa_ref, wg_ref, shift_ref, wdw_ref, bdw_ref, lg_ref, lb_ref,
                 w2_ref, e1_ref, e3_ref, e2_ref, out_ref, e1b_ref, e3b_ref, e2b_ref,
                 u_sc, xs_sc, c_sc, y_sc, i1_sc, i3_sc, i2_sc, o1_sc, o3_sc, o2_sc, in_sem,
                 out_sem):
    n = pl.program_id(1)
    real_blocks = pl.num_programs(1) - 1
    cast_step = pl.program_id(0) * real_blocks + n - 1
    cast_steps = pl.num_programs(0) * real_blocks

    @pl.when(n > 0)
    def _():
        _cast_stream(cast_step, cast_steps, (e1_ref, e3_ref, e2_ref),
                     (e1b_ref, e3b_ref, e2b_ref), (i1_sc, i3_sc, i2_sc), (o1_sc, o3_sc, o2_sc),
                     in_sem, out_sem)

    h = h_ref[...]
    xn = _rms(h, g_ref[...]).astype(BF16)
    u = _dot(xn, wa_ref[...]) * jax.nn.sigmoid(_dot(xn, wg_ref[...]))
    u_sc[CONV_HALO:CONV_SPAN, :] = u.astype(BF16)

    @pl.when(n > 0)
    def _():
        span = u_sc[...]
        xs_sc[0] = span.astype(F32)
        shifted = _dot(shift_ref[...], span)
        for r in range(1, SUBLANES):
            xs_sc[r] = shifted[(r - 1) * CONV_SPAN:r * CONV_SPAN]
        first = CONV_HALO - (CONV_WIDTH - 1)
        for cb in range(D_MODEL // LANES):
            cs = slice(cb * LANES, (cb + 1) * LANES)
            acc = jnp.zeros((BLOCK // SUBLANES, SUBLANES, LANES), F32)
            for j in range(CONV_WIDTH):
                r = (first + j) % SUBLANES
                base = first + j - r
                x = xs_sc[r, base:base + BLOCK, cs].reshape(BLOCK // SUBLANES, SUBLANES, LANES)
                acc = acc + x * wdw_ref[j, :, cs]
            c_sc[:, cs] = acc.reshape(BLOCK, LANES)
        for rb in range(BLOCK // CONV_ROWS):
            acc = c_sc[rb * CONV_ROWS:(rb + 1) * CONV_ROWS, :] + bdw_ref[...]
            mu = jnp.mean(acc, axis=-1, keepdims=True)
            dev = acc - mu
            var = jnp.mean(dev * dev, axis=-1, keepdims=True)
            y = dev * lax.rsqrt(var + EPS) * lg_ref[...] + lb_ref[...]
            y_sc[rb * CONV_ROWS:(rb + 1) * CONV_ROWS, :] = (y * jax.nn.sigmoid(y)).astype(BF16)
        out_ref[...] = h + _dot(y_sc[...], w2_ref[...])

    u_sc[0:CONV_HALO, :] = u_sc[BLOCK:CONV_SPAN, :]


def _conv_module(h, g, w_pw1, shift, wdw, bdw, lg, lb, w2, experts, batch, nb):
    const = lambda shape: pl.BlockSpec(shape, lambda b, n: (0,) * len(shape))
    any_spec = pl.BlockSpec(memory_space=pl.ANY)
    n_chunks = batch * (nb - 1)
    chunk = lambda w: (2, w.shape[0] // n_chunks, w.shape[1])
    assert all(w.shape[0] % (n_chunks * GROUP) == 0 for w in experts)
    return pl.pallas_call(
        _conv_kernel,
        out_shape=(jax.ShapeDtypeStruct((batch * (nb - 1) * BLOCK, D_MODEL), F32),)
        + tuple(jax.ShapeDtypeStruct(w.shape, BF16) for w in experts),
        grid=(batch, nb),
        in_specs=[pl.BlockSpec((BLOCK, D_MODEL), lambda b, n: (b * nb + n, 0)),
                  const((1, D_MODEL)),
                  pl.BlockSpec((D_MODEL, D_MODEL), lambda b, n: (0, 0)),
                  pl.BlockSpec((D_MODEL, D_MODEL), lambda b, n: (0, 1)),
                  const(((SUBLANES - 1) * CONV_SPAN, CONV_SPAN)),
                  const((CONV_WIDTH, SUBLANES, D_MODEL)), const((1, D_MODEL)), const((1, D_MODEL)),
                  const((1, D_MODEL)), const((D_MODEL, D_MODEL)),
                  any_spec, any_spec, any_spec],
        out_specs=(pl.BlockSpec((BLOCK, D_MODEL),
                                lambda b, n: (b * (nb - 1) + jnp.maximum(n - 1, 0), 0)),
                   any_spec, any_spec, any_spec),
        scratch_shapes=[pltpu.VMEM((CONV_SPAN, D_MODEL), BF16),
                        pltpu.VMEM((SUBLANES, CONV_SPAN, D_MODEL), F32),
                        pltpu.VMEM((BLOCK, D_MODEL), F32),
                        pltpu.VMEM((BLOCK, D_MODEL), BF16)]
        + [pltpu.VMEM(chunk(w), F32) for w in experts]
        + [pltpu.VMEM(chunk(w), BF16) for w in experts]
        + [pltpu.SemaphoreType.DMA((2,)), pltpu.SemaphoreType.DMA((2,))],
        compiler_params=_params("arbitrary", "arbitrary"),
        name="conv_module",
    )(h, g, w_pw1, w_pw1, shift, wdw, bdw, lg, lb, w2, *experts)


SORT_TOKENS = 512
GROUP = 16
SORT_CAP = 2 * SORT_TOKENS + N_EXPERTS * GROUP
SORT_COLS = D_MODEL + LANES
MOE_ROWS = 512
GROUPS_PER_TILE = MOE_ROWS // GROUP


def _one_hot_rows(pos1, pos2, width):
    r = lax.broadcasted_iota(jnp.int32, (pos1.shape[0], width), 1).astype(F32)
    return jnp.where(r == pos1, 1.0, jnp.where(r == pos2, 1.0, 0.0)).astype(BF16)


SORT_TILES_PER_STEP = 2
SPARE_ZERO = 0
SPARE_SINK = 64
assert SPARE_SINK + 2 * MOE_ROWS <= SORT_CAP


def _route_sort_kernel(h_ref, g_ref, wr_hi_ref, wr_lo_ref, xs_ref, pos_ref, ng_ref):
    is_spare = pl.program_id(0) == pl.num_programs(0) - 1

    @pl.when(is_spare)
    def _():
        xs_ref[...] = jnp.zeros_like(xs_ref)

    @pl.when(jnp.logical_not(is_spare))
    def _():
        tiles = [_route_sort_tile(h_ref.at[pl.ds(t * SORT_TOKENS, SORT_TOKENS)], g_ref,
                                  wr_hi_ref, wr_lo_ref,
                                  xs_ref.at[pl.ds(t * SORT_CAP, SORT_CAP)],
                                  pos_ref.at[pl.ds(t * SORT_TOKENS, SORT_TOKENS)], ng_ref.at[t])
                 for t in range(h_ref.shape[0] // SORT_TOKENS)]
        while tiles:
            tiles = [t for t in tiles if next(t, "done") != "done"]


def _route_sort_tile(h_ref, g_ref, wr_hi_ref, wr_lo_ref, xs_ref, pos_ref, ng_ref):
    s = SORT_TOKENS
    xn = _rms(h_ref[...], g_ref[...])
    x_hi = xn.astype(BF16)
    x_lo = (xn - x_hi.astype(F32)).astype(BF16)
    logits = (_dot(x_hi, wr_hi_ref[...]) + _dot(x_lo, wr_hi_ref[...])
              + _dot(x_hi, wr_lo_ref[...]))
    yield
    lane = lax.broadcasted_iota(jnp.int32, logits.shape, 1)
    logits = jnp.where(lane < N_EXPERTS, logits, -jnp.inf)
    m1 = jnp.max(logits, axis=-1, keepdims=True)
    i1 = jnp.min(jnp.where(logits == m1, lane, LANES), axis=-1, keepdims=True)
    rest = jnp.where(lane == i1, -jnp.inf, logits)
    m2 = jnp.max(rest, axis=-1, keepdims=True)
    i2 = jnp.min(jnp.where(rest == m2, lane, LANES), axis=-1, keepdims=True)
    e2 = jnp.exp(m2 - m1)
    g1 = 1.0 / (1.0 + e2)
    g2 = e2 / (1.0 + e2)
    yield

    oh1 = lane == i1
    oh2 = lane == i2
    oh = jnp.where(oh1, 1.0, jnp.where(oh2, 1.0, 0.0))
    cnt = jnp.sum(oh, axis=0, keepdims=True)
    ngr = jnp.floor((cnt + (GROUP - 1)) * (1.0 / GROUP))
    ri = lax.broadcasted_iota(jnp.int32, (s, s), 0)
    ci = lax.broadcasted_iota(jnp.int32, (s, s), 1)
    rank = _dot(jnp.where(ri > ci, 1.0, 0.0).astype(BF16), oh.astype(BF16))
    li = lax.broadcasted_iota(jnp.int32, (LANES, LANES), 0)
    lj = lax.broadcasted_iota(jnp.int32, (LANES, LANES), 1)
    before = _dot(jnp.broadcast_to(ngr, (8, LANES)).astype(BF16),
                  jnp.where(li < lj, 1.0, 0.0).astype(BF16))[0:1]
    yield
    base = before * GROUP + rank
    base1 = jnp.where(oh1, base, 0.0)
    base2 = jnp.where(oh2, base, 0.0)
    pos1 = jnp.sum(base1, axis=-1, keepdims=True)
    pos2 = jnp.sum(base2, axis=-1, keepdims=True)

    def as_row(v):
        ones = jnp.ones((8, LANES), BF16)
        hi = jnp.floor(v * (1.0 / 256.0))
        return (_dot_nt(ones, hi.astype(BF16)) * 256.0
                + _dot_nt(ones, (v - hi * 256.0).astype(BF16)))[0:1]

    row1, row2 = as_row(base1), as_row(base2)
    yield
    r = lax.broadcasted_iota(jnp.int32, (SORT_CAP, s), 0).astype(F32)
    p1 = jnp.where(r == row1, 1.0, 0.0).astype(BF16)
    p2 = jnp.where(r == row2, 1.0, 0.0).astype(BF16)
    xs_ref[:, 0:D_MODEL] = _dot(p1 + p2, x_hi).astype(BF16)

    def gate_lanes(gv):
        hi = gv.astype(BF16).astype(F32)
        return jnp.where(lane == 0, hi, jnp.where(lane == 1, gv - hi, 0.0)).astype(BF16)

    xs_ref[:, D_MODEL:SORT_COLS] = (_dot(p1, gate_lanes(g1))
                                    + _dot(p2, gate_lanes(g2))).astype(BF16)
    pos_ref[...] = jnp.where(lane == 0, pos1, jnp.where(lane == 1, pos2, 0.0))
    ng_ref[...] = jnp.broadcast_to(ngr, (8, LANES)).astype(jnp.int32)


def _route_sort(h, g, wr_hi, wr_lo):
    tokens = h.shape[0]
    nt = tokens // SORT_TOKENS
    tps = math.gcd(nt, SORT_TILES_PER_STEP)
    steps = nt // tps
    last = lambda i: jnp.minimum(i, steps - 1)
    return pl.pallas_call(
        _route_sort_kernel,
        out_shape=(jax.ShapeDtypeStruct(((nt + tps) * SORT_CAP, SORT_COLS), BF16),
                   jax.ShapeDtypeStruct((tokens, LANES), F32),
                   jax.ShapeDtypeStruct((nt, 8, LANES), jnp.int32)),
        grid=(steps + 1,),
        in_specs=[pl.BlockSpec((tps * SORT_TOKENS, D_MODEL), lambda i: (last(i), 0)),
                  pl.BlockSpec((1, D_MODEL), lambda i: (0, 0)),
                  pl.BlockSpec((D_MODEL, LANES), lambda i: (0, 0)),
                  pl.BlockSpec((D_MODEL, LANES), lambda i: (0, 0))],
        out_specs=(pl.BlockSpec((tps * SORT_CAP, SORT_COLS), lambda i: (i, 0)),
                   pl.BlockSpec((tps * SORT_TOKENS, LANES), lambda i: (last(i), 0)),
                   pl.BlockSpec((tps, 8, LANES), lambda i: (last(i), 0, 0))),
        compiler_params=_params("arbitrary"),
        name="route_sort",
    )(h, g, wr_hi, wr_lo)


def _group_tables(ng, n_row_tiles):
    nt, ne = ng.shape
    gpt = GROUPS_PER_TILE
    n_e = jnp.sum(ng, axis=0)
    rt_e = (n_e + gpt - 1) // gpt
    rt_end = jnp.cumsum(rt_e)
    n_rt = rt_end[-1]
    r = jnp.arange(n_row_tiles, dtype=jnp.int32)
    valid_r = r < n_rt
    e_r = jnp.minimum(jnp.sum((r[:, None] >= rt_end[None, :]).astype(jnp.int32), axis=1), ne - 1)
    e_last = jnp.sum(jnp.where(r == n_rt - 1, e_r, 0))
    e_r = jnp.where(valid_r, e_r, e_last)
    sel = (e_r[:, None] == jnp.arange(ne, dtype=jnp.int32)[None, :]).astype(jnp.int32)
    pick = lambda v: jnp.sum(sel * v[None, :], axis=1)
    k0 = (r - pick(rt_end - rt_e)) * gpt
    gidx = k0[:, None] + jnp.arange(gpt, dtype=jnp.int32)[None, :]
    valid = valid_r[:, None] & (gidx < pick(n_e)[:, None])
    incl = jnp.cumsum(ng, axis=0)
    incl_r = jnp.sum(sel[:, None, :] * incl[None, :, :], axis=2)
    ng_r = jnp.sum(sel[:, None, :] * ng[None, :, :], axis=2)
    lo = jnp.cumsum(ng, axis=1) - ng
    lo_r = jnp.sum(sel[:, None, :] * lo[None, :, :], axis=2)
    ti = jnp.sum((incl_r[:, None, :] <= gidx[:, :, None]).astype(jnp.int32), axis=2)
    ti = jnp.minimum(ti, nt - 1)
    tsel = (ti[:, :, None] == jnp.arange(nt, dtype=jnp.int32)[None, None, :]).astype(jnp.int32)
    tpick = lambda v: jnp.sum(tsel * v[:, None, :], axis=2)
    within = gidx - tpick(incl_r - ng_r)
    rows = ti * SORT_CAP + GROUP * (tpick(lo_r) + within)
    spare = nt * SORT_CAP
    sink = (spare + SPARE_SINK + (r % 2)[:, None] * MOE_ROWS
            + GROUP * jnp.arange(gpt, dtype=jnp.int32)[None, :])
    src = jnp.where(valid, rows, spare + SPARE_ZERO).reshape(-1)
    dst = jnp.where(valid, rows, sink).reshape(-1)
    return e_r, valid_r.astype(jnp.int32), src, dst


def _group_copies(rows_ref, tile, hbm, buf, slot, sem, to_hbm, start):
    for k in range(GROUPS_PER_TILE):
        off = pl.multiple_of(rows_ref[tile * GROUPS_PER_TILE + k], GROUP)
        buf_rows = buf.at[slot, pl.ds(k * GROUP, GROUP), :]
        hbm_rows = hbm.at[pl.ds(off, GROUP), pl.ds(0, buf_rows.shape[-1])]
        cp = (pltpu.make_async_copy(buf_rows, hbm_rows, sem.at[slot]) if to_hbm
              else pltpu.make_async_copy(hbm_rows, buf_rows, sem.at[slot]))
        if start:
            cp.start()
        else:
            cp.wait()


def _moe_group_kernel(te_ref, tv_ref, src_ref, dst_ref, xs_hbm, w1_ref, w3_ref, w2_ref, ys_hbm,
                      xbuf, ybuf, acc_sc, xsem, ysem):
    r = pl.program_id(0)
    j = pl.program_id(1)
    nr = pl.num_programs(0)
    nj = pl.num_programs(1)
    slot = lax.rem(r, 2)
    valid = lambda t: tv_ref[jnp.clip(t, 0, nr - 1)] > 0
    gather = functools.partial(_group_copies, src_ref, hbm=xs_hbm, buf=xbuf, sem=xsem,
                               to_hbm=False)
    scatter = functools.partial(_group_copies, dst_ref, hbm=ys_hbm, buf=ybuf, sem=ysem,
                                to_hbm=True)

    @pl.when(j == 0)
    def _():
        @pl.when((r == 0) & valid(r))
        def _():
            gather(tile=r, slot=slot, start=True)

        @pl.when(valid(r))
        def _():
            gather(tile=r, slot=slot, start=False)

        @pl.when((r + 1 < nr) & valid(r + 1))
        def _():
            gather(tile=r + 1, slot=1 - slot, start=True)

        @pl.when((r >= 2) & valid(r - 2))
        def _():
            scatter(tile=r - 2, slot=slot, start=False)

    @pl.when(valid(r))
    def _():
        x = xbuf[slot, :, 0:D_MODEL]
        a = _dot(x, w1_ref[0])
        b = _dot(x, w3_ref[0])
        hid = (a * jax.nn.sigmoid(a) * b).astype(BF16)
        part = _dot(hid, w2_ref[0])

        @pl.when(j == 0)
        def _():
            acc_sc[...] = part

        @pl.when((j > 0) & (j < nj - 1))
        def _():
            acc_sc[...] += part

        @pl.when(j == nj - 1)
        def _():
            gl = xbuf[slot, :, D_MODEL:SORT_COLS].astype(F32)
            gate = gl[:, 0:1] + gl[:, 1:2]
            ybuf[slot] = ((acc_sc[...] + part) * gate).astype(BF16)
            scatter(tile=r, slot=slot, start=True)

    @pl.when((r == nr - 1) & (j == nj - 1))
    def _():
        @pl.when((r >= 1) & valid(r - 1))
        def _():
            scatter(tile=r - 1, slot=1 - slot, start=False)

        @pl.when(valid(r))
        def _():
            scatter(tile=r, slot=slot, start=False)


def _moe_group(xs, tile_expert, tile_valid, src, dst, w1, w3, w2, n_row_tiles, tf):
    ne, _, dff = w1.shape
    nj = dff // tf
    assert nj >= 2

    def w_in(r, j, te, tv, s, d):
        return (te[r], 0, jnp.where(tv[r] > 0, j, nj - 1))

    def w_out(r, j, te, tv, s, d):
        return (te[r], jnp.where(tv[r] > 0, j, nj - 1), 0)

    grid_spec = pltpu.PrefetchScalarGridSpec(
        num_scalar_prefetch=4,
        grid=(n_row_tiles, nj),
        in_specs=[pl.BlockSpec(memory_space=pl.ANY),
                  pl.BlockSpec((1, D_MODEL, tf), w_in),
                  pl.BlockSpec((1, D_MODEL, tf), w_in),
                  pl.BlockSpec((1, tf, D_MODEL), w_out)],
        out_specs=pl.BlockSpec(memory_space=pl.ANY),
        scratch_shapes=[pltpu.VMEM((2, MOE_ROWS, SORT_COLS), BF16),
                        pltpu.VMEM((2, MOE_ROWS, D_MODEL), BF16),
                        pltpu.VMEM((MOE_ROWS, D_MODEL), F32),
                        pltpu.SemaphoreType.DMA((2,)),
                        pltpu.SemaphoreType.DMA((2,))],
    )
    return pl.pallas_call(
        _moe_group_kernel,
        out_shape=jax.ShapeDtypeStruct(xs.shape, xs.dtype),
        grid_spec=grid_spec,
        input_output_aliases={4: 0},
        compiler_params=_params("arbitrary", "arbitrary"),
        name="moe_group",
    )(tile_expert, tile_valid, src, dst, xs, w1, w3, w2)


def _combine_kernel(h_ref, pos_ref, ys_ref, out_ref):
    pos = pos_ref[...]
    pt = _one_hot_rows(pos[:, 0:1], pos[:, 1:2], SORT_CAP)
    out_ref[...] = h_ref[...] + _dot(pt, ys_ref[...])


def _combine(h, pos, ys):
    tokens = h.shape[0]
    return pl.pallas_call(
        _combine_kernel,
        out_shape=jax.ShapeDtypeStruct((tokens, D_MODEL), F32),
        grid=(tokens // SORT_TOKENS,),
        in_specs=[pl.BlockSpec((SORT_TOKENS, D_MODEL), lambda i: (i, 0)),
                  pl.BlockSpec((SORT_TOKENS, LANES), lambda i: (i, 0)),
                  pl.BlockSpec((SORT_CAP, D_MODEL), lambda i: (i, 0))],
        out_specs=pl.BlockSpec((SORT_TOKENS, D_MODEL), lambda i: (i, 0)),
        compiler_params=_params("parallel"),
        name="combine",
    )(h, pos, ys)


def _row_tile(rows, want):
    tm = want
    while rows % tm:
        tm //= 2
    return tm


def _rope_tables(lp):
    inv_freq = 1.0 / (ROPE_THETA ** (jnp.arange(0, SWA_HEAD_DIM, 2, dtype=F32) / SWA_HEAD_DIM))
    pos = jnp.arange(lp, dtype=F32) - FRONT
    ang = pos[:, None] * inv_freq[None, :]
    cos, sin = jnp.cos(ang), jnp.sin(ang)
    zero = jnp.zeros_like(sin)
    rep = LANES // SWA_HEAD_DIM
    cos_t = jnp.tile(jnp.concatenate([cos, cos], axis=-1), (1, rep))
    sin_a = jnp.tile(jnp.concatenate([zero, sin], axis=-1), (1, rep))
    sin_b = jnp.tile(jnp.concatenate([-sin, zero], axis=-1), (1, rep))
    return cos_t, sin_a, sin_b


def kernel(x, meta, a_norm, a_w_in, a_w_gate2, a_b_gate, a_q_norm, a_k_norm, a_sinks, a_o_norm,
           a_w_out, f_norm, f_w1, f_w3, f_w2, c_norm, c_w_pw1, c_w_dw, c_b_dw, c_ln_g, c_ln_b,
           c_w_pw2, m_norm, m_w_router, m_w1, m_w3, m_w2):
    batch, seq, _ = x.shape
    assert seq % BLOCK == 0
    lp = BLOCK + seq
    nb = lp // BLOCK
    rows = batch * lp

    head = jnp.concatenate([jnp.zeros((FRONT, D_MODEL), x.dtype), meta.astype(x.dtype)], axis=0)
    h = jnp.concatenate([jnp.broadcast_to(head[None], (batch, BLOCK, D_MODEL)), x], axis=1)
    h = h.reshape(rows, D_MODEL)

    w_in = a_w_in[0]
    gq, gk, gv, gr, glr, sq, sk, sv = jnp.split(
        w_in, [256, 512, 1024, 1536, 1552, 2064, 2192], axis=-1)
    w_in_r = jnp.concatenate(
        [gq, gk, gv, gr, sq, sk, sv, glr,
         jnp.zeros((D_MODEL, Z_COLS - Z_LR - GLA_LOWRANK), w_in.dtype)], axis=-1).astype(BF16)
    z = _in_proj(h, a_norm[0][None], w_in_r, _row_tile(rows, 512))

    wg = jnp.zeros((LANES, GLA_QK), F32).at[:GLA_LOWRANK].set(a_w_gate2[0]).astype(BF16)
    z = z.reshape(batch, lp, Z_COLS)
    bb = math.gcd(batch, MIX_BATCH)
    o_a = _gla(z, wg, a_b_gate[0][None], a_o_norm[0][None], bb).reshape(rows, GLA_V)

    cos_t, sin_a, sin_b = _rope_tables(lp)
    qn = jnp.tile(a_q_norm[0], SWA_Q_HEADS)[None]
    kn = jnp.tile(a_k_norm[0], SWA_KV_HEADS)[None]
    lane = jnp.arange(LANES)
    bd = (lane[:, None] // SWA_HEAD_DIM == lane[None, :] // SWA_HEAD_DIM).astype(BF16)
    o_b, d_w1, d_w3, d_w2 = _swa(z, a_sinks[0], cos_t, sin_a, sin_b, qn, kn, bd,
                                 (f_w1[0], f_w3[0], f_w2[0]), bb)
    o_b = o_b.reshape(rows, SWA_Q)

    w_out = a_w_out[0].astype(BF16)
    h = _mix_out_ffn(h, o_a, o_b, w_out[:GLA_V], w_out[GLA_V:], f_norm[0][None],
                     d_w1, d_w3, d_w2, _row_tile(rows, 512), FF_TILE)

    si = jnp.arange((SUBLANES - 1) * CONV_SPAN)
    shift = (jnp.arange(CONV_SPAN)[None, :]
             == (si % CONV_SPAN + si // CONV_SPAN + 1)[:, None]).astype(BF16)
    ne, _, dff = m_w1[0].shape
    experts = (m_w1[0].reshape(ne * D_MODEL, dff), m_w3[0].reshape(ne * D_MODEL, dff),
               m_w2[0].reshape(ne * dff, D_MODEL))
    h, e_w1, e_w3, e_w2 = _conv_module(
        h, c_norm[0][None], c_w_pw1[0].astype(BF16), shift,
        jnp.broadcast_to(c_w_dw[0][:, None, :], (CONV_WIDTH, SUBLANES, D_MODEL)),
        c_b_dw[0][None], c_ln_g[0][None], c_ln_b[0][None], c_w_pw2[0].astype(BF16), experts,
        batch, nb)

    wr = jnp.zeros((D_MODEL, LANES), F32).at[:, :N_EXPERTS].set(m_w_router[0])
    wr_hi = wr.astype(BF16)
    wr_lo = (wr - wr_hi.astype(F32)).astype(BF16)
    tokens = batch * seq
    assert tokens % SORT_TOKENS == 0
    nt = tokens // SORT_TOKENS
    xs, pos, ng = _route_sort(h, m_norm[0][None], wr_hi, wr_lo)
    ng = ng[:, 0, :N_EXPERTS]
    n_row_tiles = (nt * (SORT_CAP // GROUP)) // GROUPS_PER_TILE + N_EXPERTS
    tile_expert, tile_valid, src, dst = _group_tables(ng, n_row_tiles)
    ys = _moe_group(xs, tile_expert, tile_valid, src, dst, e_w1.reshape(ne, D_MODEL, dff),
                    e_w3.reshape(ne, D_MODEL, dff), e_w2.reshape(ne, dff, D_MODEL), n_row_tiles,
                    FF_TILE)
    out = _combine(h, pos, ys)
    return out.reshape(batch, seq, D_MODEL)
```

```python
import functools
import math

import jax
import jax.numpy as jnp
from jax import lax
from jax.experimental import pallas as pl
from jax.experimental.pallas import tpu as pltpu

F32 = jnp.float32
BF16 = jnp.bfloat16

D_MODEL = 1024
N_META = 16
EPS = 1e-6
ROPE_THETA = 10000.0
NEG_INF = -1e30
LOG2E = math.log2(math.e)
GLA_HEADS = 4
GLA_DK = 64
GLA_DV = 128
GLA_LOWRANK = 16
GLA_TAU = 16.0
GLA_CHUNK = 64
SWA_Q_HEADS = 8
SWA_KV_HEADS = 2
SWA_HEAD_DIM = 64
SWA_WINDOW = 128
CONV_WIDTH = 31
D_FF = 3584
N_EXPERTS = 8

GLA_QK = GLA_HEADS * GLA_DK
GLA_V = GLA_HEADS * GLA_DV
SWA_Q = SWA_Q_HEADS * SWA_HEAD_DIM
SWA_KV = SWA_KV_HEADS * SWA_HEAD_DIM
SWA_GROUP = SWA_Q_HEADS // SWA_KV_HEADS

LANES = 128
FF_TILE = 1792
BLOCK = 128
FRONT = BLOCK - N_META
VMEM_LIMIT = 56 * 1024 * 1024

Z_Q, Z_K, Z_V, Z_R = 0, 256, 512, 1024
Z_SQ, Z_SK, Z_SV, Z_LR = 1536, 2048, 2176, 2304
Z_COLS = 2432


def _rms(x, g):
    return x * lax.rsqrt(jnp.mean(x * x, axis=-1, keepdims=True) + EPS) * g


def _dot(a, b):
    return jnp.dot(a, b, preferred_element_type=F32)


def _dot_nt(a, b):
    return lax.dot_general(a, b, (((1,), (1,)), ((), ())), preferred_element_type=F32)


def _dot_tn(a, b):
    return lax.dot_general(a, b, (((0,), (0,)), ((), ())), preferred_element_type=F32)


def _stored_block(n, lp):
    return jnp.where(n == 0, lp // BLOCK - 1, n - 1)


def _params(*sem):
    return pltpu.CompilerParams(dimension_semantics=sem, vmem_limit_bytes=VMEM_LIMIT)


def _in_proj_kernel(h_ref, g_ref, w_ref, *rest):
    z_ref = rest[-1]
    xn = _rms(h_ref[...], g_ref[...]).astype(BF16)
    z_ref[...] = _dot(xn, w_ref[...]).astype(z_ref.dtype)


def _in_proj(h, g, w, batch, lp, tm, first_row, z=None):
    hb, rows, _ = h.shape
    n = w.shape[1]
    in_specs = [pl.BlockSpec((None, tm, D_MODEL), lambda b, j: (jnp.minimum(b, hb - 1), j, 0)),
                pl.BlockSpec((1, D_MODEL), lambda b, j: (0, 0)),
                pl.BlockSpec((D_MODEL, n), lambda b, j: (0, 0))]
    args = (h, g, w)
    if z is not None:
        in_specs.append(pl.BlockSpec(memory_space=pl.ANY))
        args += (z,)
    return pl.pallas_call(
        _in_proj_kernel,
        out_shape=jax.ShapeDtypeStruct((batch, lp, n), BF16),
        grid=(batch, rows // tm),
        in_specs=in_specs,
        out_specs=pl.BlockSpec((None, tm, n), lambda b, j: (b, first_row // tm + j, 0)),
        input_output_aliases={} if z is None else {3: 0},
        compiler_params=_params("parallel", "parallel"),
        name="in_proj",
    )(*args)


MIX_BATCH = 4


def _gla_kernel(q_ref, k_ref, v_ref, r_ref, lr_ref, wg_ref, bg_ref, on_ref, o_ref, st_ref):
    n = pl.program_id(1)

    @pl.when(n == 0)
    def _():
        st_ref[...] = jnp.zeros_like(st_ref)

    c = GLA_CHUNK
    items = [(bi, ch) for bi in range(q_ref.shape[0]) for ch in range(BLOCK // c)]
    ri = lax.broadcasted_iota(jnp.int32, (BLOCK, BLOCK), 0)
    ci = lax.broadcasted_iota(jnp.int32, (BLOCK, BLOCK), 1)
    tri = jnp.where((ri >= ci) & (ri // c == ci // c), 1.0, 0.0).astype(BF16)
    causal = (lax.broadcasted_iota(jnp.int32, (c, c), 0)
              >= lax.broadcasted_iota(jnp.int32, (c, c), 1))
    row = lax.broadcasted_iota(jnp.int32, (BLOCK, GLA_QK), 0) + n * BLOCK
    on = on_ref[...]

    b_all = []
    for bi in range(q_ref.shape[0]):
        gp = _dot(lr_ref[bi], wg_ref[...]) + bg_ref[...]
        g = (jnp.minimum(gp, 0.0) - jnp.log(1.0 + jnp.exp(-jnp.abs(gp)))) * (1.0 / GLA_TAU)
        g = jnp.where(row >= FRONT, g, 0.0)
        g_hi = g.astype(BF16)
        g_lo = (g - g_hi.astype(F32)).astype(BF16)
        b_all.append(_dot(tri, g_hi) + _dot(tri, g_lo))

    q_t, k_t, k_d, dec = {}, {}, {}, {}
    for bi, ch in items:
        rs = slice(ch * c, (ch + 1) * c)
        b = b_all[bi][rs]
        b_last = b_all[bi][(ch + 1) * c - 1:(ch + 1) * c, :]
        qf = q_ref[bi, rs, :].astype(F32)
        kf = k_ref[bi, rs, :].astype(F32)
        q_t[bi, ch] = (qf * jnp.exp(b) * (GLA_DK ** -0.5)).astype(BF16)
        k_t[bi, ch] = (kf * jnp.exp(-b)).astype(BF16)
        k_d[bi, ch] = (kf * jnp.exp(b_last - b)).astype(BF16)
        dec[bi, ch] = jnp.exp(b_last)

    att, kv = {}, {}
    for bi, ch in items:
        rs = slice(ch * c, (ch + 1) * c)
        for hd in range(GLA_HEADS):
            ks = slice(hd * GLA_DK, (hd + 1) * GLA_DK)
            vh = v_ref[bi, rs, hd * GLA_DV:(hd + 1) * GLA_DV]
            att[bi, ch, hd] = jnp.where(
                causal, _dot_nt(q_t[bi, ch][:, ks], k_t[bi, ch][:, ks]), 0.0).astype(BF16)
            kv[bi, ch, hd] = _dot_tn(vh, k_d[bi, ch][:, ks])

    for bi in range(q_ref.shape[0]):
        for hd in range(GLA_HEADS):
            ks = slice(hd * GLA_DK, (hd + 1) * GLA_DK)
            vs = slice(hd * GLA_DV, (hd + 1) * GLA_DV)
            st = st_ref[bi, hd]
            for ch in range(BLOCK // c):
                rs = slice(ch * c, (ch + 1) * c)
                o = (_dot(att[bi, ch, hd], v_ref[bi, rs, vs])
                     + _dot_nt(q_t[bi, ch][:, ks], st.astype(BF16)))
                st = st * dec[bi, ch][:, ks] + kv[bi, ch, hd]
                o = _rms(o, on)
                r = r_ref[bi, rs, vs].astype(F32)
                o_ref[bi, rs, vs] = (o * (r * jax.nn.sigmoid(r))).astype(o_ref.dtype)
            st_ref[bi, hd] = st


def _gla(z, wg, bg, on, bb):
    batch, lp, _ = z.shape

    def col(width, start):
        return pl.BlockSpec((bb, BLOCK, width), lambda b, n: (b, _stored_block(n, lp), start // width))

    const = lambda shape: pl.BlockSpec(shape, lambda b, n: (0,) * len(shape))
    return pl.pallas_call(
        _gla_kernel,
        out_shape=jax.ShapeDtypeStruct((batch, lp, GLA_V), BF16),
        grid=(batch // bb, lp // BLOCK),
        in_specs=[col(GLA_QK, Z_Q), col(GLA_QK, Z_K), col(GLA_V, Z_V), col(GLA_V, Z_R),
                  col(LANES, Z_LR), const((LANES, GLA_QK)), const((1, GLA_QK)),
                  const((1, GLA_DV))],
        out_specs=pl.BlockSpec((bb, BLOCK, GLA_V), lambda b, n: (b, _stored_block(n, lp), 0)),
        scratch_shapes=[pltpu.VMEM((bb, GLA_HEADS, GLA_DV, GLA_DK), F32)],
        compiler_params=_params("parallel", "arbitrary"),
        name="gla",
    )(z, z, z, z, z, wg, bg, on)


def _group_mean_sq(x, bd):
    sq = x * x
    hi = sq.astype(BF16)
    lo = (sq - hi.astype(F32)).astype(BF16)
    return _dot(hi, bd) + _dot(lo, bd)


def _norm_rope(x, gain, cos, sin_a, sin_b, bd):
    nbat, rows, width = x.shape
    x2 = x.reshape(nbat * rows, width)
    parts = []
    for s in range(width // LANES):
        xs = x2[:, s * LANES:(s + 1) * LANES]
        ms = _group_mean_sq(xs, bd) * (1.0 / SWA_HEAD_DIM)
        parts.append(xs * lax.rsqrt(ms + EPS))
    xn = (parts[0] if len(parts) == 1 else jnp.concatenate(parts, axis=-1)) * gain
    half = SWA_HEAD_DIM // 2
    back = lambda t: t.reshape(nbat, rows, width)
    return (back(xn) * cos + back(pltpu.roll(xn, half, 1)) * sin_a
            + back(pltpu.roll(xn, width - half, 1)) * sin_b)


def _swa_kernel(sinks_ref, q_ref, kc_ref, kp_ref, vc_ref, vp_ref, cosc_ref, sac_ref, sbc_ref,
                cosp_ref, sap_ref, sbp_ref, qn_ref, kn_ref, bd_ref, f1_ref, f3_ref, f2_ref,
                o_ref, f1b_ref, f3b_ref, f2b_ref, i1_sc, i3_sc, i2_sc, o1_sc, o3_sc, o2_sc,
                in_sem, out_sem, *, cast_steps):
    n = pl.program_id(1)
    nbat = q_ref.shape[0]

    cast_step = pl.program_id(0) * pl.num_programs(1) + n

    @pl.when(cast_step < cast_steps)
    def _():
        _cast_stream(cast_step, cast_steps, (f1_ref, f3_ref, f2_ref),
                     (f1b_ref, f3b_ref, f2b_ref), (i1_sc, i3_sc, i2_sc), (o1_sc, o3_sc, o2_sc),
                     in_sem, out_sem)

    bd = bd_ref[...]
    rep = SWA_Q // LANES
    tile = lambda t: jnp.concatenate([t] * rep, axis=-1)
    cos_c, sa_c, sb_c = cosc_ref[...], sac_ref[...], sbc_ref[...]
    q = _norm_rope(q_ref[...].astype(F32), qn_ref[...], tile(cos_c), tile(sa_c), tile(sb_c), bd)
    q = (q * (SWA_HEAD_DIM ** -0.5 * LOG2E)).astype(BF16)
    k_c = _norm_rope(kc_ref[...].astype(F32), kn_ref[...], cos_c, sa_c, sb_c, bd).astype(BF16)
    k_p = _norm_rope(kp_ref[...].astype(F32), kn_ref[...], cosp_ref[...], sap_ref[...],
                     sbp_ref[...], bd).astype(BF16)

    gq = SWA_GROUP * BLOCK
    r = lax.broadcasted_iota(jnp.int32, (gq, 2 * BLOCK), 0) & (BLOCK - 1)
    cidx = lax.broadcasted_iota(jnp.int32, (gq, 2 * BLOCK), 1)
    allowed = (cidx > r) & (cidx <= r + SWA_WINDOW) & ((n - 1) * BLOCK + cidx >= FRONT)
    rgroup = lax.broadcasted_iota(jnp.int32, (gq, 1), 0) // BLOCK
    items = [(bi, hk) for bi in range(nbat) for hk in range(SWA_KV_HEADS)]

    s, sink = {}, {}
    for hk in range(SWA_KV_HEADS):
        sk = jnp.zeros((gq, 1), F32)
        for g in range(SWA_GROUP):
            sk = jnp.where(rgroup == g, sinks_ref[hk * SWA_GROUP + g] * LOG2E, sk)
        sink[hk] = sk
    for bi, hk in items:
        ks = slice(hk * SWA_HEAD_DIM, (hk + 1) * SWA_HEAD_DIM)
        qg = jnp.concatenate(
            [q[bi][:, (hk * SWA_GROUP + g) * SWA_HEAD_DIM:(hk * SWA_GROUP + g + 1) * SWA_HEAD_DIM]
             for g in range(SWA_GROUP)], axis=0)
        k = jnp.concatenate([k_p[bi][:, ks], k_c[bi][:, ks]], axis=0)
        s[bi, hk] = jnp.where(allowed, _dot_nt(qg, k), NEG_INF)

    p, denom = {}, {}
    for bi, hk in items:
        m = jnp.maximum(jnp.max(s[bi, hk], axis=-1, keepdims=True), sink[hk])
        e = jnp.exp2(s[bi, hk] - m)
        denom[bi, hk] = jnp.sum(e, axis=-1, keepdims=True) + jnp.exp2(sink[hk] - m)
        p[bi, hk] = e.astype(BF16)

    for bi, hk in items:
        ks = slice(hk * SWA_HEAD_DIM, (hk + 1) * SWA_HEAD_DIM)
        v = jnp.concatenate([vp_ref[bi, :, ks], vc_ref[bi, :, ks]], axis=0)
        o = _dot(p[bi, hk], v) / denom[bi, hk]
        for g in range(SWA_GROUP):
            hq = hk * SWA_GROUP + g
            o_ref[bi, :, hq * SWA_HEAD_DIM:(hq + 1) * SWA_HEAD_DIM] = (
                o[g * BLOCK:(g + 1) * BLOCK].astype(o_ref.dtype))


def _swa(z, sinks, cos, sin_a, sin_b, qn, kn, bd, ffn, bb):
    batch, lp, _ = z.shape
    steps = (batch // bb) * (lp // BLOCK)
    cast_steps = 1 << (steps.bit_length() - 1)
    assert all(w.shape[0] % (cast_steps * GROUP) == 0 for w in ffn)
    chunk = lambda w: (2, w.shape[0] // cast_steps, w.shape[1])
    any_spec = pl.BlockSpec(memory_space=pl.ANY)

    def cur(width, start):
        return pl.BlockSpec((bb, BLOCK, width), lambda b, n, s: (b, _stored_block(n, lp), start // width))

    def prev(width, start):
        return pl.BlockSpec((bb, BLOCK, width),
                            lambda b, n, s: (b, _stored_block(jnp.maximum(n - 1, 0), lp),
                                             start // width))

    tab_c = pl.BlockSpec((BLOCK, LANES), lambda b, n, s: (n, 0))
    tab_p = pl.BlockSpec((BLOCK, LANES), lambda b, n, s: (jnp.maximum(n - 1, 0), 0))
    const = lambda shape: pl.BlockSpec(shape, lambda b, n, s: (0,) * len(shape))
    grid_spec = pltpu.PrefetchScalarGridSpec(
        num_scalar_prefetch=1,
        grid=(batch // bb, lp // BLOCK),
        in_specs=[cur(SWA_Q, Z_SQ), cur(SWA_KV, Z_SK), prev(SWA_KV, Z_SK),
                  cur(SWA_KV, Z_SV), prev(SWA_KV, Z_SV),
                  tab_c, tab_c, tab_c, tab_p, tab_p, tab_p,
                  const((1, SWA_Q)), const((1, SWA_KV)), const((LANES, LANES)),
                  any_spec, any_spec, any_spec],
        out_specs=(pl.BlockSpec((bb, BLOCK, SWA_Q),
                                lambda b, n, s: (b, _stored_block(n, lp), 0)),
                   any_spec, any_spec, any_spec),
        scratch_shapes=[pltpu.VMEM(chunk(w), F32) for w in ffn]
        + [pltpu.VMEM(chunk(w), BF16) for w in ffn]
        + [pltpu.SemaphoreType.DMA((2,)), pltpu.SemaphoreType.DMA((2,))],
    )
    return pl.pallas_call(
        functools.partial(_swa_kernel, cast_steps=cast_steps),
        out_shape=(jax.ShapeDtypeStruct((batch, lp, SWA_Q), BF16),)
        + tuple(jax.ShapeDtypeStruct(w.shape, BF16) for w in ffn),
        grid_spec=grid_spec,
        compiler_params=_params("arbitrary", "arbitrary"),
        name="swa",
    )(sinks, z, z, z, z, z, cos, sin_a, sin_b, cos, sin_a, sin_b, qn, kn, bd, *ffn)


def _ffn_kernel(h_ref, oa_ref, ob_ref, woa_ref, wob_ref, g_ref, w1_ref, w3_ref, w2_ref,
                *rest):
    out_ref, h1_sc, xn_sc, acc_sc = rest[-4:]
    j = pl.program_id(2)
    nj = pl.num_programs(2)

    @pl.when(j == 0)
    def _():
        h1 = h_ref[...] + _dot(oa_ref[...], woa_ref[...]) + _dot(ob_ref[...], wob_ref[...])
        h1_sc[...] = h1
        xn_sc[...] = _rms(h1, g_ref[...]).astype(BF16)

    x = xn_sc[...]
    a = _dot(x, w1_ref[...])
    b = _dot(x, w3_ref[...])
    hid = (a * jax.nn.sigmoid(a) * b).astype(BF16)
    part = _dot(hid, w2_ref[...])

    @pl.when(j == 0)
    def _():
        acc_sc[...] = h1_sc[...] + part

    @pl.when((j > 0) & (j < nj - 1))
    def _():
        acc_sc[...] += part

    @pl.when(j == nj - 1)
    def _():
        out_ref[...] = acc_sc[...] + part


def _mix_out_ffn(h, oa, ob, woa, wob, g, w1, w3, w2, tm, tf, first_row, out=None):
    hb, rows, _ = h.shape
    batch, lp, _ = oa.shape
    dff = w1.shape[1]
    row0 = first_row // tm
    const = lambda shape: pl.BlockSpec(shape, lambda b, i, j: (0, 0))
    in_specs = [pl.BlockSpec((None, tm, D_MODEL),
                             lambda b, i, j: (jnp.minimum(b, hb - 1), i, 0)),
                pl.BlockSpec((None, tm, GLA_V), lambda b, i, j: (b, row0 + i, 0)),
                pl.BlockSpec((None, tm, SWA_Q), lambda b, i, j: (b, row0 + i, 0)),
                const((GLA_V, D_MODEL)), const((SWA_Q, D_MODEL)), const((1, D_MODEL)),
                pl.BlockSpec((D_MODEL, tf), lambda b, i, j: (0, j)),
                pl.BlockSpec((D_MODEL, tf), lambda b, i, j: (0, j)),
                pl.BlockSpec((tf, D_MODEL), lambda b, i, j: (j, 0))]
    args = (h, oa, ob, woa, wob, g, w1, w3, w2)
    if out is not None:
        in_specs.append(pl.BlockSpec(memory_space=pl.ANY))
        args += (out,)
    return pl.pallas_call(
        _ffn_kernel,
        out_shape=jax.ShapeDtypeStruct((batch, lp, D_MODEL), F32),
        grid=(hb, rows // tm, dff // tf),
        in_specs=in_specs,
        out_specs=pl.BlockSpec((None, tm, D_MODEL), lambda b, i, j: (b, row0 + i, 0)),
        scratch_shapes=[pltpu.VMEM((tm, D_MODEL), F32), pltpu.VMEM((tm, D_MODEL), BF16),
                        pltpu.VMEM((tm, D_MODEL), F32)],
        input_output_aliases={} if out is None else {9: 0},
        compiler_params=_params("parallel", "parallel", "arbitrary"),
        name="mix_out_ffn",
    )(*args)


CONV_HALO = 32
CONV_SPAN = CONV_HALO + BLOCK
CONV_ROWS = 32
SUBLANES = 8


def _cast_stream(step, n_steps, srcs, dsts, in_bufs, out_bufs, in_sem, out_sem):
    slot = lax.rem(step, 2)

    def copies(s, sl, fetch):
        out = []
        for src, dst, ibuf, obuf in zip(srcs, dsts, in_bufs, out_bufs):
            rows = ibuf.shape[1]
            at = pl.ds(pl.multiple_of(s * rows, rows), rows)
            out.append(pltpu.make_async_copy(src.at[at], ibuf.at[sl], in_sem.at[sl]) if fetch
                       else pltpu.make_async_copy(obuf.at[sl], dst.at[at], out_sem.at[sl]))
        return out

    @pl.when(step == 0)
    def _():
        for cp in copies(step, slot, True):
            cp.start()

    for cp in copies(step, slot, True):
        cp.wait()

    @pl.when(step + 1 < n_steps)
    def _():
        for cp in copies(step + 1, 1 - slot, True):
            cp.start()

    @pl.when(step >= 2)
    def _():
        for cp in copies(step - 2, slot, False):
            cp.wait()

    for ibuf, obuf in zip(in_bufs, out_bufs):
        obuf[slot] = ibuf[slot].astype(BF16)
    for cp in copies(step, slot, False):
        cp.start()

    @pl.when(step == n_steps - 1)
    def _():
        @pl.when(step >= 1)
        def _():
            for cp in copies(step - 1, 1 - slot, False):
                cp.wait()

        for cp in copies(step, slot, False):
            cp.wait()


def _conv_kernel(h_ref, g_ref, wa_ref, wg_ref, shift_ref, wdw_ref, bdw_ref, lg_ref, lb_ref,
                 w2_ref, e1_ref, e3_ref, e2_ref, out_ref, e1b_ref, e3b_ref, e2b_ref,
                 u_sc, xs_sc, c_sc, y_sc, i1_sc, i3_sc, i2_sc, o1_sc, o3_sc, o2_sc, in_sem,
                 out_sem):
    n = pl.program_id(1)
    real_blocks = pl.num_programs(1) - 1
    cast_step = pl.program_id(0) * real_blocks + n - 1
    cast_steps = pl.num_programs(0) * real_blocks

    @pl.when(n > 0)
    def _():
        _cast_stream(cast_step, cast_steps, (e1_ref, e3_ref, e2_ref),
                     (e1b_ref, e3b_ref, e2b_ref), (i1_sc, i3_sc, i2_sc), (o1_sc, o3_sc, o2_sc),
                     in_sem, out_sem)

    h = h_ref[...]
    xn = _rms(h, g_ref[...]).astype(BF16)
    u = _dot(xn, wa_ref[...]) * jax.nn.sigmoid(_dot(xn, wg_ref[...]))
    u_sc[CONV_HALO:CONV_SPAN, :] = u.astype(BF16)

    @pl.when(n > 0)
    def _():
        span = u_sc[...]
        xs_sc[0] = span.astype(F32)
        shifted = _dot(shift_ref[...], span)
        for r in range(1, SUBLANES):
            xs_sc[r] = shifted[(r - 1) * CONV_SPAN:r * CONV_SPAN]
        first = CONV_HALO - (CONV_WIDTH - 1)
        for cb in range(D_MODEL // LANES):
            cs = slice(cb * LANES, (cb + 1) * LANES)
            acc = jnp.zeros((BLOCK // SUBLANES, SUBLANES, LANES), F32)
            for j in range(CONV_WIDTH):
                r = (first + j) % SUBLANES
                base = first + j - r
                x = xs_sc[r, base:base + BLOCK, cs].reshape(BLOCK // SUBLANES, SUBLANES, LANES)
                acc = acc + x * wdw_ref[j, :, cs]
            c_sc[:, cs] = acc.reshape(BLOCK, LANES)
        for rb in range(BLOCK // CONV_ROWS):
            acc = c_sc[rb * CONV_ROWS:(rb + 1) * CONV_ROWS, :] + bdw_ref[...]
            mu = jnp.mean(acc, axis=-1, keepdims=True)
            dev = acc - mu
            var = jnp.mean(dev * dev, axis=-1, keepdims=True)
            y = dev * lax.rsqrt(var + EPS) * lg_ref[...] + lb_ref[...]
            y_sc[rb * CONV_ROWS:(rb + 1) * CONV_ROWS, :] = (y * jax.nn.sigmoid(y)).astype(BF16)
        out_ref[...] = h + _dot(y_sc[...], w2_ref[...])

    u_sc[0:CONV_HALO, :] = u_sc[BLOCK:CONV_SPAN, :]


def _conv_module(h, g, w_pw1, shift, wdw, bdw, lg, lb, w2, experts, batch, nb):
    const = lambda shape: pl.BlockSpec(shape, lambda b, n: (0,) * len(shape))
    any_spec = pl.BlockSpec(memory_space=pl.ANY)
    n_chunks = batch * (nb - 1)
    chunk = lambda w: (2, w.shape[0] // n_chunks, w.shape[1])
    assert all(w.shape[0] % (n_chunks * GROUP) == 0 for w in experts)
    return pl.pallas_call(
        _conv_kernel,
        out_shape=(jax.ShapeDtypeStruct((batch * (nb - 1) * BLOCK, D_MODEL), F32),)
        + tuple(jax.ShapeDtypeStruct(w.shape, BF16) for w in experts),
        grid=(batch, nb),
        in_specs=[pl.BlockSpec((BLOCK, D_MODEL),
                               lambda b, n: (jnp.where(n == 0, 0, b) * nb
                                             + _stored_block(n, nb * BLOCK), 0)),
                  const((1, D_MODEL)),
                  pl.BlockSpec((D_MODEL, D_MODEL), lambda b, n: (0, 0)),
                  pl.BlockSpec((D_MODEL, D_MODEL), lambda b, n: (0, 1)),
                  const(((SUBLANES - 1) * CONV_SPAN, CONV_SPAN)),
                  const((CONV_WIDTH, SUBLANES, D_MODEL)), const((1, D_MODEL)), const((1, D_MODEL)),
                  const((1, D_MODEL)), const((D_MODEL, D_MODEL)),
                  any_spec, any_spec, any_spec],
        out_specs=(pl.BlockSpec((BLOCK, D_MODEL),
                                lambda b, n: (b * (nb - 1) + jnp.maximum(n - 1, 0), 0)),
                   any_spec, any_spec, any_spec),
        scratch_shapes=[pltpu.VMEM((CONV_SPAN, D_MODEL), BF16),
                        pltpu.VMEM((SUBLANES, CONV_SPAN, D_MODEL), F32),
                        pltpu.VMEM((BLOCK, D_MODEL), F32),
                        pltpu.VMEM((BLOCK, D_MODEL), BF16)]
        + [pltpu.VMEM(chunk(w), F32) for w in experts]
        + [pltpu.VMEM(chunk(w), BF16) for w in experts]
        + [pltpu.SemaphoreType.DMA((2,)), pltpu.SemaphoreType.DMA((2,))],
        compiler_params=_params("arbitrary", "arbitrary"),
        name="conv_module",
    )(h, g, w_pw1, w_pw1, shift, wdw, bdw, lg, lb, w2, *experts)


SORT_TOKENS = 512
GROUP = 16
SORT_CAP = 2 * SORT_TOKENS + N_EXPERTS * GROUP
SORT_COLS = D_MODEL + LANES
MOE_ROWS = 512
GROUPS_PER_TILE = MOE_ROWS // GROUP


def _one_hot_rows(pos1, pos2, width):
    r = lax.broadcasted_iota(jnp.int32, (pos1.shape[0], width), 1).astype(F32)
    return jnp.where(r == pos1, 1.0, jnp.where(r == pos2, 1.0, 0.0)).astype(BF16)


SORT_TILES_PER_STEP = 2
SPARE_ZERO = 0
SPARE_SINK = 64
assert SPARE_SINK + 2 * MOE_ROWS <= SORT_CAP


def _route_sort_kernel(h_ref, g_ref, wr_hi_ref, wr_lo_ref, xs_ref, pos_ref, ng_ref):
    is_spare = pl.program_id(0) == pl.num_programs(0) - 1

    @pl.when(is_spare)
    def _():
        xs_ref[...] = jnp.zeros_like(xs_ref)

    @pl.when(jnp.logical_not(is_spare))
    def _():
        tiles = [_route_sort_tile(h_ref.at[pl.ds(t * SORT_TOKENS, SORT_TOKENS)], g_ref,
                                  wr_hi_ref, wr_lo_ref,
                                  xs_ref.at[pl.ds(t * SORT_CAP, SORT_CAP)],
                                  pos_ref.at[pl.ds(t * SORT_TOKENS, SORT_TOKENS)], ng_ref.at[t])
                 for t in range(h_ref.shape[0] // SORT_TOKENS)]
        while tiles:
            tiles = [t for t in tiles if next(t, "done") != "done"]


def _route_sort_tile(h_ref, g_ref, wr_hi_ref, wr_lo_ref, xs_ref, pos_ref, ng_ref):
    s = SORT_TOKENS
    xn = _rms(h_ref[...], g_ref[...])
    x_hi = xn.astype(BF16)
    x_lo = (xn - x_hi.astype(F32)).astype(BF16)
    logits = (_dot(x_hi, wr_hi_ref[...]) + _dot(x_lo, wr_hi_ref[...])
              + _dot(x_hi, wr_lo_ref[...]))
    yield
    lane = lax.broadcasted_iota(jnp.int32, logits.shape, 1)
    logits = jnp.where(lane < N_EXPERTS, logits, -jnp.inf)
    m1 = jnp.max(logits, axis=-1, keepdims=True)
    i1 = jnp.min(jnp.where(logits == m1, lane, LANES), axis=-1, keepdims=True)
    rest = jnp.where(lane == i1, -jnp.inf, logits)
    m2 = jnp.max(rest, axis=-1, keepdims=True)
    i2 = jnp.min(jnp.where(rest == m2, lane, LANES), axis=-1, keepdims=True)
    e2 = jnp.exp(m2 - m1)
    g1 = 1.0 / (1.0 + e2)
    g2 = e2 / (1.0 + e2)
    yield

    oh1 = lane == i1
    oh2 = lane == i2
    oh = jnp.where(oh1, 1.0, jnp.where(oh2, 1.0, 0.0))
    cnt = jnp.sum(oh, axis=0, keepdims=True)
    ngr = jnp.floor((cnt + (GROUP - 1)) * (1.0 / GROUP))
    ri = lax.broadcasted_iota(jnp.int32, (s, s), 0)
    ci = lax.broadcasted_iota(jnp.int32, (s, s), 1)
    rank = _dot(jnp.where(ri > ci, 1.0, 0.0).astype(BF16), oh.astype(BF16))
    li = lax.broadcasted_iota(jnp.int32, (LANES, LANES), 0)
    lj = lax.broadcasted_iota(jnp.int32, (LANES, LANES), 1)
    before = _dot(jnp.broadcast_to(ngr, (8, LANES)).astype(BF16),
                  jnp.where(li < lj, 1.0, 0.0).astype(BF16))[0:1]
    yield
    base = before * GROUP + rank
    base1 = jnp.where(oh1, base, 0.0)
    base2 = jnp.where(oh2, base, 0.0)
    pos1 = jnp.sum(base1, axis=-1, keepdims=True)
    pos2 = jnp.sum(base2, axis=-1, keepdims=True)

    def as_row(v):
        ones = jnp.ones((8, LANES), BF16)
        hi = jnp.floor(v * (1.0 / 256.0))
        return (_dot_nt(ones, hi.astype(BF16)) * 256.0
                + _dot_nt(ones, (v - hi * 256.0).astype(BF16)))[0:1]

    row1, row2 = as_row(base1), as_row(base2)
    yield
    r = lax.broadcasted_iota(jnp.int32, (SORT_CAP, s), 0).astype(F32)
    p1 = jnp.where(r == row1, 1.0, 0.0).astype(BF16)
    p2 = jnp.where(r == row2, 1.0, 0.0).astype(BF16)
    xs_ref[:, 0:D_MODEL] = _dot(p1 + p2, x_hi).astype(BF16)

    def gate_lanes(gv):
        hi = gv.astype(BF16).astype(F32)
        return jnp.where(lane == 0, hi, jnp.where(lane == 1, gv - hi, 0.0)).astype(BF16)

    xs_ref[:, D_MODEL:SORT_COLS] = (_dot(p1, gate_lanes(g1))
                                    + _dot(p2, gate_lanes(g2))).astype(BF16)
    pos_ref[...] = jnp.where(lane == 0, pos1, jnp.where(lane == 1, pos2, 0.0))
    ng_ref[...] = jnp.broadcast_to(ngr, (8, LANES)).astype(jnp.int32)


def _route_sort(h, g, wr_hi, wr_lo):
    tokens = h.shape[0]
    nt = tokens // SORT_TOKENS
    tps = math.gcd(nt, SORT_TILES_PER_STEP)
    steps = nt // tps
    last = lambda i: jnp.minimum(i, steps - 1)
    return pl.pallas_call(
        _route_sort_kernel,
        out_shape=(jax.ShapeDtypeStruct(((nt + tps) * SORT_CAP, SORT_COLS), BF16),
                   jax.ShapeDtypeStruct((tokens, LANES), F32),
                   jax.ShapeDtypeStruct((nt, 8, LANES), jnp.int32)),
        grid=(steps + 1,),
        in_specs=[pl.BlockSpec((tps * SORT_TOKENS, D_MODEL), lambda i: (last(i), 0)),
                  pl.BlockSpec((1, D_MODEL), lambda i: (0, 0)),
                  pl.BlockSpec((D_MODEL, LANES), lambda i: (0, 0)),
                  pl.BlockSpec((D_MODEL, LANES), lambda i: (0, 0))],
        out_specs=(pl.BlockSpec((tps * SORT_CAP, SORT_COLS), lambda i: (i, 0)),
                   pl.BlockSpec((tps * SORT_TOKENS, LANES), lambda i: (last(i), 0)),
                   pl.BlockSpec((tps, 8, LANES), lambda i: (last(i), 0, 0))),
        compiler_params=_params("arbitrary"),
        name="route_sort",
    )(h, g, wr_hi, wr_lo)


def _group_tables(ng, n_row_tiles):
    nt, ne = ng.shape
    gpt = GROUPS_PER_TILE
    n_e = jnp.sum(ng, axis=0)
    rt_e = (n_e + gpt - 1) // gpt
    rt_end = jnp.cumsum(rt_e)
    n_rt = rt_end[-1]
    r = jnp.arange(n_row_tiles, dtype=jnp.int32)
    valid_r = r < n_rt
    e_r = jnp.minimum(jnp.sum((r[:, None] >= rt_end[None, :]).astype(jnp.int32), axis=1), ne - 1)
    e_last = jnp.sum(jnp.where(r == n_rt - 1, e_r, 0))
    e_r = jnp.where(valid_r, e_r, e_last)
    sel = (e_r[:, None] == jnp.arange(ne, dtype=jnp.int32)[None, :]).astype(jnp.int32)
    pick = lambda v: jnp.sum(sel * v[None, :], axis=1)
    k0 = (r - pick(rt_end - rt_e)) * gpt
    gidx = k0[:, None] + jnp.arange(gpt, dtype=jnp.int32)[None, :]
    valid = valid_r[:, None] & (gidx < pick(n_e)[:, None])
    incl = jnp.cumsum(ng, axis=0)
    incl_r = jnp.sum(sel[:, None, :] * incl[None, :, :], axis=2)
    ng_r = jnp.sum(sel[:, None, :] * ng[None, :, :], axis=2)
    lo = jnp.cumsum(ng, axis=1) - ng
    lo_r = jnp.sum(sel[:, None, :] * lo[None, :, :], axis=2)
    ti = jnp.sum((incl_r[:, None, :] <= gidx[:, :, None]).astype(jnp.int32), axis=2)
    ti = jnp.minimum(ti, nt - 1)
    tsel = (ti[:, :, None] == jnp.arange(nt, dtype=jnp.int32)[None, None, :]).astype(jnp.int32)
    tpick = lambda v: jnp.sum(tsel * v[:, None, :], axis=2)
    within = gidx - tpick(incl_r - ng_r)
    rows = ti * SORT_CAP + GROUP * (tpick(lo_r) + within)
    spare = nt * SORT_CAP
    sink = (spare + SPARE_SINK + (r % 2)[:, None] * MOE_ROWS
            + GROUP * jnp.arange(gpt, dtype=jnp.int32)[None, :])
    src = jnp.where(valid, rows, spare + SPARE_ZERO).reshape(-1)
    dst = jnp.where(valid, rows, sink).reshape(-1)
    return e_r, jnp.sum(valid.astype(jnp.int32), axis=1), src, dst


def _group_copies(rows_ref, tile, hbm, buf, slot, sem, to_hbm, start):
    for k in range(GROUPS_PER_TILE):
        off = pl.multiple_of(rows_ref[tile * GROUPS_PER_TILE + k], GROUP)
        buf_rows = buf.at[slot, pl.ds(k * GROUP, GROUP), :]
        hbm_rows = hbm.at[pl.ds(off, GROUP), pl.ds(0, buf_rows.shape[-1])]
        cp = (pltpu.make_async_copy(buf_rows, hbm_rows, sem.at[slot]) if to_hbm
              else pltpu.make_async_copy(hbm_rows, buf_rows, sem.at[slot]))
        if start:
            cp.start()
        else:
            cp.wait()


def _moe_group_kernel(te_ref, tv_ref, src_ref, dst_ref, xs_hbm, w1_ref, w3_ref, w2_ref, ys_hbm,
                      xbuf, ybuf, acc_sc, xsem, ysem):
    r = pl.program_id(0)
    j = pl.program_id(1)
    nr = pl.num_programs(0)
    nj = pl.num_programs(1)
    slot = lax.rem(r, 2)
    valid = lambda t: tv_ref[jnp.clip(t, 0, nr - 1)] > 0
    gather = functools.partial(_group_copies, src_ref, hbm=xs_hbm, buf=xbuf, sem=xsem,
                               to_hbm=False)
    scatter = functools.partial(_group_copies, dst_ref, hbm=ys_hbm, buf=ybuf, sem=ysem,
                                to_hbm=True)

    @pl.when(j == 0)
    def _():
        @pl.when((r == 0) & valid(r))
        def _():
            gather(tile=r, slot=slot, start=True)

        @pl.when(valid(r))
        def _():
            gather(tile=r, slot=slot, start=False)

        @pl.when((r + 1 < nr) & valid(r + 1))
        def _():
            gather(tile=r + 1, slot=1 - slot, start=True)

        @pl.when((r >= 2) & valid(r - 2))
        def _():
            scatter(tile=r - 2, slot=slot, start=False)

    def compute(rows):
        x = xbuf[slot, rows, 0:D_MODEL]
        a = _dot(x, w1_ref[0])
        b = _dot(x, w3_ref[0])
        hid = (a * jax.nn.sigmoid(a) * b).astype(BF16)
        part = _dot(hid, w2_ref[0])

        @pl.when(j == 0)
        def _():
            acc_sc[rows, :] = part

        @pl.when((j > 0) & (j < nj - 1))
        def _():
            acc_sc[rows, :] += part

        @pl.when(j == nj - 1)
        def _():
            gl = xbuf[slot, rows, D_MODEL:SORT_COLS].astype(F32)
            gate = gl[:, 0:1] + gl[:, 1:2]
            ybuf[slot, rows, :] = ((acc_sc[rows, :] + part) * gate).astype(BF16)

    half = MOE_ROWS // 2
    full = tv_ref[r] > GROUPS_PER_TILE // 2

    @pl.when(full)
    def _():
        compute(slice(0, MOE_ROWS))

    @pl.when(valid(r) & jnp.logical_not(full))
    def _():
        compute(slice(0, half))

        @pl.when(j == nj - 1)
        def _():
            ybuf[slot, half:MOE_ROWS, :] = jnp.zeros((half, D_MODEL), BF16)

    @pl.when(valid(r) & (j == nj - 1))
    def _():
        scatter(tile=r, slot=slot, start=True)

    @pl.when((r == nr - 1) & (j == nj - 1))
    def _():
        @pl.when((r >= 1) & valid(r - 1))
        def _():
            scatter(tile=r - 1, slot=1 - slot, start=False)

        @pl.when(valid(r))
        def _():
            scatter(tile=r, slot=slot, start=False)


def _moe_group(xs, tile_expert, tile_valid, src, dst, w1, w3, w2, n_row_tiles, tf):
    ne, _, dff = w1.shape
    nj = dff // tf
    assert nj >= 2

    def w_in(r, j, te, tv, s, d):
        return (te[r], 0, jnp.where(tv[r] > 0, j, nj - 1))

    def w_out(r, j, te, tv, s, d):
        return (te[r], jnp.where(tv[r] > 0, j, nj - 1), 0)

    grid_spec = pltpu.PrefetchScalarGridSpec(
        num_scalar_prefetch=4,
        grid=(n_row_tiles, nj),
        in_specs=[pl.BlockSpec(memory_space=pl.ANY),
                  pl.BlockSpec((1, D_MODEL, tf), w_in),
                  pl.BlockSpec((1, D_MODEL, tf), w_in),
                  pl.BlockSpec((1, tf, D_MODEL), w_out)],
        out_specs=pl.BlockSpec(memory_space=pl.ANY),
        scratch_shapes=[pltpu.VMEM((2, MOE_ROWS, SORT_COLS), BF16),
                        pltpu.VMEM((2, MOE_ROWS, D_MODEL), BF16),
                        pltpu.VMEM((MOE_ROWS, D_MODEL), F32),
                        pltpu.SemaphoreType.DMA((2,)),
                        pltpu.SemaphoreType.DMA((2,))],
    )
    return pl.pallas_call(
        _moe_group_kernel,
        out_shape=jax.ShapeDtypeStruct(xs.shape, xs.dtype),
        grid_spec=grid_spec,
        input_output_aliases={4: 0},
        compiler_params=_params("arbitrary", "arbitrary"),
        name="moe_group",
    )(tile_expert, tile_valid, src, dst, xs, w1, w3, w2)


def _combine_kernel(h_ref, pos_ref, ys_ref, out_ref):
    pos = pos_ref[...]
    pt = _one_hot_rows(pos[:, 0:1], pos[:, 1:2], SORT_CAP)
    out_ref[...] = h_ref[...] + _dot(pt, ys_ref[...])


def _combine(h, pos, ys):
    tokens = h.shape[0]
    return pl.pallas_call(
        _combine_kernel,
        out_shape=jax.ShapeDtypeStruct((tokens, D_MODEL), F32),
        grid=(tokens // SORT_TOKENS,),
        in_specs=[pl.BlockSpec((SORT_TOKENS, D_MODEL), lambda i: (i, 0)),
                  pl.BlockSpec((SORT_TOKENS, LANES), lambda i: (i, 0)),
                  pl.BlockSpec((SORT_CAP, D_MODEL), lambda i: (i, 0))],
        out_specs=pl.BlockSpec((SORT_TOKENS, D_MODEL), lambda i: (i, 0)),
        compiler_params=_params("parallel"),
        name="combine",
    )(h, pos, ys)


def _row_tile(rows, want):
    tm = want
    while rows % tm:
        tm //= 2
    return tm


def _rope_tables(lp):
    inv_freq = 1.0 / (ROPE_THETA ** (jnp.arange(0, SWA_HEAD_DIM, 2, dtype=F32) / SWA_HEAD_DIM))
    pos = jnp.arange(lp, dtype=F32) - FRONT
    ang = pos[:, None] * inv_freq[None, :]
    cos, sin = jnp.cos(ang), jnp.sin(ang)
    zero = jnp.zeros_like(sin)
    rep = LANES // SWA_HEAD_DIM
    cos_t = jnp.tile(jnp.concatenate([cos, cos], axis=-1), (1, rep))
    sin_a = jnp.tile(jnp.concatenate([zero, sin], axis=-1), (1, rep))
    sin_b = jnp.tile(jnp.concatenate([-sin, zero], axis=-1), (1, rep))
    return cos_t, sin_a, sin_b


def kernel(x, meta, a_norm, a_w_in, a_w_gate2, a_b_gate, a_q_norm, a_k_norm, a_sinks, a_o_norm,
           a_w_out, f_norm, f_w1, f_w3, f_w2, c_norm, c_w_pw1, c_w_dw, c_b_dw, c_ln_g, c_ln_b,
           c_w_pw2, m_norm, m_w_router, m_w1, m_w3, m_w2):
    batch, seq, _ = x.shape
    assert seq % BLOCK == 0
    lp = BLOCK + seq
    nb = lp // BLOCK
    rows = batch * lp

    head = jnp.concatenate([jnp.zeros((FRONT, D_MODEL), x.dtype), meta.astype(x.dtype)],
                           axis=0)[None]
    tm = _row_tile(seq, 512)

    w_in = a_w_in[0]
    gq, gk, gv, gr, glr, sq, sk, sv = jnp.split(
        w_in, [256, 512, 1024, 1536, 1552, 2064, 2192], axis=-1)
    w_in_r = jnp.concatenate(
        [gq, gk, gv, gr, sq, sk, sv, glr,
         jnp.zeros((D_MODEL, Z_COLS - Z_LR - GLA_LOWRANK), w_in.dtype)], axis=-1).astype(BF16)
    z = _in_proj(x, a_norm[0][None], w_in_r, batch, lp, tm, 0)
    z = _in_proj(head, a_norm[0][None], w_in_r, batch, lp, BLOCK, seq, z)

    wg = jnp.zeros((LANES, GLA_QK), F32).at[:GLA_LOWRANK].set(a_w_gate2[0]).astype(BF16)
    bb = math.gcd(batch, MIX_BATCH)
    o_a = _gla(z, wg, a_b_gate[0][None], a_o_norm[0][None], bb)

    cos_t, sin_a, sin_b = _rope_tables(lp)
    qn = jnp.tile(a_q_norm[0], SWA_Q_HEADS)[None]
    kn = jnp.tile(a_k_norm[0], SWA_KV_HEADS)[None]
    lane = jnp.arange(LANES)
    bd = (lane[:, None] // SWA_HEAD_DIM == lane[None, :] // SWA_HEAD_DIM).astype(BF16)
    o_b, d_w1, d_w3, d_w2 = _swa(z, a_sinks[0], cos_t, sin_a, sin_b, qn, kn, bd,
                                 (f_w1[0], f_w3[0], f_w2[0]), bb)

    w_out = a_w_out[0].astype(BF16)
    ffn = (w_out[:GLA_V], w_out[GLA_V:], f_norm[0][None], d_w1, d_w3, d_w2)
    h = _mix_out_ffn(x, o_a, o_b, *ffn, tm, FF_TILE, 0)
    h = _mix_out_ffn(head, o_a, o_b, *ffn, BLOCK, FF_TILE, seq, h).reshape(rows, D_MODEL)

    si = jnp.arange((SUBLANES - 1) * CONV_SPAN)
    shift = (jnp.arange(CONV_SPAN)[None, :]
             == (si % CONV_SPAN + si // CONV_SPAN + 1)[:, None]).astype(BF16)
    ne, _, dff = m_w1[0].shape
    experts = (m_w1[0].reshape(ne * D_MODEL, dff), m_w3[0].reshape(ne * D_MODEL, dff),
               m_w2[0].reshape(ne * dff, D_MODEL))
    h, e_w1, e_w3, e_w2 = _conv_module(
        h, c_norm[0][None], c_w_pw1[0].astype(BF16), shift,
        jnp.broadcast_to(c_w_dw[0][:, None, :], (CONV_WIDTH, SUBLANES, D_MODEL)),
        c_b_dw[0][None], c_ln_g[0][None], c_ln_b[0][None], c_w_pw2[0].astype(BF16), experts,
        batch, nb)

    wr = jnp.zeros((D_MODEL, LANES), F32).at[:, :N_EXPERTS].set(m_w_router[0])
    wr_hi = wr.astype(BF16)
    wr_lo = (wr - wr_hi.astype(F32)).astype(BF16)
    tokens = batch * seq
    assert tokens % SORT_TOKENS == 0
    nt = tokens // SORT_TOKENS
    xs, pos, ng = _route_sort(h, m_norm[0][None], wr_hi, wr_lo)
    ng = ng[:, 0, :N_EXPERTS]
    n_row_tiles = (nt * (SORT_CAP // GROUP)) // GROUPS_PER_TILE + N_EXPERTS
    tile_expert, tile_valid, src, dst = _group_tables(ng, n_row_tiles)
    ys = _moe_group(xs, tile_expert, tile_valid, src, dst, e_w1.reshape(ne, D_MODEL, dff),
                    e_w3.reshape(ne, D_MODEL, dff), e_w2.reshape(ne, dff, D_MODEL), n_row_tiles,
                    FF_TILE)
    out = _combine(h, pos, ys)
    return out.reshape(batch, seq, D_MODEL)
```

```python
import functools
import math

import jax
import jax.numpy as jnp
from jax import lax
from jax.experimental import pallas as pl
from jax.experimental.pallas import tpu as pltpu

F32 = jnp.float32
BF16 = jnp.bfloat16

D_MODEL = 1024
N_META = 16
EPS = 1e-6
ROPE_THETA = 10000.0
NEG_INF = -1e30
LOG2E = math.log2(math.e)
GLA_HEADS = 4
GLA_DK = 64
GLA_DV = 128
GLA_LOWRANK = 16
GLA_TAU = 16.0
GLA_CHUNK = 64
SWA_Q_HEADS = 8
SWA_KV_HEADS = 2
SWA_HEAD_DIM = 64
SWA_WINDOW = 128
CONV_WIDTH = 31
D_FF = 3584
N_EXPERTS = 8

GLA_QK = GLA_HEADS * GLA_DK
GLA_V = GLA_HEADS * GLA_DV
SWA_Q = SWA_Q_HEADS * SWA_HEAD_DIM
SWA_KV = SWA_KV_HEADS * SWA_HEAD_DIM
SWA_GROUP = SWA_Q_HEADS // SWA_KV_HEADS

LANES = 128
FF_TILE = 1792
BLOCK = 128
FRONT = BLOCK - N_META
VMEM_LIMIT = 56 * 1024 * 1024

Z_Q, Z_K, Z_V, Z_R = 0, 256, 512, 1024
Z_SQ, Z_SK, Z_SV, Z_LR = 1536, 2048, 2176, 2304
Z_COLS = 2432


def _rms(x, g):
    return x * lax.rsqrt(jnp.mean(x * x, axis=-1, keepdims=True) + EPS) * g


def _dot(a, b):
    return jnp.dot(a, b, preferred_element_type=F32)


def _dot_nt(a, b):
    return lax.dot_general(a, b, (((1,), (1,)), ((), ())), preferred_element_type=F32)


def _dot_tn(a, b):
    return lax.dot_general(a, b, (((0,), (0,)), ((), ())), preferred_element_type=F32)


def _stored_block(n, lp):
    return jnp.where(n == 0, lp // BLOCK - 1, n - 1)


def _params(*sem):
    return pltpu.CompilerParams(dimension_semantics=sem, vmem_limit_bytes=VMEM_LIMIT)


def _in_proj_kernel(h_ref, g_ref, w_ref, *rest):
    z_ref = rest[-1]
    xn = _rms(h_ref[...], g_ref[...]).astype(BF16)
    z = _dot(xn, w_ref[...]).astype(z_ref.dtype)
    z_ref[...] = z if z_ref.ndim == 2 else jnp.broadcast_to(z[None], z_ref.shape)


def _in_proj(h, g, w, batch, lp, tm, first_row, z=None):
    hb, rows, _ = h.shape
    n = w.shape[1]
    shared = hb == 1 and batch > 1
    in_specs = [pl.BlockSpec((None, tm, D_MODEL), lambda b, j: (b, j, 0)),
                pl.BlockSpec((1, D_MODEL), lambda b, j: (0, 0)),
                pl.BlockSpec((D_MODEL, n), lambda b, j: (0, 0))]
    args = (h, g, w)
    if z is not None:
        in_specs.append(pl.BlockSpec(memory_space=pl.ANY))
        args += (z,)
    return pl.pallas_call(
        _in_proj_kernel,
        out_shape=jax.ShapeDtypeStruct((batch, lp, n), BF16),
        grid=(hb, rows // tm),
        in_specs=in_specs,
        out_specs=pl.BlockSpec((batch if shared else None, tm, n),
                               lambda b, j: (b, first_row // tm + j, 0)),
        input_output_aliases={} if z is None else {3: 0},
        compiler_params=_params("parallel", "parallel"),
        name="in_proj",
    )(*args)


MIX_BATCH = 8


def _gla_kernel(q_ref, k_ref, v_ref, r_ref, lr_ref, wg_ref, bg_ref, on_ref, o_ref, st_ref):
    n = pl.program_id(1)

    @pl.when(n == 0)
    def _():
        st_ref[...] = jnp.zeros_like(st_ref)

    c = GLA_CHUNK
    items = [(bi, ch) for bi in range(q_ref.shape[0]) for ch in range(BLOCK // c)]
    ri = lax.broadcasted_iota(jnp.int32, (BLOCK, BLOCK), 0)
    ci = lax.broadcasted_iota(jnp.int32, (BLOCK, BLOCK), 1)
    tri = jnp.where((ri >= ci) & (ri // c == ci // c), 1.0, 0.0).astype(BF16)
    causal = (lax.broadcasted_iota(jnp.int32, (c, c), 0)
              >= lax.broadcasted_iota(jnp.int32, (c, c), 1))
    row = lax.broadcasted_iota(jnp.int32, (BLOCK, GLA_QK), 0) + n * BLOCK
    on = on_ref[...]

    b_all = []
    for bi in range(q_ref.shape[0]):
        gp = _dot(lr_ref[bi], wg_ref[...]) + bg_ref[...]
        g = (jnp.minimum(gp, 0.0) - jnp.log(1.0 + jnp.exp(-jnp.abs(gp)))) * (1.0 / GLA_TAU)
        g = jnp.where(row >= FRONT, g, 0.0)
        g_hi = g.astype(BF16)
        g_lo = (g - g_hi.astype(F32)).astype(BF16)
        b_all.append(_dot(tri, g_hi) + _dot(tri, g_lo))

    q_t, k_t, k_d, dec = {}, {}, {}, {}
    for bi, ch in items:
        rs = slice(ch * c, (ch + 1) * c)
        b = b_all[bi][rs]
        b_last = b_all[bi][(ch + 1) * c - 1:(ch + 1) * c, :]
        qf = q_ref[bi, rs, :].astype(F32)
        kf = k_ref[bi, rs, :].astype(F32)
        q_t[bi, ch] = (qf * jnp.exp(b) * (GLA_DK ** -0.5)).astype(BF16)
        k_t[bi, ch] = (kf * jnp.exp(-b)).astype(BF16)
        k_d[bi, ch] = (kf * jnp.exp(b_last - b)).astype(BF16)
        dec[bi, ch] = jnp.exp(b_last)

    att, kv = {}, {}
    for bi, ch in items:
        rs = slice(ch * c, (ch + 1) * c)
        for hd in range(GLA_HEADS):
            ks = slice(hd * GLA_DK, (hd + 1) * GLA_DK)
            vh = v_ref[bi, rs, hd * GLA_DV:(hd + 1) * GLA_DV]
            att[bi, ch, hd] = jnp.where(
                causal, _dot_nt(q_t[bi, ch][:, ks], k_t[bi, ch][:, ks]), 0.0).astype(BF16)
            kv[bi, ch, hd] = _dot_tn(vh, k_d[bi, ch][:, ks])

    for bi in range(q_ref.shape[0]):
        for hd in range(GLA_HEADS):
            ks = slice(hd * GLA_DK, (hd + 1) * GLA_DK)
            vs = slice(hd * GLA_DV, (hd + 1) * GLA_DV)
            st = st_ref[bi, hd]
            for ch in range(BLOCK // c):
                rs = slice(ch * c, (ch + 1) * c)
                o = (_dot(att[bi, ch, hd], v_ref[bi, rs, vs])
                     + _dot_nt(q_t[bi, ch][:, ks], st.astype(BF16)))
                st = st * dec[bi, ch][:, ks] + kv[bi, ch, hd]
                o = _rms(o, on)
                r = r_ref[bi, rs, vs].astype(F32)
                o_ref[bi, rs, vs] = (o * (r * jax.nn.sigmoid(r))).astype(o_ref.dtype)
            st_ref[bi, hd] = st


def _gla(z, wg, bg, on, bb):
    batch, lp, _ = z.shape

    def col(width, start):
        return pl.BlockSpec((bb, BLOCK, width), lambda b, n: (b, _stored_block(n, lp), start // width))

    const = lambda shape: pl.BlockSpec(shape, lambda b, n: (0,) * len(shape))
    return pl.pallas_call(
        _gla_kernel,
        out_shape=jax.ShapeDtypeStruct((batch, lp, GLA_V), BF16),
        grid=(batch // bb, lp // BLOCK),
        in_specs=[col(GLA_QK, Z_Q), col(GLA_QK, Z_K), col(GLA_V, Z_V), col(GLA_V, Z_R),
                  col(LANES, Z_LR), const((LANES, GLA_QK)), const((1, GLA_QK)),
                  const((1, GLA_DV))],
        out_specs=pl.BlockSpec((bb, BLOCK, GLA_V), lambda b, n: (b, _stored_block(n, lp), 0)),
        scratch_shapes=[pltpu.VMEM((bb, GLA_HEADS, GLA_DV, GLA_DK), F32)],
        compiler_params=_params("parallel", "arbitrary"),
        name="gla",
    )(z, z, z, z, z, wg, bg, on)


def _group_mean_sq(x, bd):
    sq = x * x
    hi = sq.astype(BF16)
    lo = (sq - hi.astype(F32)).astype(BF16)
    return _dot(hi, bd) + _dot(lo, bd)


def _norm_rope(x, gain, cos, sin_a, sin_b, bd):
    nbat, rows, width = x.shape
    x2 = x.reshape(nbat * rows, width)
    parts = []
    for s in range(width // LANES):
        xs = x2[:, s * LANES:(s + 1) * LANES]
        ms = _group_mean_sq(xs, bd) * (1.0 / SWA_HEAD_DIM)
        parts.append(xs * lax.rsqrt(ms + EPS))
    xn = (parts[0] if len(parts) == 1 else jnp.concatenate(parts, axis=-1)) * gain
    half = SWA_HEAD_DIM // 2
    back = lambda t: t.reshape(nbat, rows, width)
    return (back(xn) * cos + back(pltpu.roll(xn, half, 1)) * sin_a
            + back(pltpu.roll(xn, width - half, 1)) * sin_b)


def _swa_kernel(sinks_ref, q_ref, kc_ref, kp_ref, vc_ref, vp_ref, cosc_ref, sac_ref, sbc_ref,
                cosp_ref, sap_ref, sbp_ref, qn_ref, kn_ref, bd_ref, f1_ref, f3_ref, f2_ref,
                o_ref, f1b_ref, f3b_ref, f2b_ref, i1_sc, i3_sc, i2_sc, o1_sc, o3_sc, o2_sc,
                in_sem, out_sem, *, cast_steps):
    n = pl.program_id(1)
    nbat = q_ref.shape[0]

    cast_step = pl.program_id(0) * pl.num_programs(1) + n

    @pl.when(cast_step < cast_steps)
    def _():
        _cast_stream(cast_step, cast_steps, (f1_ref, f3_ref, f2_ref),
                     (f1b_ref, f3b_ref, f2b_ref), (i1_sc, i3_sc, i2_sc), (o1_sc, o3_sc, o2_sc),
                     in_sem, out_sem)

    bd = bd_ref[...]
    rep = SWA_Q // LANES
    tile = lambda t: jnp.concatenate([t] * rep, axis=-1)
    cos_c, sa_c, sb_c = cosc_ref[...], sac_ref[...], sbc_ref[...]
    q = _norm_rope(q_ref[...].astype(F32), qn_ref[...], tile(cos_c), tile(sa_c), tile(sb_c), bd)
    q = (q * (SWA_HEAD_DIM ** -0.5 * LOG2E)).astype(BF16)
    k_c = _norm_rope(kc_ref[...].astype(F32), kn_ref[...], cos_c, sa_c, sb_c, bd).astype(BF16)
    k_p = _norm_rope(kp_ref[...].astype(F32), kn_ref[...], cosp_ref[...], sap_ref[...],
                     sbp_ref[...], bd).astype(BF16)

    gq = SWA_GROUP * BLOCK
    r = lax.broadcasted_iota(jnp.int32, (gq, 2 * BLOCK), 0) & (BLOCK - 1)
    cidx = lax.broadcasted_iota(jnp.int32, (gq, 2 * BLOCK), 1)
    allowed = (cidx > r) & (cidx <= r + SWA_WINDOW) & ((n - 1) * BLOCK + cidx >= FRONT)
    rgroup = lax.broadcasted_iota(jnp.int32, (gq, 1), 0) // BLOCK
    items = [(bi, hk) for bi in range(nbat) for hk in range(SWA_KV_HEADS)]

    s, sink = {}, {}
    for hk in range(SWA_KV_HEADS):
        sk = jnp.zeros((gq, 1), F32)
        for g in range(SWA_GROUP):
            sk = jnp.where(rgroup == g, sinks_ref[hk * SWA_GROUP + g] * LOG2E, sk)
        sink[hk] = sk
    for bi, hk in items:
        ks = slice(hk * SWA_HEAD_DIM, (hk + 1) * SWA_HEAD_DIM)
        qg = jnp.concatenate(
            [q[bi][:, (hk * SWA_GROUP + g) * SWA_HEAD_DIM:(hk * SWA_GROUP + g + 1) * SWA_HEAD_DIM]
             for g in range(SWA_GROUP)], axis=0)
        k = jnp.concatenate([k_p[bi][:, ks], k_c[bi][:, ks]], axis=0)
        s[bi, hk] = jnp.where(allowed, _dot_nt(qg, k), NEG_INF)

    p, denom = {}, {}
    for bi, hk in items:
        m = jnp.maximum(jnp.max(s[bi, hk], axis=-1, keepdims=True), sink[hk])
        e = jnp.exp2(s[bi, hk] - m)
        denom[bi, hk] = jnp.sum(e, axis=-1, keepdims=True) + jnp.exp2(sink[hk] - m)
        p[bi, hk] = e.astype(BF16)

    for bi, hk in items:
        ks = slice(hk * SWA_HEAD_DIM, (hk + 1) * SWA_HEAD_DIM)
        v = jnp.concatenate([vp_ref[bi, :, ks], vc_ref[bi, :, ks]], axis=0)
        o = _dot(p[bi, hk], v) / denom[bi, hk]
        for g in range(SWA_GROUP):
            hq = hk * SWA_GROUP + g
            o_ref[bi, :, hq * SWA_HEAD_DIM:(hq + 1) * SWA_HEAD_DIM] = (
                o[g * BLOCK:(g + 1) * BLOCK].astype(o_ref.dtype))


def _swa(z, sinks, cos, sin_a, sin_b, qn, kn, bd, ffn, bb):
    batch, lp, _ = z.shape
    steps = (batch // bb) * (lp // BLOCK)
    cast_steps = 1 << (steps.bit_length() - 1)
    assert all(w.shape[0] % (cast_steps * GROUP) == 0 for w in ffn)
    chunk = lambda w: (2, w.shape[0] // cast_steps, w.shape[1])
    any_spec = pl.BlockSpec(memory_space=pl.ANY)

    def cur(width, start):
        return pl.BlockSpec((bb, BLOCK, width), lambda b, n, s: (b, _stored_block(n, lp), start // width))

    def prev(width, start):
        return pl.BlockSpec((bb, BLOCK, width),
                            lambda b, n, s: (b, _stored_block(jnp.maximum(n - 1, 0), lp),
                                             start // width))

    tab_c = pl.BlockSpec((BLOCK, LANES), lambda b, n, s: (n, 0))
    tab_p = pl.BlockSpec((BLOCK, LANES), lambda b, n, s: (jnp.maximum(n - 1, 0), 0))
    const = lambda shape: pl.BlockSpec(shape, lambda b, n, s: (0,) * len(shape))
    grid_spec = pltpu.PrefetchScalarGridSpec(
        num_scalar_prefetch=1,
        grid=(batch // bb, lp // BLOCK),
        in_specs=[cur(SWA_Q, Z_SQ), cur(SWA_KV, Z_SK), prev(SWA_KV, Z_SK),
                  cur(SWA_KV, Z_SV), prev(SWA_KV, Z_SV),
                  tab_c, tab_c, tab_c, tab_p, tab_p, tab_p,
                  const((1, SWA_Q)), const((1, SWA_KV)), const((LANES, LANES)),
                  any_spec, any_spec, any_spec],
        out_specs=(pl.BlockSpec((bb, BLOCK, SWA_Q),
                                lambda b, n, s: (b, _stored_block(n, lp), 0)),
                   any_spec, any_spec, any_spec),
        scratch_shapes=[pltpu.VMEM(chunk(w), F32) for w in ffn]
        + [pltpu.VMEM(chunk(w), BF16) for w in ffn]
        + [pltpu.SemaphoreType.DMA((2,)), pltpu.SemaphoreType.DMA((2,))],
    )
    return pl.pallas_call(
        functools.partial(_swa_kernel, cast_steps=cast_steps),
        out_shape=(jax.ShapeDtypeStruct((batch, lp, SWA_Q), BF16),)
        + tuple(jax.ShapeDtypeStruct(w.shape, BF16) for w in ffn),
        grid_spec=grid_spec,
        compiler_params=_params("arbitrary", "arbitrary"),
        name="swa",
    )(sinks, z, z, z, z, z, cos, sin_a, sin_b, cos, sin_a, sin_b, qn, kn, bd, *ffn)


def _ffn_kernel(h_ref, oa_ref, ob_ref, woa_ref, wob_ref, g_ref, w1_ref, w3_ref, w2_ref,
                *rest):
    out_ref, h1_sc, xn_sc, acc_sc = rest[-4:]
    j = pl.program_id(2)
    nj = pl.num_programs(2)

    @pl.when(j == 0)
    def _():
        h1 = h_ref[...] + _dot(oa_ref[...], woa_ref[...]) + _dot(ob_ref[...], wob_ref[...])
        h1_sc[...] = h1
        xn_sc[...] = _rms(h1, g_ref[...]).astype(BF16)

    x = xn_sc[...]
    a = _dot(x, w1_ref[...])
    b = _dot(x, w3_ref[...])
    hid = (a * jax.nn.sigmoid(a) * b).astype(BF16)
    part = _dot(hid, w2_ref[...])

    @pl.when(j == 0)
    def _():
        acc_sc[...] = h1_sc[...] + part

    @pl.when((j > 0) & (j < nj - 1))
    def _():
        acc_sc[...] += part

    @pl.when(j == nj - 1)
    def _():
        out_ref[...] = acc_sc[...] + part


def _mix_out_ffn(h, oa, ob, woa, wob, g, w1, w3, w2, tm, tf, first_row, out=None):
    hb, rows, _ = h.shape
    batch, lp, _ = oa.shape
    dff = w1.shape[1]
    row0 = first_row // tm
    const = lambda shape: pl.BlockSpec(shape, lambda b, i, j: (0, 0))
    in_specs = [pl.BlockSpec((None, tm, D_MODEL),
                             lambda b, i, j: (jnp.minimum(b, hb - 1), i, 0)),
                pl.BlockSpec((None, tm, GLA_V), lambda b, i, j: (b, row0 + i, 0)),
                pl.BlockSpec((None, tm, SWA_Q), lambda b, i, j: (b, row0 + i, 0)),
                const((GLA_V, D_MODEL)), const((SWA_Q, D_MODEL)), const((1, D_MODEL)),
                pl.BlockSpec((D_MODEL, tf), lambda b, i, j: (0, j)),
                pl.BlockSpec((D_MODEL, tf), lambda b, i, j: (0, j)),
                pl.BlockSpec((tf, D_MODEL), lambda b, i, j: (j, 0))]
    args = (h, oa, ob, woa, wob, g, w1, w3, w2)
    if out is not None:
        in_specs.append(pl.BlockSpec(memory_space=pl.ANY))
        args += (out,)
    return pl.pallas_call(
        _ffn_kernel,
        out_shape=jax.ShapeDtypeStruct((batch, lp, D_MODEL), F32),
        grid=(hb, rows // tm, dff // tf),
        in_specs=in_specs,
        out_specs=pl.BlockSpec((None, tm, D_MODEL), lambda b, i, j: (b, row0 + i, 0)),
        scratch_shapes=[pltpu.VMEM((tm, D_MODEL), F32), pltpu.VMEM((tm, D_MODEL), BF16),
                        pltpu.VMEM((tm, D_MODEL), F32)],
        input_output_aliases={} if out is None else {9: 0},
        compiler_params=_params("parallel", "parallel", "arbitrary"),
        name="mix_out_ffn",
    )(*args)


CONV_BLOCK = BLOCK
CONV_HALO = 32
CONV_SPAN = CONV_HALO + CONV_BLOCK
CONV_ROWS = 32
SUBLANES = 8


def _cast_stream(step, n_steps, srcs, dsts, in_bufs, out_bufs, in_sem, out_sem):
    slot = lax.rem(step, 2)

    def copies(s, sl, fetch):
        out = []
        for src, dst, ibuf, obuf in zip(srcs, dsts, in_bufs, out_bufs):
            rows = ibuf.shape[1]
            at = pl.ds(pl.multiple_of(s * rows, rows), rows)
            out.append(pltpu.make_async_copy(src.at[at], ibuf.at[sl], in_sem.at[sl]) if fetch
                       else pltpu.make_async_copy(obuf.at[sl], dst.at[at], out_sem.at[sl]))
        return out

    @pl.when(step == 0)
    def _():
        for cp in copies(step, slot, True):
            cp.start()

    for cp in copies(step, slot, True):
        cp.wait()

    @pl.when(step + 1 < n_steps)
    def _():
        for cp in copies(step + 1, 1 - slot, True):
            cp.start()

    @pl.when(step >= 2)
    def _():
        for cp in copies(step - 2, slot, False):
            cp.wait()

    for ibuf, obuf in zip(in_bufs, out_bufs):
        obuf[slot] = ibuf[slot].astype(BF16)
    for cp in copies(step, slot, False):
        cp.start()

    @pl.when(step == n_steps - 1)
    def _():
        @pl.when(step >= 1)
        def _():
            for cp in copies(step - 1, 1 - slot, False):
                cp.wait()

        for cp in copies(step, slot, False):
            cp.wait()


def _conv_kernel(h_ref, hm_ref, g_ref, wa_ref, wg_ref, shift_ref, wdw_ref, bdw_ref, lg_ref, lb_ref,
                 w2_ref, e1_ref, e3_ref, e2_ref, out_ref, e1b_ref, e3b_ref, e2b_ref,
                 u_sc, xs_sc, c_sc, y_sc, i1_sc, i3_sc, i2_sc, o1_sc, o3_sc, o2_sc, in_sem,
                 out_sem):
    n = pl.program_id(1)
    _cast_stream(pl.program_id(0) * pl.num_programs(1) + n,
                 pl.num_programs(0) * pl.num_programs(1), (e1_ref, e3_ref, e2_ref),
                 (e1b_ref, e3b_ref, e2b_ref), (i1_sc, i3_sc, i2_sc), (o1_sc, o3_sc, o2_sc),
                 in_sem, out_sem)

    def glu(x):
        xn = _rms(x, g_ref[...]).astype(BF16)
        return (_dot(xn, wa_ref[...]) * jax.nn.sigmoid(_dot(xn, wg_ref[...]))).astype(BF16)

    @pl.when(n == 0)
    def _():
        u_sc[0:CONV_HALO, :] = glu(hm_ref[BLOCK - CONV_HALO:BLOCK, :])

    h = h_ref[...]
    u_sc[CONV_HALO:CONV_SPAN, :] = glu(h)
    span = u_sc[...]
    xs_sc[0] = span.astype(F32)
    shifted = _dot(shift_ref[...], span)
    for r in range(1, SUBLANES):
        xs_sc[r] = shifted[(r - 1) * CONV_SPAN:r * CONV_SPAN]
    first = CONV_HALO - (CONV_WIDTH - 1)
    for part in range(CONV_BLOCK // BLOCK):
        for cb in range(D_MODEL // LANES):
            cs = slice(cb * LANES, (cb + 1) * LANES)
            acc = jnp.zeros((BLOCK // SUBLANES, SUBLANES, LANES), F32)
            for j in range(CONV_WIDTH):
                r = (first + j) % SUBLANES
                base = first + j - r + part * BLOCK
                x = xs_sc[r, base:base + BLOCK, cs].reshape(BLOCK // SUBLANES, SUBLANES, LANES)
                acc = acc + x * wdw_ref[j, :, cs]
            c_sc[part * BLOCK:(part + 1) * BLOCK, cs] = acc.reshape(BLOCK, LANES)
    for rb in range(CONV_BLOCK // CONV_ROWS):
        acc = c_sc[rb * CONV_ROWS:(rb + 1) * CONV_ROWS, :] + bdw_ref[...]
        mu = jnp.mean(acc, axis=-1, keepdims=True)
        dev = acc - mu
        var = jnp.mean(dev * dev, axis=-1, keepdims=True)
        y = dev * lax.rsqrt(var + EPS) * lg_ref[...] + lb_ref[...]
        y_sc[rb * CONV_ROWS:(rb + 1) * CONV_ROWS, :] = (y * jax.nn.sigmoid(y)).astype(BF16)
    out_ref[...] = h + _dot(y_sc[...], w2_ref[...])
    u_sc[0:CONV_HALO, :] = u_sc[CONV_BLOCK:CONV_SPAN, :]


def _conv_module(h, g, w_pw1, shift, wdw, bdw, lg, lb, w2, experts):
    batch, lp, _ = h.shape
    seq = lp - BLOCK
    const = lambda shape: pl.BlockSpec(shape, lambda b, n: (0,) * len(shape))
    any_spec = pl.BlockSpec(memory_space=pl.ANY)
    n_chunks = batch * (seq // CONV_BLOCK)
    chunk = lambda w: (2, w.shape[0] // n_chunks, w.shape[1])
    assert seq % CONV_BLOCK == 0
    assert all(w.shape[0] % (n_chunks * GROUP) == 0 for w in experts)
    return pl.pallas_call(
        _conv_kernel,
        out_shape=(jax.ShapeDtypeStruct((batch, seq, D_MODEL), F32),)
        + tuple(jax.ShapeDtypeStruct(w.shape, BF16) for w in experts),
        grid=(batch, seq // CONV_BLOCK),
        in_specs=[pl.BlockSpec((None, CONV_BLOCK, D_MODEL), lambda b, n: (b, n, 0)),
                  pl.BlockSpec((None, BLOCK, D_MODEL), lambda b, n: (0, seq // BLOCK, 0)),
                  const((1, D_MODEL)),
                  pl.BlockSpec((D_MODEL, D_MODEL), lambda b, n: (0, 0)),
                  pl.BlockSpec((D_MODEL, D_MODEL), lambda b, n: (0, 1)),
                  const(((SUBLANES - 1) * CONV_SPAN, CONV_SPAN)),
                  const((CONV_WIDTH, SUBLANES, D_MODEL)), const((1, D_MODEL)), const((1, D_MODEL)),
                  const((1, D_MODEL)), const((D_MODEL, D_MODEL)),
                  any_spec, any_spec, any_spec],
        out_specs=(pl.BlockSpec((None, CONV_BLOCK, D_MODEL), lambda b, n: (b, n, 0)),
                   any_spec, any_spec, any_spec),
        scratch_shapes=[pltpu.VMEM((CONV_SPAN, D_MODEL), BF16),
                        pltpu.VMEM((SUBLANES, CONV_SPAN, D_MODEL), F32),
                        pltpu.VMEM((CONV_BLOCK, D_MODEL), F32),
                        pltpu.VMEM((CONV_BLOCK, D_MODEL), BF16)]
        + [pltpu.VMEM(chunk(w), F32) for w in experts]
        + [pltpu.VMEM(chunk(w), BF16) for w in experts]
        + [pltpu.SemaphoreType.DMA((2,)), pltpu.SemaphoreType.DMA((2,))],
        compiler_params=_params("arbitrary", "arbitrary"),
        name="conv_module",
    )(h, h, g, w_pw1, w_pw1, shift, wdw, bdw, lg, lb, w2, *experts)


SORT_TOKENS = 512
GROUP = 16
SORT_CAP = 2 * SORT_TOKENS + N_EXPERTS * GROUP
SORT_COLS = D_MODEL + LANES
MOE_ROWS = 512
GROUPS_PER_TILE = MOE_ROWS // GROUP


def _one_hot_rows(pos1, pos2, width):
    r = lax.broadcasted_iota(jnp.int32, (pos1.shape[0], width), 1).astype(F32)
    return jnp.where(r == pos1, 1.0, jnp.where(r == pos2, 1.0, 0.0)).astype(BF16)


SORT_TILES_PER_STEP = 2
SPARE_ZERO = 0
SPARE_SINK = 64
assert SPARE_SINK + 2 * MOE_ROWS <= SORT_CAP


def _route_sort_kernel(h_ref, g_ref, wr_hi_ref, wr_lo_ref, xs_ref, pos_ref, ng_ref):
    is_spare = pl.program_id(0) == pl.num_programs(0) - 1

    @pl.when(is_spare)
    def _():
        xs_ref[...] = jnp.zeros_like(xs_ref)

    @pl.when(jnp.logical_not(is_spare))
    def _():
        tiles = [_route_sort_tile(h_ref.at[pl.ds(t * SORT_TOKENS, SORT_TOKENS)], g_ref,
                                  wr_hi_ref, wr_lo_ref,
                                  xs_ref.at[pl.ds(t * SORT_CAP, SORT_CAP)],
                                  pos_ref.at[pl.ds(t * SORT_TOKENS, SORT_TOKENS)], ng_ref.at[t])
                 for t in range(h_ref.shape[0] // SORT_TOKENS)]
        while tiles:
            tiles = [t for t in tiles if next(t, "done") != "done"]


def _route_sort_tile(h_ref, g_ref, wr_hi_ref, wr_lo_ref, xs_ref, pos_ref, ng_ref):
    s = SORT_TOKENS
    xn = _rms(h_ref[...], g_ref[...])
    x_hi = xn.astype(BF16)
    x_lo = (xn - x_hi.astype(F32)).astype(BF16)
    logits = (_dot(x_hi, wr_hi_ref[...]) + _dot(x_lo, wr_hi_ref[...])
              + _dot(x_hi, wr_lo_ref[...]))
    yield
    lane = lax.broadcasted_iota(jnp.int32, logits.shape, 1)
    logits = jnp.where(lane < N_EXPERTS, logits, -jnp.inf)
    m1 = jnp.max(logits, axis=-1, keepdims=True)
    i1 = jnp.min(jnp.where(logits == m1, lane, LANES), axis=-1, keepdims=True)
    rest = jnp.where(lane == i1, -jnp.inf, logits)
    m2 = jnp.max(rest, axis=-1, keepdims=True)
    i2 = jnp.min(jnp.where(rest == m2, lane, LANES), axis=-1, keepdims=True)
    e2 = jnp.exp(m2 - m1)
    g1 = 1.0 / (1.0 + e2)
    g2 = e2 / (1.0 + e2)
    yield

    oh1 = lane == i1
    oh2 = lane == i2
    oh = jnp.where(oh1, 1.0, jnp.where(oh2, 1.0, 0.0))
    cnt = jnp.sum(oh, axis=0, keepdims=True)
    ngr = jnp.floor((cnt + (GROUP - 1)) * (1.0 / GROUP))
    ri = lax.broadcasted_iota(jnp.int32, (s, s), 0)
    ci = lax.broadcasted_iota(jnp.int32, (s, s), 1)
    rank = _dot(jnp.where(ri > ci, 1.0, 0.0).astype(BF16), oh.astype(BF16))
    li = lax.broadcasted_iota(jnp.int32, (LANES, LANES), 0)
    lj = lax.broadcasted_iota(jnp.int32, (LANES, LANES), 1)
    before = _dot(jnp.broadcast_to(ngr, (8, LANES)).astype(BF16),
                  jnp.where(li < lj, 1.0, 0.0).astype(BF16))[0:1]
    yield
    base = before * GROUP + rank
    base1 = jnp.where(oh1, base, 0.0)
    base2 = jnp.where(oh2, base, 0.0)
    pos1 = jnp.sum(base1, axis=-1, keepdims=True)
    pos2 = jnp.sum(base2, axis=-1, keepdims=True)

    def as_row(v):
        ones = jnp.ones((8, LANES), BF16)
        hi = jnp.floor(v * (1.0 / 256.0))
        return (_dot_nt(ones, hi.astype(BF16)) * 256.0
                + _dot_nt(ones, (v - hi * 256.0).astype(BF16)))[0:1]

    row1, row2 = as_row(base1), as_row(base2)
    yield
    r = lax.broadcasted_iota(jnp.int32, (SORT_CAP, s), 0).astype(F32)
    p1 = jnp.where(r == row1, 1.0, 0.0).astype(BF16)
    p2 = jnp.where(r == row2, 1.0, 0.0).astype(BF16)
    xs_ref[:, 0:D_MODEL] = _dot(p1 + p2, x_hi).astype(BF16)

    def gate_lanes(gv):
        hi = gv.astype(BF16).astype(F32)
        return jnp.where(lane == 0, hi, jnp.where(lane == 1, gv - hi, 0.0)).astype(BF16)

    xs_ref[:, D_MODEL:SORT_COLS] = (_dot(p1, gate_lanes(g1))
                                    + _dot(p2, gate_lanes(g2))).astype(BF16)
    pos_ref[...] = jnp.where(lane == 0, pos1, jnp.where(lane == 1, pos2, 0.0))
    ng_ref[...] = jnp.broadcast_to(ngr, (8, LANES)).astype(jnp.int32)


def _route_sort(h, g, wr_hi, wr_lo):
    tokens = h.shape[0]
    nt = tokens // SORT_TOKENS
    tps = math.gcd(nt, SORT_TILES_PER_STEP)
    steps = nt // tps
    last = lambda i: jnp.minimum(i, steps - 1)
    return pl.pallas_call(
        _route_sort_kernel,
        out_shape=(jax.ShapeDtypeStruct(((nt + tps) * SORT_CAP, SORT_COLS), BF16),
                   jax.ShapeDtypeStruct((tokens, LANES), F32),
                   jax.ShapeDtypeStruct((nt, 8, LANES), jnp.int32)),
        grid=(steps + 1,),
        in_specs=[pl.BlockSpec((tps * SORT_TOKENS, D_MODEL), lambda i: (last(i), 0)),
                  pl.BlockSpec((1, D_MODEL), lambda i: (0, 0)),
                  pl.BlockSpec((D_MODEL, LANES), lambda i: (0, 0)),
                  pl.BlockSpec((D_MODEL, LANES), lambda i: (0, 0))],
        out_specs=(pl.BlockSpec((tps * SORT_CAP, SORT_COLS), lambda i: (i, 0)),
                   pl.BlockSpec((tps * SORT_TOKENS, LANES), lambda i: (last(i), 0)),
                   pl.BlockSpec((tps, 8, LANES), lambda i: (last(i), 0, 0))),
        compiler_params=_params("arbitrary"),
        name="route_sort",
    )(h, g, wr_hi, wr_lo)


def _group_tables(ng, n_row_tiles):
    nt, ne = ng.shape
    gpt = GROUPS_PER_TILE
    n_e = jnp.sum(ng, axis=0)
    rt_e = (n_e + gpt - 1) // gpt
    rt_end = jnp.cumsum(rt_e)
    n_rt = rt_end[-1]
    r = jnp.arange(n_row_tiles, dtype=jnp.int32)
    valid_r = r < n_rt
    e_r = jnp.minimum(jnp.sum((r[:, None] >= rt_end[None, :]).astype(jnp.int32), axis=1), ne - 1)
    e_last = jnp.sum(jnp.where(r == n_rt - 1, e_r, 0))
    e_r = jnp.where(valid_r, e_r, e_last)
    sel = (e_r[:, None] == jnp.arange(ne, dtype=jnp.int32)[None, :]).astype(jnp.int32)
    pick = lambda v: jnp.sum(sel * v[None, :], axis=1)
    k0 = (r - pick(rt_end - rt_e)) * gpt
    gidx = k0[:, None] + jnp.arange(gpt, dtype=jnp.int32)[None, :]
    valid = valid_r[:, None] & (gidx < pick(n_e)[:, None])
    incl = jnp.cumsum(ng, axis=0)
    incl_r = jnp.sum(sel[:, None, :] * incl[None, :, :], axis=2)
    ng_r = jnp.sum(sel[:, None, :] * ng[None, :, :], axis=2)
    lo = jnp.cumsum(ng, axis=1) - ng
    lo_r = jnp.sum(sel[:, None, :] * lo[None, :, :], axis=2)
    ti = jnp.sum((incl_r[:, None, :] <= gidx[:, :, None]).astype(jnp.int32), axis=2)
    ti = jnp.minimum(ti, nt - 1)
    tsel = (ti[:, :, None] == jnp.arange(nt, dtype=jnp.int32)[None, None, :]).astype(jnp.int32)
    tpick = lambda v: jnp.sum(tsel * v[:, None, :], axis=2)
    within = gidx - tpick(incl_r - ng_r)
    rows = ti * SORT_CAP + GROUP * (tpick(lo_r) + within)
    spare = nt * SORT_CAP
    sink = (spare + SPARE_SINK + (r % 2)[:, None] * MOE_ROWS
            + GROUP * jnp.arange(gpt, dtype=jnp.int32)[None, :])
    src = jnp.where(valid, rows, spare + SPARE_ZERO).reshape(-1)
    dst = jnp.where(valid, rows, sink).reshape(-1)
    return e_r, jnp.sum(valid.astype(jnp.int32), axis=1), src, dst


def _group_copies(rows_ref, tile, hbm, buf, slot, sem, to_hbm, start):
    for k in range(GROUPS_PER_TILE):
        off = pl.multiple_of(rows_ref[tile * GROUPS_PER_TILE + k], GROUP)
        buf_rows = buf.at[slot, pl.ds(k * GROUP, GROUP), :]
        hbm_rows = hbm.at[pl.ds(off, GROUP), pl.ds(0, buf_rows.shape[-1])]
        cp = (pltpu.make_async_copy(buf_rows, hbm_rows, sem.at[slot]) if to_hbm
              else pltpu.make_async_copy(hbm_rows, buf_rows, sem.at[slot]))
        if start:
            cp.start()
        else:
            cp.wait()


def _moe_group_kernel(te_ref, tv_ref, src_ref, dst_ref, xs_hbm, w1_ref, w3_ref, w2_ref, ys_hbm,
                      xbuf, ybuf, acc_sc, xsem, ysem):
    r = pl.program_id(0)
    j = pl.program_id(1)
    nr = pl.num_programs(0)
    nj = pl.num_programs(1)
    slot = lax.rem(r, 2)
    valid = lambda t: tv_ref[jnp.clip(t, 0, nr - 1)] > 0
    gather = functools.partial(_group_copies, src_ref, hbm=xs_hbm, buf=xbuf, sem=xsem,
                               to_hbm=False)
    scatter = functools.partial(_group_copies, dst_ref, hbm=ys_hbm, buf=ybuf, sem=ysem,
                                to_hbm=True)

    @pl.when(j == 0)
    def _():
        @pl.when((r == 0) & valid(r))
        def _():
            gather(tile=r, slot=slot, start=True)

        @pl.when(valid(r))
        def _():
            gather(tile=r, slot=slot, start=False)

        @pl.when((r + 1 < nr) & valid(r + 1))
        def _():
            gather(tile=r + 1, slot=1 - slot, start=True)

        @pl.when((r >= 2) & valid(r - 2))
        def _():
            scatter(tile=r - 2, slot=slot, start=False)

    def compute(rows):
        x = xbuf[slot, rows, 0:D_MODEL]
        a = _dot(x, w1_ref[0])
        b = _dot(x, w3_ref[0])
        hid = (a * jax.nn.sigmoid(a) * b).astype(BF16)
        part = _dot(hid, w2_ref[0])

        @pl.when(j == 0)
        def _():
            acc_sc[rows, :] = part

        @pl.when((j > 0) & (j < nj - 1))
        def _():
            acc_sc[rows, :] += part

        @pl.when(j == nj - 1)
        def _():
            gl = xbuf[slot, rows, D_MODEL:SORT_COLS].astype(F32)
            gate = gl[:, 0:1] + gl[:, 1:2]
            ybuf[slot, rows, :] = ((acc_sc[rows, :] + part) * gate).astype(BF16)

    half = MOE_ROWS // 2
    full = tv_ref[r] > GROUPS_PER_TILE // 2

    @pl.when(full)
    def _():
        compute(slice(0, MOE_ROWS))

    @pl.when(valid(r) & jnp.logical_not(full))
    def _():
        compute(slice(0, half))

        @pl.when(j == nj - 1)
        def _():
            ybuf[slot, half:MOE_ROWS, :] = jnp.zeros((half, D_MODEL), BF16)

    @pl.when(valid(r) & (j == nj - 1))
    def _():
        scatter(tile=r, slot=slot, start=True)

    @pl.when((r == nr - 1) & (j == nj - 1))
    def _():
        @pl.when((r >= 1) & valid(r - 1))
        def _():
            scatter(tile=r - 1, slot=1 - slot, start=False)

        @pl.when(valid(r))
        def _():
            scatter(tile=r, slot=slot, start=False)


def _moe_group(xs, tile_expert, tile_valid, src, dst, w1, w3, w2, n_row_tiles, tf):
    ne, _, dff = w1.shape
    nj = dff // tf
    assert nj >= 2

    def w_in(r, j, te, tv, s, d):
        return (te[r], 0, jnp.where(tv[r] > 0, j, nj - 1))

    def w_out(r, j, te, tv, s, d):
        return (te[r], jnp.where(tv[r] > 0, j, nj - 1), 0)

    grid_spec = pltpu.PrefetchScalarGridSpec(
        num_scalar_prefetch=4,
        grid=(n_row_tiles, nj),
        in_specs=[pl.BlockSpec(memory_space=pl.ANY),
                  pl.BlockSpec((1, D_MODEL, tf), w_in),
                  pl.BlockSpec((1, D_MODEL, tf), w_in),
                  pl.BlockSpec((1, tf, D_MODEL), w_out)],
        out_specs=pl.BlockSpec(memory_space=pl.ANY),
        scratch_shapes=[pltpu.VMEM((2, MOE_ROWS, SORT_COLS), BF16),
                        pltpu.VMEM((2, MOE_ROWS, D_MODEL), BF16),
                        pltpu.VMEM((MOE_ROWS, D_MODEL), F32),
                        pltpu.SemaphoreType.DMA((2,)),
                        pltpu.SemaphoreType.DMA((2,))],
    )
    return pl.pallas_call(
        _moe_group_kernel,
        out_shape=jax.ShapeDtypeStruct(xs.shape, xs.dtype),
        grid_spec=grid_spec,
        input_output_aliases={4: 0},
        compiler_params=_params("arbitrary", "arbitrary"),
        name="moe_group",
    )(tile_expert, tile_valid, src, dst, xs, w1, w3, w2)


def _combine_kernel(h_ref, pos_ref, ys_ref, out_ref):
    pos = pos_ref[...]
    pt = _one_hot_rows(pos[:, 0:1], pos[:, 1:2], SORT_CAP)
    out_ref[...] = h_ref[...] + _dot(pt, ys_ref[...])


def _combine(h, pos, ys):
    tokens = h.shape[0]
    return pl.pallas_call(
        _combine_kernel,
        out_shape=jax.ShapeDtypeStruct((tokens, D_MODEL), F32),
        grid=(tokens // SORT_TOKENS,),
        in_specs=[pl.BlockSpec((SORT_TOKENS, D_MODEL), lambda i: (i, 0)),
                  pl.BlockSpec((SORT_TOKENS, LANES), lambda i: (i, 0)),
                  pl.BlockSpec((SORT_CAP, D_MODEL), lambda i: (i, 0))],
        out_specs=pl.BlockSpec((SORT_TOKENS, D_MODEL), lambda i: (i, 0)),
        compiler_params=_params("parallel"),
        name="combine",
    )(h, pos, ys)


def _row_tile(rows, want):
    tm = want
    while rows % tm:
        tm //= 2
    return tm


def _rope_tables(lp):
    inv_freq = 1.0 / (ROPE_THETA ** (jnp.arange(0, SWA_HEAD_DIM, 2, dtype=F32) / SWA_HEAD_DIM))
    pos = jnp.arange(lp, dtype=F32) - FRONT
    ang = pos[:, None] * inv_freq[None, :]
    cos, sin = jnp.cos(ang), jnp.sin(ang)
    zero = jnp.zeros_like(sin)
    rep = LANES // SWA_HEAD_DIM
    cos_t = jnp.tile(jnp.concatenate([cos, cos], axis=-1), (1, rep))
    sin_a = jnp.tile(jnp.concatenate([zero, sin], axis=-1), (1, rep))
    sin_b = jnp.tile(jnp.concatenate([-sin, zero], axis=-1), (1, rep))
    return cos_t, sin_a, sin_b


def kernel(x, meta, a_norm, a_w_in, a_w_gate2, a_b_gate, a_q_norm, a_k_norm, a_sinks, a_o_norm,
           a_w_out, f_norm, f_w1, f_w3, f_w2, c_norm, c_w_pw1, c_w_dw, c_b_dw, c_ln_g, c_ln_b,
           c_w_pw2, m_norm, m_w_router, m_w1, m_w3, m_w2):
    batch, seq, _ = x.shape
    assert seq % BLOCK == 0
    lp = BLOCK + seq
    nb = lp // BLOCK
    rows = batch * lp

    head = jnp.concatenate([jnp.zeros((FRONT, D_MODEL), x.dtype), meta.astype(x.dtype)],
                           axis=0)[None]
    tm = _row_tile(seq, 512)

    w_in = a_w_in[0]
    gq, gk, gv, gr, glr, sq, sk, sv = jnp.split(
        w_in, [256, 512, 1024, 1536, 1552, 2064, 2192], axis=-1)
    w_in_r = jnp.concatenate(
        [gq, gk, gv, gr, sq, sk, sv, glr,
         jnp.zeros((D_MODEL, Z_COLS - Z_LR - GLA_LOWRANK), w_in.dtype)], axis=-1).astype(BF16)
    z = _in_proj(x, a_norm[0][None], w_in_r, batch, lp, tm, 0)
    z = _in_proj(head, a_norm[0][None], w_in_r, batch, lp, BLOCK, seq, z)

    wg = jnp.zeros((LANES, GLA_QK), F32).at[:GLA_LOWRANK].set(a_w_gate2[0]).astype(BF16)
    bb = math.gcd(batch, MIX_BATCH)
    o_a = _gla(z, wg, a_b_gate[0][None], a_o_norm[0][None], bb)

    cos_t, sin_a, sin_b = _rope_tables(lp)
    qn = jnp.tile(a_q_norm[0], SWA_Q_HEADS)[None]
    kn = jnp.tile(a_k_norm[0], SWA_KV_HEADS)[None]
    lane = jnp.arange(LANES)
    bd = (lane[:, None] // SWA_HEAD_DIM == lane[None, :] // SWA_HEAD_DIM).astype(BF16)
    o_b, d_w1, d_w3, d_w2 = _swa(z, a_sinks[0], cos_t, sin_a, sin_b, qn, kn, bd,
                                 (f_w1[0], f_w3[0], f_w2[0]), bb)

    w_out = a_w_out[0].astype(BF16)
    ffn = (w_out[:GLA_V], w_out[GLA_V:], f_norm[0][None], d_w1, d_w3, d_w2)
    h = _mix_out_ffn(x, o_a, o_b, *ffn, tm, FF_TILE, 0)
    h = _mix_out_ffn(head, o_a, o_b, *ffn, BLOCK, FF_TILE, seq, h)

    si = jnp.arange((SUBLANES - 1) * CONV_SPAN)
    shift = (jnp.arange(CONV_SPAN)[None, :]
             == (si % CONV_SPAN + si // CONV_SPAN + 1)[:, None]).astype(BF16)
    ne, _, dff = m_w1[0].shape
    experts = (m_w1[0].reshape(ne * D_MODEL, dff), m_w3[0].reshape(ne * D_MODEL, dff),
               m_w2[0].reshape(ne * dff, D_MODEL))
    h, e_w1, e_w3, e_w2 = _conv_module(
        h, c_norm[0][None], c_w_pw1[0].astype(BF16), shift,
        jnp.broadcast_to(c_w_dw[0][:, None, :], (CONV_WIDTH, SUBLANES, D_MODEL)),
        c_b_dw[0][None], c_ln_g[0][None], c_ln_b[0][None], c_w_pw2[0].astype(BF16), experts)
    h = h.reshape(batch * seq, D_MODEL)

    wr = jnp.zeros((D_MODEL, LANES), F32).at[:, :N_EXPERTS].set(m_w_router[0])
    wr_hi = wr.astype(BF16)
    wr_lo = (wr - wr_hi.astype(F32)).astype(BF16)
    tokens = batch * seq
    assert tokens % SORT_TOKENS == 0
    nt = tokens // SORT_TOKENS
    xs, pos, ng = _route_sort(h, m_norm[0][None], wr_hi, wr_lo)
    ng = ng[:, 0, :N_EXPERTS]
    n_row_tiles = (nt * (SORT_CAP // GROUP)) // GROUPS_PER_TILE + N_EXPERTS
    tile_expert, tile_valid, src, dst = _group_tables(ng, n_row_tiles)
    ys = _moe_group(xs, tile_expert, tile_valid, src, dst, e_w1.reshape(ne, D_MODEL, dff),
                    e_w3.reshape(ne, D_MODEL, dff), e_w2.reshape(ne, dff, D_MODEL), n_row_tiles,
                    FF_TILE)
    out = _combine(h, pos, ys)
    return out.reshape(batch, seq, D_MODEL)
```

```python
import functools
import math

import jax
import jax.numpy as jnp
from jax import lax
from jax.experimental import pallas as pl
from jax.experimental.pallas import tpu as pltpu

F32 = jnp.float32
BF16 = jnp.bfloat16

D_MODEL = 1024
N_META = 16
EPS = 1e-6
ROPE_THETA = 10000.0
NEG_INF = -1e30
LOG2E = math.log2(math.e)
GLA_HEADS = 4
GLA_DK = 64
GLA_DV = 128
GLA_LOWRANK = 16
GLA_TAU = 16.0
GLA_CHUNK = 64
SWA_Q_HEADS = 8
SWA_KV_HEADS = 2
SWA_HEAD_DIM = 64
SWA_WINDOW = 128
CONV_WIDTH = 31
D_FF = 3584
N_EXPERTS = 8

GLA_QK = GLA_HEADS * GLA_DK
GLA_V = GLA_HEADS * GLA_DV
SWA_Q = SWA_Q_HEADS * SWA_HEAD_DIM
SWA_KV = SWA_KV_HEADS * SWA_HEAD_DIM
SWA_GROUP = SWA_Q_HEADS // SWA_KV_HEADS

LANES = 128
FF_TILE = 1792
BLOCK = 128
FRONT = BLOCK - N_META
VMEM_LIMIT = 56 * 1024 * 1024

Z_Q, Z_K, Z_V, Z_R = 0, 256, 512, 1024
Z_SQ, Z_SK, Z_SV, Z_LR = 1536, 2048, 2176, 2304
Z_COLS = 2432


def _rms(x, g):
    return x * lax.rsqrt(jnp.mean(x * x, axis=-1, keepdims=True) + EPS) * g


def _dot(a, b):
    return jnp.dot(a, b, preferred_element_type=F32)


def _dot_nt(a, b):
    return lax.dot_general(a, b, (((1,), (1,)), ((), ())), preferred_element_type=F32)


def _dot_tn(a, b):
    return lax.dot_general(a, b, (((0,), (0,)), ((), ())), preferred_element_type=F32)


def _stored_block(n, lp):
    return jnp.where(n == 0, lp // BLOCK - 1, n - 1)


def _params(*sem):
    return pltpu.CompilerParams(dimension_semantics=sem, vmem_limit_bytes=VMEM_LIMIT)


IN_PROJ_ROWS = 256


def _in_proj_kernel(h_ref, g_ref, w_ref, cos_ref, sa_ref, sb_ref, qn_ref, kn_ref, bd_ref,
                    *rest):
    z_ref = rest[-1]
    tile = lambda t: jnp.concatenate([t] * (SWA_Q // LANES), axis=-1)
    tm = h_ref.shape[0]
    chunks = [slice(r, min(r + IN_PROJ_ROWS, tm)) for r in range(0, tm, IN_PROJ_ROWS)]
    project = lambda rs: _dot(_rms(h_ref[rs, :], g_ref[...]).astype(BF16), w_ref[...])
    zs = {0: project(chunks[0])}
    for i, rs in enumerate(chunks):
        if i + 1 < len(chunks):
            zs[i + 1] = project(chunks[i + 1])
        z = zs.pop(i)
        cos, sa, sb = cos_ref[rs, :], sa_ref[rs, :], sb_ref[rs, :]
        q = _norm_rope(z[None, :, Z_SQ:Z_SQ + SWA_Q], qn_ref[...], tile(cos), tile(sa),
                       tile(sb), bd_ref[...])[0] * (SWA_HEAD_DIM ** -0.5 * LOG2E)
        k = _norm_rope(z[None, :, Z_SK:Z_SK + SWA_KV], kn_ref[...], cos, sa, sb, bd_ref[...])[0]
        z = jnp.concatenate([z[:, :Z_SQ], q, k, z[:, Z_SV:]], axis=-1).astype(z_ref.dtype)
        if z_ref.ndim == 2:
            z_ref[rs, :] = z
        else:
            z_ref[:, rs, :] = jnp.broadcast_to(z[None], (z_ref.shape[0],) + z.shape)


def _in_proj(h, g, w, rope, batch, lp, tm, first_row, z=None):
    hb, rows, _ = h.shape
    n = w.shape[1]
    shared = hb == 1 and batch > 1
    const = lambda a: pl.BlockSpec(a.shape, lambda b, j: (0,) * a.ndim)
    table = pl.BlockSpec((tm, LANES), lambda b, j: (first_row // tm + j, 0))
    in_specs = [pl.BlockSpec((None, tm, D_MODEL), lambda b, j: (b, j, 0)),
                pl.BlockSpec((1, D_MODEL), lambda b, j: (0, 0)),
                pl.BlockSpec((D_MODEL, n), lambda b, j: (0, 0)),
                table, table, table, const(rope[3]), const(rope[4]), const(rope[5])]
    args = (h, g, w) + tuple(rope)
    if z is not None:
        in_specs.append(pl.BlockSpec(memory_space=pl.ANY))
        args += (z,)
    return pl.pallas_call(
        _in_proj_kernel,
        out_shape=jax.ShapeDtypeStruct((batch, lp, n), BF16),
        grid=(hb, rows // tm),
        in_specs=in_specs,
        out_specs=pl.BlockSpec((batch if shared else None, tm, n),
                               lambda b, j: (b, first_row // tm + j, 0)),
        input_output_aliases={} if z is None else {len(args) - 1: 0},
        compiler_params=_params("parallel", "parallel"),
        name="in_proj",
    )(*args)


MIX_BATCH = 8


def _gla_kernel(q_ref, k_ref, v_ref, r_ref, lr_ref, wg_ref, bg_ref, on_ref, o_ref, st_ref):
    n = pl.program_id(1)

    @pl.when(n == 0)
    def _():
        st_ref[...] = jnp.zeros_like(st_ref)

    c = GLA_CHUNK
    items = [(bi, ch) for bi in range(q_ref.shape[0]) for ch in range(BLOCK // c)]
    ri = lax.broadcasted_iota(jnp.int32, (BLOCK, BLOCK), 0)
    ci = lax.broadcasted_iota(jnp.int32, (BLOCK, BLOCK), 1)
    tri = jnp.where((ri >= ci) & (ri // c == ci // c), 1.0, 0.0).astype(BF16)
    causal = (lax.broadcasted_iota(jnp.int32, (c, c), 0)
              >= lax.broadcasted_iota(jnp.int32, (c, c), 1))
    row = lax.broadcasted_iota(jnp.int32, (BLOCK, GLA_QK), 0) + n * BLOCK
    on = on_ref[...]

    b_all = []
    for bi in range(q_ref.shape[0]):
        gp = _dot(lr_ref[bi], wg_ref[...]) + bg_ref[...]
        g = (jnp.minimum(gp, 0.0) - jnp.log(1.0 + jnp.exp(-jnp.abs(gp)))) * (1.0 / GLA_TAU)
        g = jnp.where(row >= FRONT, g, 0.0)
        g_hi = g.astype(BF16)
        g_lo = (g - g_hi.astype(F32)).astype(BF16)
        b_all.append(_dot(tri, g_hi) + _dot(tri, g_lo))

    q_t, k_t, k_d, dec = {}, {}, {}, {}
    for bi, ch in items:
        rs = slice(ch * c, (ch + 1) * c)
        b = b_all[bi][rs]
        b_last = b_all[bi][(ch + 1) * c - 1:(ch + 1) * c, :]
        qf = q_ref[bi, rs, :].astype(F32)
        kf = k_ref[bi, rs, :].astype(F32)
        q_t[bi, ch] = (qf * jnp.exp(b) * (GLA_DK ** -0.5)).astype(BF16)
        k_t[bi, ch] = (kf * jnp.exp(-b)).astype(BF16)
        k_d[bi, ch] = (kf * jnp.exp(b_last - b)).astype(BF16)
        dec[bi, ch] = jnp.exp(b_last)

    att, kv = {}, {}
    for bi, ch in items:
        rs = slice(ch * c, (ch + 1) * c)
        for hd in range(GLA_HEADS):
            ks = slice(hd * GLA_DK, (hd + 1) * GLA_DK)
            vh = v_ref[bi, rs, hd * GLA_DV:(hd + 1) * GLA_DV]
            att[bi, ch, hd] = jnp.where(
                causal, _dot_nt(q_t[bi, ch][:, ks], k_t[bi, ch][:, ks]), 0.0).astype(BF16)
            kv[bi, ch, hd] = _dot_tn(vh, k_d[bi, ch][:, ks])

    for bi in range(q_ref.shape[0]):
        for hd in range(GLA_HEADS):
            ks = slice(hd * GLA_DK, (hd + 1) * GLA_DK)
            vs = slice(hd * GLA_DV, (hd + 1) * GLA_DV)
            st = st_ref[bi, hd]
            for ch in range(BLOCK // c):
                rs = slice(ch * c, (ch + 1) * c)
                o = (_dot(att[bi, ch, hd], v_ref[bi, rs, vs])
                     + _dot_nt(q_t[bi, ch][:, ks], st.astype(BF16)))
                st = st * dec[bi, ch][:, ks] + kv[bi, ch, hd]
                o = _rms(o, on)
                r = r_ref[bi, rs, vs].astype(F32)
                o_ref[bi, rs, vs] = (o * (r * jax.nn.sigmoid(r))).astype(o_ref.dtype)
            st_ref[bi, hd] = st


def _gla(z, wg, bg, on, bb):
    batch, lp, _ = z.shape

    def col(width, start):
        return pl.BlockSpec((bb, BLOCK, width), lambda b, n: (b, _stored_block(n, lp), start // width))

    const = lambda shape: pl.BlockSpec(shape, lambda b, n: (0,) * len(shape))
    return pl.pallas_call(
        _gla_kernel,
        out_shape=jax.ShapeDtypeStruct((batch, lp, GLA_V), BF16),
        grid=(batch // bb, lp // BLOCK),
        in_specs=[col(GLA_QK, Z_Q), col(GLA_QK, Z_K), col(GLA_V, Z_V), col(GLA_V, Z_R),
                  col(LANES, Z_LR), const((LANES, GLA_QK)), const((1, GLA_QK)),
                  const((1, GLA_DV))],
        out_specs=pl.BlockSpec((bb, BLOCK, GLA_V), lambda b, n: (b, _stored_block(n, lp), 0)),
        scratch_shapes=[pltpu.VMEM((bb, GLA_HEADS, GLA_DV, GLA_DK), F32)],
        compiler_params=_params("parallel", "arbitrary"),
        name="gla",
    )(z, z, z, z, z, wg, bg, on)


def _group_mean_sq(x, bd):
    sq = x * x
    hi = sq.astype(BF16)
    lo = (sq - hi.astype(F32)).astype(BF16)
    return _dot(hi, bd) + _dot(lo, bd)


def _norm_rope(x, gain, cos, sin_a, sin_b, bd):
    nbat, rows, width = x.shape
    x2 = x.reshape(nbat * rows, width)
    parts = []
    for s in range(width // LANES):
        xs = x2[:, s * LANES:(s + 1) * LANES]
        ms = _group_mean_sq(xs, bd) * (1.0 / SWA_HEAD_DIM)
        parts.append(xs * lax.rsqrt(ms + EPS))
    xn = (parts[0] if len(parts) == 1 else jnp.concatenate(parts, axis=-1)) * gain
    half = SWA_HEAD_DIM // 2
    back = lambda t: t.reshape(nbat, rows, width)
    return (back(xn) * cos + back(pltpu.roll(xn, half, 1)) * sin_a
            + back(pltpu.roll(xn, width - half, 1)) * sin_b)


def _swa_kernel(sinks_ref, q_ref, kc_ref, kp_ref, vc_ref, vp_ref, f1_ref, f3_ref, f2_ref,
                o_ref, f1b_ref, f3b_ref, f2b_ref, i1_sc, i3_sc, i2_sc, o1_sc, o3_sc, o2_sc,
                in_sem, out_sem, *, cast_steps):
    n = pl.program_id(1)
    nbat = q_ref.shape[0]

    cast_step = pl.program_id(0) * pl.num_programs(1) + n

    @pl.when(cast_step < cast_steps)
    def _():
        _cast_stream(cast_step, cast_steps, (f1_ref, f3_ref, f2_ref),
                     (f1b_ref, f3b_ref, f2b_ref), (i1_sc, i3_sc, i2_sc), (o1_sc, o3_sc, o2_sc),
                     in_sem, out_sem)

    q = q_ref[...]
    k_c = kc_ref[...]
    k_p = kp_ref[...]

    gq = SWA_GROUP * BLOCK
    r = lax.broadcasted_iota(jnp.int32, (gq, 2 * BLOCK), 0) & (BLOCK - 1)
    cidx = lax.broadcasted_iota(jnp.int32, (gq, 2 * BLOCK), 1)
    allowed = (cidx > r) & (cidx <= r + SWA_WINDOW) & ((n - 1) * BLOCK + cidx >= FRONT)
    rgroup = lax.broadcasted_iota(jnp.int32, (gq, 1), 0) // BLOCK
    items = [(bi, hk) for bi in range(nbat) for hk in range(SWA_KV_HEADS)]

    s, sink = {}, {}
    for hk in range(SWA_KV_HEADS):
        sk = jnp.zeros((gq, 1), F32)
        for g in range(SWA_GROUP):
            sk = jnp.where(rgroup == g, sinks_ref[hk * SWA_GROUP + g] * LOG2E, sk)
        sink[hk] = sk
    for bi, hk in items:
        ks = slice(hk * SWA_HEAD_DIM, (hk + 1) * SWA_HEAD_DIM)
        qg = jnp.concatenate(
            [q[bi][:, (hk * SWA_GROUP + g) * SWA_HEAD_DIM:(hk * SWA_GROUP + g + 1) * SWA_HEAD_DIM]
             for g in range(SWA_GROUP)], axis=0)
        k = jnp.concatenate([k_p[bi][:, ks], k_c[bi][:, ks]], axis=0)
        s[bi, hk] = jnp.where(allowed, _dot_nt(qg, k), NEG_INF)

    p, denom = {}, {}
    for bi, hk in items:
        m = jnp.maximum(jnp.max(s[bi, hk], axis=-1, keepdims=True), sink[hk])
        e = jnp.exp2(s[bi, hk] - m)
        denom[bi, hk] = jnp.sum(e, axis=-1, keepdims=True) + jnp.exp2(sink[hk] - m)
        p[bi, hk] = e.astype(BF16)

    for bi, hk in items:
        ks = slice(hk * SWA_HEAD_DIM, (hk + 1) * SWA_HEAD_DIM)
        v = jnp.concatenate([vp_ref[bi, :, ks], vc_ref[bi, :, ks]], axis=0)
        o = _dot(p[bi, hk], v) / denom[bi, hk]
        for g in range(SWA_GROUP):
            hq = hk * SWA_GROUP + g
            o_ref[bi, :, hq * SWA_HEAD_DIM:(hq + 1) * SWA_HEAD_DIM] = (
                o[g * BLOCK:(g + 1) * BLOCK].astype(o_ref.dtype))


def _swa(z, sinks, ffn, bb):
    batch, lp, _ = z.shape
    steps = (batch // bb) * (lp // BLOCK)
    cast_steps = 1 << (steps.bit_length() - 1)
    assert all(w.shape[0] % (cast_steps * GROUP) == 0 for w in ffn)
    chunk = lambda w: (2, w.shape[0] // cast_steps, w.shape[1])
    any_spec = pl.BlockSpec(memory_space=pl.ANY)

    def cur(width, start):
        return pl.BlockSpec((bb, BLOCK, width),
                            lambda b, n, s: (b, _stored_block(n, lp), start // width))

    def prev(width, start):
        return pl.BlockSpec((bb, BLOCK, width),
                            lambda b, n, s: (b, _stored_block(jnp.maximum(n - 1, 0), lp),
                                             start // width))

    grid_spec = pltpu.PrefetchScalarGridSpec(
        num_scalar_prefetch=1,
        grid=(batch // bb, lp // BLOCK),
        in_specs=[cur(SWA_Q, Z_SQ), cur(SWA_KV, Z_SK), prev(SWA_KV, Z_SK),
                  cur(SWA_KV, Z_SV), prev(SWA_KV, Z_SV),
                  any_spec, any_spec, any_spec],
        out_specs=(pl.BlockSpec((bb, BLOCK, SWA_Q),
                                lambda b, n, s: (b, _stored_block(n, lp), 0)),
                   any_spec, any_spec, any_spec),
        scratch_shapes=[pltpu.VMEM(chunk(w), F32) for w in ffn]
        + [pltpu.VMEM(chunk(w), BF16) for w in ffn]
        + [pltpu.SemaphoreType.DMA((2,)), pltpu.SemaphoreType.DMA((2,))],
    )
    return pl.pallas_call(
        functools.partial(_swa_kernel, cast_steps=cast_steps),
        out_shape=(jax.ShapeDtypeStruct((batch, lp, SWA_Q), BF16),)
        + tuple(jax.ShapeDtypeStruct(w.shape, BF16) for w in ffn),
        grid_spec=grid_spec,
        compiler_params=_params("arbitrary", "arbitrary"),
        name="swa",
    )(sinks, z, z, z, z, z, *ffn)


def _ffn_kernel(h_ref, oa_ref, ob_ref, woa_ref, wob_ref, g_ref, w1_ref, w3_ref, w2_ref,
                *rest):
    out_ref, h1_sc, xn_sc, acc_sc = rest[-4:]
    j = pl.program_id(2)
    nj = pl.num_programs(2)

    @pl.when(j == 0)
    def _():
        h1 = h_ref[...] + _dot(oa_ref[...], woa_ref[...]) + _dot(ob_ref[...], wob_ref[...])
        h1_sc[...] = h1
        xn_sc[...] = _rms(h1, g_ref[...]).astype(BF16)

    x = xn_sc[...]
    a = _dot(x, w1_ref[...])
    b = _dot(x, w3_ref[...])
    hid = (a * jax.nn.sigmoid(a) * b).astype(BF16)
    part = _dot(hid, w2_ref[...])

    @pl.when(j == 0)
    def _():
        acc_sc[...] = h1_sc[...] + part

    @pl.when((j > 0) & (j < nj - 1))
    def _():
        acc_sc[...] += part

    @pl.when(j == nj - 1)
    def _():
        out_ref[...] = acc_sc[...] + part


def _mix_out_ffn(h, oa, ob, w_out, g, w1, w3, w2, tm, tf, first_row, out=None):
    hb, rows, _ = h.shape
    batch, lp, _ = oa.shape
    dff = w1.shape[1]
    row0 = first_row // tm
    const = lambda shape: pl.BlockSpec(shape, lambda b, i, j: (0, 0))
    in_specs = [pl.BlockSpec((None, tm, D_MODEL),
                             lambda b, i, j: (jnp.minimum(b, hb - 1), i, 0)),
                pl.BlockSpec((None, tm, GLA_V), lambda b, i, j: (b, row0 + i, 0)),
                pl.BlockSpec((None, tm, SWA_Q), lambda b, i, j: (b, row0 + i, 0)),
                pl.BlockSpec((GLA_V, D_MODEL), lambda b, i, j: (0, 0)),
                pl.BlockSpec((SWA_Q, D_MODEL), lambda b, i, j: (GLA_V // SWA_Q, 0)),
                const((1, D_MODEL)),
                pl.BlockSpec((D_MODEL, tf), lambda b, i, j: (0, j)),
                pl.BlockSpec((D_MODEL, tf), lambda b, i, j: (0, j)),
                pl.BlockSpec((tf, D_MODEL), lambda b, i, j: (j, 0))]
    assert GLA_V % SWA_Q == 0
    args = (h, oa, ob, w_out, w_out, g, w1, w3, w2)
    if out is not None:
        in_specs.append(pl.BlockSpec(memory_space=pl.ANY))
        args += (out,)
    return pl.pallas_call(
        _ffn_kernel,
        out_shape=jax.ShapeDtypeStruct((batch, lp, D_MODEL), F32),
        grid=(hb, rows // tm, dff // tf),
        in_specs=in_specs,
        out_specs=pl.BlockSpec((None, tm, D_MODEL), lambda b, i, j: (b, row0 + i, 0)),
        scratch_shapes=[pltpu.VMEM((tm, D_MODEL), F32), pltpu.VMEM((tm, D_MODEL), BF16),
                        pltpu.VMEM((tm, D_MODEL), F32)],
        input_output_aliases={} if out is None else {9: 0},
        compiler_params=_params("parallel", "parallel", "arbitrary"),
        name="mix_out_ffn",
    )(*args)


CONV_BLOCK = BLOCK
CONV_HALO = 32
CONV_SPAN = CONV_HALO + CONV_BLOCK
CONV_ROWS = 32
SUBLANES = 8


def _cast_stream(step, n_steps, srcs, dsts, in_bufs, out_bufs, in_sem, out_sem):
    slot = lax.rem(step, 2)

    def copies(s, sl, fetch):
        out = []
        for src, dst, ibuf, obuf in zip(srcs, dsts, in_bufs, out_bufs):
            rows = ibuf.shape[1]
            at = pl.ds(pl.multiple_of(s * rows, rows), rows)
            out.append(pltpu.make_async_copy(src.at[at], ibuf.at[sl], in_sem.at[sl]) if fetch
                       else pltpu.make_async_copy(obuf.at[sl], dst.at[at], out_sem.at[sl]))
        return out

    @pl.when(step == 0)
    def _():
        for cp in copies(step, slot, True):
            cp.start()

    for cp in copies(step, slot, True):
        cp.wait()

    @pl.when(step + 1 < n_steps)
    def _():
        for cp in copies(step + 1, 1 - slot, True):
            cp.start()

    @pl.when(step >= 2)
    def _():
        for cp in copies(step - 2, slot, False):
            cp.wait()

    for ibuf, obuf in zip(in_bufs, out_bufs):
        obuf[slot] = ibuf[slot].astype(BF16)
    for cp in copies(step, slot, False):
        cp.start()

    @pl.when(step == n_steps - 1)
    def _():
        @pl.when(step >= 1)
        def _():
            for cp in copies(step - 1, 1 - slot, False):
                cp.wait()

        for cp in copies(step, slot, False):
            cp.wait()


def _conv_kernel(h_ref, hm_ref, g_ref, wa_ref, wg_ref, shift_ref, wdw_ref, bdw_ref, lg_ref, lb_ref,
                 w2_ref, e1_ref, e3_ref, e2_ref, out_ref, e1b_ref, e3b_ref, e2b_ref,
                 u_sc, xs_sc, c_sc, y_sc, i1_sc, i3_sc, i2_sc, o1_sc, o3_sc, o2_sc, in_sem,
                 out_sem):
    n = pl.program_id(1)
    _cast_stream(pl.program_id(0) * pl.num_programs(1) + n,
                 pl.num_programs(0) * pl.num_programs(1), (e1_ref, e3_ref, e2_ref),
                 (e1b_ref, e3b_ref, e2b_ref), (i1_sc, i3_sc, i2_sc), (o1_sc, o3_sc, o2_sc),
                 in_sem, out_sem)

    def glu(x):
        xn = _rms(x, g_ref[...]).astype(BF16)
        return (_dot(xn, wa_ref[...]) * jax.nn.sigmoid(_dot(xn, wg_ref[...]))).astype(BF16)

    @pl.when(n == 0)
    def _():
        u_sc[0:CONV_HALO, :] = glu(hm_ref[BLOCK - CONV_HALO:BLOCK, :])

    h = h_ref[...]
    u_sc[CONV_HALO:CONV_SPAN, :] = glu(h)
    span = u_sc[...]
    xs_sc[0] = span.astype(F32)
    shifted = _dot(shift_ref[...], span)
    for r in range(1, SUBLANES):
        xs_sc[r] = shifted[(r - 1) * CONV_SPAN:r * CONV_SPAN]
    first = CONV_HALO - (CONV_WIDTH - 1)
    for part in range(CONV_BLOCK // BLOCK):
        for cb in range(D_MODEL // LANES):
            cs = slice(cb * LANES, (cb + 1) * LANES)
            acc = jnp.zeros((BLOCK // SUBLANES, SUBLANES, LANES), F32)
            for j in range(CONV_WIDTH):
                r = (first + j) % SUBLANES
                base = first + j - r + part * BLOCK
                x = xs_sc[r, base:base + BLOCK, cs].reshape(BLOCK // SUBLANES, SUBLANES, LANES)
                acc = acc + x * wdw_ref[j, :, cs]
            c_sc[part * BLOCK:(part + 1) * BLOCK, cs] = acc.reshape(BLOCK, LANES)
    for rb in range(CONV_BLOCK // CONV_ROWS):
        acc = c_sc[rb * CONV_ROWS:(rb + 1) * CONV_ROWS, :] + bdw_ref[...]
        mu = jnp.mean(acc, axis=-1, keepdims=True)
        dev = acc - mu
        var = jnp.mean(dev * dev, axis=-1, keepdims=True)
        y = dev * lax.rsqrt(var + EPS) * lg_ref[...] + lb_ref[...]
        y_sc[rb * CONV_ROWS:(rb + 1) * CONV_ROWS, :] = (y * jax.nn.sigmoid(y)).astype(BF16)
    out_ref[...] = h + _dot(y_sc[...], w2_ref[...])
    u_sc[0:CONV_HALO, :] = u_sc[CONV_BLOCK:CONV_SPAN, :]


def _conv_module(h, g, w_pw1, shift, wdw, bdw, lg, lb, w2, experts):
    batch, lp, _ = h.shape
    seq = lp - BLOCK
    const = lambda shape: pl.BlockSpec(shape, lambda b, n: (0,) * len(shape))
    any_spec = pl.BlockSpec(memory_space=pl.ANY)
    n_chunks = batch * (seq // CONV_BLOCK)
    chunk = lambda w: (2, w.shape[0] // n_chunks, w.shape[1])
    assert seq % CONV_BLOCK == 0
    assert all(w.shape[0] % (n_chunks * GROUP) == 0 for w in experts)
    return pl.pallas_call(
        _conv_kernel,
        out_shape=(jax.ShapeDtypeStruct((batch, seq, D_MODEL), F32),)
        + tuple(jax.ShapeDtypeStruct(w.shape, BF16) for w in experts),
        grid=(batch, seq // CONV_BLOCK),
        in_specs=[pl.BlockSpec((None, CONV_BLOCK, D_MODEL), lambda b, n: (b, n, 0)),
                  pl.BlockSpec((None, BLOCK, D_MODEL), lambda b, n: (0, seq // BLOCK, 0)),
                  const((1, D_MODEL)),
                  pl.BlockSpec((D_MODEL, D_MODEL), lambda b, n: (0, 0)),
                  pl.BlockSpec((D_MODEL, D_MODEL), lambda b, n: (0, 1)),
                  const(((SUBLANES - 1) * CONV_SPAN, CONV_SPAN)),
                  const((CONV_WIDTH, SUBLANES, D_MODEL)), const((1, D_MODEL)), const((1, D_MODEL)),
                  const((1, D_MODEL)), const((D_MODEL, D_MODEL)),
                  any_spec, any_spec, any_spec],
        out_specs=(pl.BlockSpec((None, CONV_BLOCK, D_MODEL), lambda b, n: (b, n, 0)),
                   any_spec, any_spec, any_spec),
        scratch_shapes=[pltpu.VMEM((CONV_SPAN, D_MODEL), BF16),
                        pltpu.VMEM((SUBLANES, CONV_SPAN, D_MODEL), F32),
                        pltpu.VMEM((CONV_BLOCK, D_MODEL), F32),
                        pltpu.VMEM((CONV_BLOCK, D_MODEL), BF16)]
        + [pltpu.VMEM(chunk(w), F32) for w in experts]
        + [pltpu.VMEM(chunk(w), BF16) for w in experts]
        + [pltpu.SemaphoreType.DMA((2,)), pltpu.SemaphoreType.DMA((2,))],
        compiler_params=_params("arbitrary", "arbitrary"),
        name="conv_module",
    )(h, h, g, w_pw1, w_pw1, shift, wdw, bdw, lg, lb, w2, *experts)


SORT_TOKENS = 512
GROUP = 16
SORT_CAP = 2 * SORT_TOKENS + N_EXPERTS * GROUP
SORT_COLS = D_MODEL + LANES
MOE_ROWS = 512
GROUPS_PER_TILE = MOE_ROWS // GROUP


def _one_hot_rows(pos1, pos2, width):
    r = lax.broadcasted_iota(jnp.int32, (pos1.shape[0], width), 1).astype(F32)
    return jnp.where(r == pos1, 1.0, jnp.where(r == pos2, 1.0, 0.0)).astype(BF16)


SORT_TILES_PER_STEP = 4
SPARE_ZERO = 0
SPARE_SINK = 64
assert SPARE_SINK + 2 * MOE_ROWS <= SORT_CAP


def _route_sort_kernel(h_ref, g_ref, wr_hi_ref, wr_lo_ref, xs_ref, pos_ref, ng_ref):
    is_spare = pl.program_id(0) == pl.num_programs(0) - 1

    @pl.when(is_spare)
    def _():
        xs_ref[...] = jnp.zeros_like(xs_ref)

    @pl.when(jnp.logical_not(is_spare))
    def _():
        tiles = [_route_sort_tile(h_ref.at[pl.ds(t * SORT_TOKENS, SORT_TOKENS)], g_ref,
                                  wr_hi_ref, wr_lo_ref,
                                  xs_ref.at[pl.ds(t * SORT_CAP, SORT_CAP)],
                                  pos_ref.at[pl.ds(t * SORT_TOKENS, SORT_TOKENS)], ng_ref.at[t])
                 for t in range(h_ref.shape[0] // SORT_TOKENS)]
        while tiles:
            tiles = [t for t in tiles if next(t, "done") != "done"]


def _route_sort_tile(h_ref, g_ref, wr_hi_ref, wr_lo_ref, xs_ref, pos_ref, ng_ref):
    s = SORT_TOKENS
    xn = _rms(h_ref[...], g_ref[...])
    x_hi = xn.astype(BF16)
    x_lo = (xn - x_hi.astype(F32)).astype(BF16)
    logits = (_dot(x_hi, wr_hi_ref[...]) + _dot(x_lo, wr_hi_ref[...])
              + _dot(x_hi, wr_lo_ref[...]))
    yield
    lane = lax.broadcasted_iota(jnp.int32, logits.shape, 1)
    logits = jnp.where(lane < N_EXPERTS, logits, -jnp.inf)
    m1 = jnp.max(logits, axis=-1, keepdims=True)
    i1 = jnp.min(jnp.where(logits == m1, lane, LANES), axis=-1, keepdims=True)
    rest = jnp.where(lane == i1, -jnp.inf, logits)
    m2 = jnp.max(rest, axis=-1, keepdims=True)
    i2 = jnp.min(jnp.where(rest == m2, lane, LANES), axis=-1, keepdims=True)
    e2 = jnp.exp(m2 - m1)
    g1 = 1.0 / (1.0 + e2)
    g2 = e2 / (1.0 + e2)
    yield

    oh1 = lane == i1
    oh2 = lane == i2
    oh = jnp.where(oh1, 1.0, jnp.where(oh2, 1.0, 0.0))
    cnt = jnp.sum(oh, axis=0, keepdims=True)
    ngr = jnp.floor((cnt + (GROUP - 1)) * (1.0 / GROUP))
    ri = lax.broadcasted_iota(jnp.int32, (s, s), 0)
    ci = lax.broadcasted_iota(jnp.int32, (s, s), 1)
    rank = _dot(jnp.where(ri > ci, 1.0, 0.0).astype(BF16), oh.astype(BF16))
    li = lax.broadcasted_iota(jnp.int32, (LANES, LANES), 0)
    lj = lax.broadcasted_iota(jnp.int32, (LANES, LANES), 1)
    before = _dot(jnp.broadcast_to(ngr, (8, LANES)).astype(BF16),
                  jnp.where(li < lj, 1.0, 0.0).astype(BF16))[0:1]
    yield
    base = before * GROUP + rank
    base1 = jnp.where(oh1, base, 0.0)
    base2 = jnp.where(oh2, base, 0.0)
    pos1 = jnp.sum(base1, axis=-1, keepdims=True)
    pos2 = jnp.sum(base2, axis=-1, keepdims=True)

    def as_row(v):
        ones = jnp.ones((8, LANES), BF16)
        hi = jnp.floor(v * (1.0 / 256.0))
        return (_dot_nt(ones, hi.astype(BF16)) * 256.0
                + _dot_nt(ones, (v - hi * 256.0).astype(BF16)))[0:1]

    row1, row2 = as_row(base1), as_row(base2)
    yield
    r = lax.broadcasted_iota(jnp.int32, (SORT_CAP, s), 0).astype(F32)
    p1 = jnp.where(r == row1, 1.0, 0.0).astype(BF16)
    p2 = jnp.where(r == row2, 1.0, 0.0).astype(BF16)
    xs_ref[:, 0:D_MODEL] = _dot(p1 + p2, x_hi).astype(BF16)

    def gate_lanes(gv):
        hi = gv.astype(BF16).astype(F32)
        return jnp.where(lane == 0, hi, jnp.where(lane == 1, gv - hi, 0.0)).astype(BF16)

    xs_ref[:, D_MODEL:SORT_COLS] = (_dot(p1, gate_lanes(g1))
                                    + _dot(p2, gate_lanes(g2))).astype(BF16)
    pos_ref[...] = jnp.where(lane == 0, pos1, jnp.where(lane == 1, pos2, 0.0))
    ng_ref[...] = jnp.broadcast_to(ngr, (8, LANES)).astype(jnp.int32)


def _route_sort(h, g, wr_hi, wr_lo):
    tokens = h.shape[0]
    nt = tokens // SORT_TOKENS
    tps = math.gcd(nt, SORT_TILES_PER_STEP)
    steps = nt // tps
    last = lambda i: jnp.minimum(i, steps - 1)
    return pl.pallas_call(
        _route_sort_kernel,
        out_shape=(jax.ShapeDtypeStruct(((nt + tps) * SORT_CAP, SORT_COLS), BF16),
                   jax.ShapeDtypeStruct((tokens, LANES), F32),
                   jax.ShapeDtypeStruct((nt, 8, LANES), jnp.int32)),
        grid=(steps + 1,),
        in_specs=[pl.BlockSpec((tps * SORT_TOKENS, D_MODEL), lambda i: (last(i), 0)),
                  pl.BlockSpec((1, D_MODEL), lambda i: (0, 0)),
                  pl.BlockSpec((D_MODEL, LANES), lambda i: (0, 0)),
                  pl.BlockSpec((D_MODEL, LANES), lambda i: (0, 0))],
        out_specs=(pl.BlockSpec((tps * SORT_CAP, SORT_COLS), lambda i: (i, 0)),
                   pl.BlockSpec((tps * SORT_TOKENS, LANES), lambda i: (last(i), 0)),
                   pl.BlockSpec((tps, 8, LANES), lambda i: (last(i), 0, 0))),
        compiler_params=_params("arbitrary"),
        name="route_sort",
    )(h, g, wr_hi, wr_lo)


def _group_tables(ng, n_row_tiles):
    nt, ne = ng.shape
    gpt = GROUPS_PER_TILE
    n_e = jnp.sum(ng, axis=0)
    rt_e = (n_e + gpt - 1) // gpt
    rt_end = jnp.cumsum(rt_e)
    n_rt = rt_end[-1]
    r = jnp.arange(n_row_tiles, dtype=jnp.int32)
    valid_r = r < n_rt
    e_r = jnp.minimum(jnp.sum((r[:, None] >= rt_end[None, :]).astype(jnp.int32), axis=1), ne - 1)
    e_last = jnp.sum(jnp.where(r == n_rt - 1, e_r, 0))
    e_r = jnp.where(valid_r, e_r, e_last)
    sel = (e_r[:, None] == jnp.arange(ne, dtype=jnp.int32)[None, :]).astype(jnp.int32)
    pick = lambda v: jnp.sum(sel * v[None, :], axis=1)
    k0 = (r - pick(rt_end - rt_e)) * gpt
    gidx = k0[:, None] + jnp.arange(gpt, dtype=jnp.int32)[None, :]
    valid = valid_r[:, None] & (gidx < pick(n_e)[:, None])
    incl = jnp.cumsum(ng, axis=0)
    incl_r = jnp.sum(sel[:, None, :] * incl[None, :, :], axis=2)
    ng_r = jnp.sum(sel[:, None, :] * ng[None, :, :], axis=2)
    lo = jnp.cumsum(ng, axis=1) - ng
    lo_r = jnp.sum(sel[:, None, :] * lo[None, :, :], axis=2)
    ti = jnp.sum((incl_r[:, None, :] <= gidx[:, :, None]).astype(jnp.int32), axis=2)
    ti = jnp.minimum(ti, nt - 1)
    tsel = (ti[:, :, None] == jnp.arange(nt, dtype=jnp.int32)[None, None, :]).astype(jnp.int32)
    tpick = lambda v: jnp.sum(tsel * v[:, None, :], axis=2)
    within = gidx - tpick(incl_r - ng_r)
    rows = ti * SORT_CAP + GROUP * (tpick(lo_r) + within)
    spare = nt * SORT_CAP
    sink = (spare + SPARE_SINK + (r % 2)[:, None] * MOE_ROWS
            + GROUP * jnp.arange(gpt, dtype=jnp.int32)[None, :])
    src = jnp.where(valid, rows, spare + SPARE_ZERO).reshape(-1)
    dst = jnp.where(valid, rows, sink).reshape(-1)
    return e_r, jnp.sum(valid.astype(jnp.int32), axis=1), src, dst


def _group_copies(rows_ref, tile, hbm, buf, slot, sem, to_hbm, start):
    for k in range(GROUPS_PER_TILE):
        off = pl.multiple_of(rows_ref[tile * GROUPS_PER_TILE + k], GROUP)
        buf_rows = buf.at[slot, pl.ds(k * GROUP, GROUP), :]
        hbm_rows = hbm.at[pl.ds(off, GROUP), pl.ds(0, buf_rows.shape[-1])]
        cp = (pltpu.make_async_copy(buf_rows, hbm_rows, sem.at[slot]) if to_hbm
              else pltpu.make_async_copy(hbm_rows, buf_rows, sem.at[slot]))
        if start:
            cp.start()
        else:
            cp.wait()


def _moe_group_kernel(te_ref, tv_ref, src_ref, dst_ref, xs_hbm, w1_ref, w3_ref, w2_ref, ys_hbm,
                      xbuf, ybuf, acc_sc, xsem, ysem):
    r = pl.program_id(0)
    j = pl.program_id(1)
    nr = pl.num_programs(0)
    nj = pl.num_programs(1)
    slot = lax.rem(r, 2)
    valid = lambda t: tv_ref[jnp.clip(t, 0, nr - 1)] > 0
    gather = functools.partial(_group_copies, src_ref, hbm=xs_hbm, buf=xbuf, sem=xsem,
                               to_hbm=False)
    scatter = functools.partial(_group_copies, dst_ref, hbm=ys_hbm, buf=ybuf, sem=ysem,
                                to_hbm=True)

    @pl.when(j == 0)
    def _():
        @pl.when((r == 0) & valid(r))
        def _():
            gather(tile=r, slot=slot, start=True)

        @pl.when(valid(r))
        def _():
            gather(tile=r, slot=slot, start=False)

        @pl.when((r + 1 < nr) & valid(r + 1))
        def _():
            gather(tile=r + 1, slot=1 - slot, start=True)

        @pl.when((r >= 2) & valid(r - 2))
        def _():
            scatter(tile=r - 2, slot=slot, start=False)

    def compute(rows):
        x = xbuf[slot, rows, 0:D_MODEL]
        a = _dot(x, w1_ref[0])
        b = _dot(x, w3_ref[0])
        hid = (a * jax.nn.sigmoid(a) * b).astype(BF16)
        part = _dot(hid, w2_ref[0])

        @pl.when(j == 0)
        def _():
            acc_sc[rows, :] = part

        @pl.when((j > 0) & (j < nj - 1))
        def _():
            acc_sc[rows, :] += part

        @pl.when(j == nj - 1)
        def _():
            gl = xbuf[slot, rows, D_MODEL:SORT_COLS].astype(F32)
            gate = gl[:, 0:1] + gl[:, 1:2]
            ybuf[slot, rows, :] = ((acc_sc[rows, :] + part) * gate).astype(BF16)

    half = MOE_ROWS // 2
    full = tv_ref[r] > GROUPS_PER_TILE // 2

    @pl.when(full)
    def _():
        compute(slice(0, MOE_ROWS))

    @pl.when(valid(r) & jnp.logical_not(full))
    def _():
        compute(slice(0, half))

        @pl.when(j == nj - 1)
        def _():
            ybuf[slot, half:MOE_ROWS, :] = jnp.zeros((half, D_MODEL), BF16)

    @pl.when(valid(r) & (j == nj - 1))
    def _():
        scatter(tile=r, slot=slot, start=True)

    @pl.when((r == nr - 1) & (j == nj - 1))
    def _():
        @pl.when((r >= 1) & valid(r - 1))
        def _():
            scatter(tile=r - 1, slot=1 - slot, start=False)

        @pl.when(valid(r))
        def _():
            scatter(tile=r, slot=slot, start=False)


def _moe_group(xs, tile_expert, tile_valid, src, dst, w1, w3, w2, n_row_tiles, tf):
    ne, _, dff = w1.shape
    nj = dff // tf
    assert nj >= 2

    def w_in(r, j, te, tv, s, d):
        return (te[r], 0, jnp.where(tv[r] > 0, j, nj - 1))

    def w_out(r, j, te, tv, s, d):
        return (te[r], jnp.where(tv[r] > 0, j, nj - 1), 0)

    grid_spec = pltpu.PrefetchScalarGridSpec(
        num_scalar_prefetch=4,
        grid=(n_row_tiles, nj),
        in_specs=[pl.BlockSpec(memory_space=pl.ANY),
                  pl.BlockSpec((1, D_MODEL, tf), w_in),
                  pl.BlockSpec((1, D_MODEL, tf), w_in),
                  pl.BlockSpec((1, tf, D_MODEL), w_out)],
        out_specs=pl.BlockSpec(memory_space=pl.ANY),
        scratch_shapes=[pltpu.VMEM((2, MOE_ROWS, SORT_COLS), BF16),
                        pltpu.VMEM((2, MOE_ROWS, D_MODEL), BF16),
                        pltpu.VMEM((MOE_ROWS, D_MODEL), F32),
                        pltpu.SemaphoreType.DMA((2,)),
                        pltpu.SemaphoreType.DMA((2,))],
    )
    return pl.pallas_call(
        _moe_group_kernel,
        out_shape=jax.ShapeDtypeStruct(xs.shape, xs.dtype),
        grid_spec=grid_spec,
        input_output_aliases={4: 0},
        compiler_params=_params("arbitrary", "arbitrary"),
        name="moe_group",
    )(tile_expert, tile_valid, src, dst, xs, w1, w3, w2)


def _combine_kernel(h_ref, pos_ref, ys_ref, out_ref):
    pos = pos_ref[...]
    pt = _one_hot_rows(pos[:, 0:1], pos[:, 1:2], SORT_CAP)
    out_ref[...] = h_ref[...] + _dot(pt, ys_ref[...])


def _combine(h, pos, ys):
    tokens = h.shape[0]
    return pl.pallas_call(
        _combine_kernel,
        out_shape=jax.ShapeDtypeStruct((tokens, D_MODEL), F32),
        grid=(tokens // SORT_TOKENS,),
        in_specs=[pl.BlockSpec((SORT_TOKENS, D_MODEL), lambda i: (i, 0)),
                  pl.BlockSpec((SORT_TOKENS, LANES), lambda i: (i, 0)),
                  pl.BlockSpec((SORT_CAP, D_MODEL), lambda i: (i, 0))],
        out_specs=pl.BlockSpec((SORT_TOKENS, D_MODEL), lambda i: (i, 0)),
        compiler_params=_params("parallel"),
        name="combine",
    )(h, pos, ys)


def _row_tile(rows, want):
    tm = want
    while rows % tm:
        tm //= 2
    return tm


def _rope_tables(lp):
    inv_freq = 1.0 / (ROPE_THETA ** (jnp.arange(0, SWA_HEAD_DIM, 2, dtype=F32) / SWA_HEAD_DIM))
    row = jnp.arange(lp)
    seq = lp - BLOCK
    pos = jnp.where(row < seq, row + N_META, row - seq - FRONT).astype(F32)
    ang = pos[:, None] * inv_freq[None, :]
    cos, sin = jnp.cos(ang), jnp.sin(ang)
    zero = jnp.zeros_like(sin)
    rep = LANES // SWA_HEAD_DIM
    cos_t = jnp.tile(jnp.concatenate([cos, cos], axis=-1), (1, rep))
    sin_a = jnp.tile(jnp.concatenate([zero, sin], axis=-1), (1, rep))
    sin_b = jnp.tile(jnp.concatenate([-sin, zero], axis=-1), (1, rep))
    return cos_t, sin_a, sin_b


def kernel(x, meta, a_norm, a_w_in, a_w_gate2, a_b_gate, a_q_norm, a_k_norm, a_sinks, a_o_norm,
           a_w_out, f_norm, f_w1, f_w3, f_w2, c_norm, c_w_pw1, c_w_dw, c_b_dw, c_ln_g, c_ln_b,
           c_w_pw2, m_norm, m_w_router, m_w1, m_w3, m_w2):
    batch, seq, _ = x.shape
    assert seq % BLOCK == 0
    lp = BLOCK + seq
    nb = lp // BLOCK
    rows = batch * lp

    head = jnp.concatenate([jnp.zeros((FRONT, D_MODEL), x.dtype), meta.astype(x.dtype)],
                           axis=0)[None]
    tm = _row_tile(seq, 512)

    w_in = a_w_in[0]
    gq, gk, gv, gr, glr, sq, sk, sv = jnp.split(
        w_in, [256, 512, 1024, 1536, 1552, 2064, 2192], axis=-1)
    w_in_r = jnp.concatenate(
        [gq, gk, gv, gr, sq, sk, sv, glr,
         jnp.zeros((D_MODEL, Z_COLS - Z_LR - GLA_LOWRANK), w_in.dtype)], axis=-1).astype(BF16)
    lane = jnp.arange(LANES)
    bd = (lane[:, None] // SWA_HEAD_DIM == lane[None, :] // SWA_HEAD_DIM).astype(BF16)
    rope = _rope_tables(lp) + (jnp.tile(a_q_norm[0], SWA_Q_HEADS)[None],
                               jnp.tile(a_k_norm[0], SWA_KV_HEADS)[None], bd)
    z = _in_proj(x, a_norm[0][None], w_in_r, rope, batch, lp, _row_tile(seq, 1024), 0)
    z = _in_proj(head, a_norm[0][None], w_in_r, rope, batch, lp, BLOCK, seq, z)

    wg = jnp.zeros((LANES, GLA_QK), F32).at[:GLA_LOWRANK].set(a_w_gate2[0]).astype(BF16)
    bb = math.gcd(batch, MIX_BATCH)
    o_a = _gla(z, wg, a_b_gate[0][None], a_o_norm[0][None], bb)
    o_b, d_w1, d_w3, d_w2 = _swa(z, a_sinks[0], (f_w1[0], f_w3[0], f_w2[0]), bb)

    w_out = a_w_out[0].astype(BF16)
    ffn = (w_out, f_norm[0][None], d_w1, d_w3, d_w2)
    h = _mix_out_ffn(x, o_a, o_b, *ffn, tm, FF_TILE, 0)
    h = _mix_out_ffn(head, o_a, o_b, *ffn, BLOCK, FF_TILE, seq, h)

    si = jnp.arange((SUBLANES - 1) * CONV_SPAN)
    shift = (jnp.arange(CONV_SPAN)[None, :]
             == (si % CONV_SPAN + si // CONV_SPAN + 1)[:, None]).astype(BF16)
    ne, _, dff = m_w1[0].shape
    experts = (m_w1[0].reshape(ne * D_MODEL, dff), m_w3[0].reshape(ne * D_MODEL, dff),
               m_w2[0].reshape(ne * dff, D_MODEL))
    h, e_w1, e_w3, e_w2 = _conv_module(
        h, c_norm[0][None], c_w_pw1[0].astype(BF16), shift,
        jnp.broadcast_to(c_w_dw[0][:, None, :], (CONV_WIDTH, SUBLANES, D_MODEL)),
        c_b_dw[0][None], c_ln_g[0][None], c_ln_b[0][None], c_w_pw2[0].astype(BF16), experts)
    h = h.reshape(batch * seq, D_MODEL)

    wr = jnp.zeros((D_MODEL, LANES), F32).at[:, :N_EXPERTS].set(m_w_router[0])
    wr_hi = wr.astype(BF16)
    wr_lo = (wr - wr_hi.astype(F32)).astype(BF16)
    tokens = batch * seq
    assert tokens % SORT_TOKENS == 0
    nt = tokens // SORT_TOKENS
    xs, pos, ng = _route_sort(h, m_norm[0][None], wr_hi, wr_lo)
    ng = ng[:, 0, :N_EXPERTS]
    n_row_tiles = (nt * (SORT_CAP // GROUP)) // GROUPS_PER_TILE + N_EXPERTS
    tile_expert, tile_valid, src, dst = _group_tables(ng, n_row_tiles)
    ys = _moe_group(xs, tile_expert, tile_valid, src, dst, e_w1.reshape(ne, D_MODEL, dff),
                    e_w3.reshape(ne, D_MODEL, dff), e_w2.reshape(ne, dff, D_MODEL), n_row_tiles,
                    FF_TILE)
    out = _combine(h, pos, ys)
    return out.reshape(batch, seq, D_MODEL)
```

```python
import functools
import math

import jax
import jax.numpy as jnp
from jax import lax
from jax.experimental import pallas as pl
from jax.experimental.pallas import tpu as pltpu

F32 = jnp.float32
BF16 = jnp.bfloat16

D_MODEL = 1024
N_META = 16
EPS = 1e-6
ROPE_THETA = 10000.0
NEG_INF = -1e30
LOG2E = math.log2(math.e)
GLA_HEADS = 4
GLA_DK = 64
GLA_DV = 128
GLA_LOWRANK = 16
GLA_TAU = 16.0
GLA_CHUNK = 64
SWA_Q_HEADS = 8
SWA_KV_HEADS = 2
SWA_HEAD_DIM = 64
SWA_WINDOW = 128
CONV_WIDTH = 31
D_FF = 3584
N_EXPERTS = 8

GLA_QK = GLA_HEADS * GLA_DK
GLA_V = GLA_HEADS * GLA_DV
SWA_Q = SWA_Q_HEADS * SWA_HEAD_DIM
SWA_KV = SWA_KV_HEADS * SWA_HEAD_DIM
SWA_GROUP = SWA_Q_HEADS // SWA_KV_HEADS

LANES = 128
FF_TILE = 1792
BLOCK = 128
FRONT = BLOCK - N_META
VMEM_LIMIT = 56 * 1024 * 1024

Z_Q, Z_K, Z_V, Z_R = 0, 256, 512, 1024
Z_SQ, Z_SK, Z_SV, Z_LR = 1536, 2048, 2176, 2304
Z_COLS = 2432


def _rms(x, g):
    return x * lax.rsqrt(jnp.mean(x * x, axis=-1, keepdims=True) + EPS) * g


def _dot(a, b):
    return jnp.dot(a, b, preferred_element_type=F32)


def _dot_nt(a, b):
    return lax.dot_general(a, b, (((1,), (1,)), ((), ())), preferred_element_type=F32)


def _dot_tn(a, b):
    return lax.dot_general(a, b, (((0,), (0,)), ((), ())), preferred_element_type=F32)


def _stored_block(n, lp):
    return jnp.where(n == 0, lp // BLOCK - 1, n - 1)


def _params(*sem):
    return pltpu.CompilerParams(dimension_semantics=sem, vmem_limit_bytes=VMEM_LIMIT)


IN_PROJ_ROWS = 256


def _in_proj_kernel(h_ref, g_ref, w_ref, cos_ref, sa_ref, sb_ref, qn_ref, kn_ref, bd_ref,
                    *rest):
    z_ref = rest[-1]
    tile = lambda t: jnp.concatenate([t] * (SWA_Q // LANES), axis=-1)
    tm = h_ref.shape[0]
    chunks = [slice(r, min(r + IN_PROJ_ROWS, tm)) for r in range(0, tm, IN_PROJ_ROWS)]
    project = lambda rs: _dot(_rms(h_ref[rs, :], g_ref[...]).astype(BF16), w_ref[...])
    zs = {0: project(chunks[0])}
    for i, rs in enumerate(chunks):
        if i + 1 < len(chunks):
            zs[i + 1] = project(chunks[i + 1])
        z = zs.pop(i)
        cos, sa, sb = cos_ref[rs, :], sa_ref[rs, :], sb_ref[rs, :]
        q = _norm_rope(z[None, :, Z_SQ:Z_SQ + SWA_Q], qn_ref[...], tile(cos), tile(sa),
                       tile(sb), bd_ref[...])[0] * (SWA_HEAD_DIM ** -0.5 * LOG2E)
        k = _norm_rope(z[None, :, Z_SK:Z_SK + SWA_KV], kn_ref[...], cos, sa, sb, bd_ref[...])[0]
        z = jnp.concatenate([z[:, :Z_SQ], q, k, z[:, Z_SV:]], axis=-1).astype(z_ref.dtype)
        if z_ref.ndim == 2:
            z_ref[rs, :] = z
        else:
            z_ref[:, rs, :] = jnp.broadcast_to(z[None], (z_ref.shape[0],) + z.shape)


def _in_proj(h, g, w, rope, batch, lp, tm, first_row, z=None):
    hb, rows, _ = h.shape
    n = w.shape[1]
    shared = hb == 1 and batch > 1
    const = lambda a: pl.BlockSpec(a.shape, lambda b, j: (0,) * a.ndim)
    table = pl.BlockSpec((tm, LANES), lambda b, j: (first_row // tm + j, 0))
    in_specs = [pl.BlockSpec((None, tm, D_MODEL), lambda b, j: (b, j, 0)),
                pl.BlockSpec((1, D_MODEL), lambda b, j: (0, 0)),
                pl.BlockSpec((D_MODEL, n), lambda b, j: (0, 0)),
                table, table, table, const(rope[3]), const(rope[4]), const(rope[5])]
    args = (h, g, w) + tuple(rope)
    if z is not None:
        in_specs.append(pl.BlockSpec(memory_space=pl.ANY))
        args += (z,)
    return pl.pallas_call(
        _in_proj_kernel,
        out_shape=jax.ShapeDtypeStruct((batch, lp, n), BF16),
        grid=(hb, rows // tm),
        in_specs=in_specs,
        out_specs=pl.BlockSpec((batch if shared else None, tm, n),
                               lambda b, j: (b, first_row // tm + j, 0)),
        input_output_aliases={} if z is None else {len(args) - 1: 0},
        compiler_params=_params("parallel", "parallel"),
        name="in_proj",
    )(*args)


MIX_BATCH = 8


def _gla_kernel(q_ref, k_ref, v_ref, r_ref, lr_ref, wg_ref, bg_ref, on_ref, o_ref, st_ref):
    n = pl.program_id(1)

    @pl.when(n == 0)
    def _():
        st_ref[...] = jnp.zeros_like(st_ref)

    c = GLA_CHUNK
    items = [(bi, ch) for bi in range(q_ref.shape[0]) for ch in range(BLOCK // c)]
    ri = lax.broadcasted_iota(jnp.int32, (BLOCK, BLOCK), 0)
    ci = lax.broadcasted_iota(jnp.int32, (BLOCK, BLOCK), 1)
    tri = jnp.where((ri >= ci) & (ri // c == ci // c), 1.0, 0.0).astype(BF16)
    causal = (lax.broadcasted_iota(jnp.int32, (c, c), 0)
              >= lax.broadcasted_iota(jnp.int32, (c, c), 1))
    row = lax.broadcasted_iota(jnp.int32, (BLOCK, GLA_QK), 0) + n * BLOCK
    on = on_ref[...]

    b_all = []
    for bi in range(q_ref.shape[0]):
        gp = _dot(lr_ref[bi], wg_ref[...]) + bg_ref[...]
        g = (jnp.minimum(gp, 0.0) - jnp.log(1.0 + jnp.exp(-jnp.abs(gp)))) * (1.0 / GLA_TAU)
        g = jnp.where(row >= FRONT, g, 0.0)
        g_hi = g.astype(BF16)
        g_lo = (g - g_hi.astype(F32)).astype(BF16)
        b_all.append(_dot(tri, g_hi) + _dot(tri, g_lo))

    q_t, k_t, k_d, dec = {}, {}, {}, {}
    for bi, ch in items:
        rs = slice(ch * c, (ch + 1) * c)
        b = b_all[bi][rs]
        b_last = b_all[bi][(ch + 1) * c - 1:(ch + 1) * c, :]
        qf = q_ref[bi, rs, :].astype(F32)
        kf = k_ref[bi, rs, :].astype(F32)
        q_t[bi, ch] = (qf * jnp.exp(b) * (GLA_DK ** -0.5)).astype(BF16)
        k_t[bi, ch] = (kf * jnp.exp(-b)).astype(BF16)
        k_d[bi, ch] = (kf * jnp.exp(b_last - b)).astype(BF16)
        dec[bi, ch] = jnp.exp(b_last)

    att, kv = {}, {}
    for bi, ch in items:
        rs = slice(ch * c, (ch + 1) * c)
        for hd in range(GLA_HEADS):
            ks = slice(hd * GLA_DK, (hd + 1) * GLA_DK)
            vh = v_ref[bi, rs, hd * GLA_DV:(hd + 1) * GLA_DV]
            att[bi, ch, hd] = jnp.where(
                causal, _dot_nt(q_t[bi, ch][:, ks], k_t[bi, ch][:, ks]), 0.0).astype(BF16)
            kv[bi, ch, hd] = _dot_tn(vh, k_d[bi, ch][:, ks])

    for bi in range(q_ref.shape[0]):
        for hd in range(GLA_HEADS):
            ks = slice(hd * GLA_DK, (hd + 1) * GLA_DK)
            vs = slice(hd * GLA_DV, (hd + 1) * GLA_DV)
            st = st_ref[bi, hd]
            for ch in range(BLOCK // c):
                rs = slice(ch * c, (ch + 1) * c)
                o = (_dot(att[bi, ch, hd], v_ref[bi, rs, vs])
                     + _dot_nt(q_t[bi, ch][:, ks], st.astype(BF16)))
                st = st * dec[bi, ch][:, ks] + kv[bi, ch, hd]
                o = _rms(o, on)
                r = r_ref[bi, rs, vs].astype(F32)
                o_ref[bi, rs, vs] = (o * (r * jax.nn.sigmoid(r))).astype(o_ref.dtype)
            st_ref[bi, hd] = st


def _gla(z, wg, bg, on, bb):
    batch, lp, _ = z.shape

    def col(width, start):
        return pl.BlockSpec((bb, BLOCK, width), lambda b, n: (b, _stored_block(n, lp), start // width))

    const = lambda shape: pl.BlockSpec(shape, lambda b, n: (0,) * len(shape))
    return pl.pallas_call(
        _gla_kernel,
        out_shape=jax.ShapeDtypeStruct((batch, lp, GLA_V), BF16),
        grid=(batch // bb, lp // BLOCK),
        in_specs=[col(GLA_QK, Z_Q), col(GLA_QK, Z_K), col(GLA_V, Z_V), col(GLA_V, Z_R),
                  col(LANES, Z_LR), const((LANES, GLA_QK)), const((1, GLA_QK)),
                  const((1, GLA_DV))],
        out_specs=pl.BlockSpec((bb, BLOCK, GLA_V), lambda b, n: (b, _stored_block(n, lp), 0)),
        scratch_shapes=[pltpu.VMEM((bb, GLA_HEADS, GLA_DV, GLA_DK), F32)],
        compiler_params=_params("parallel", "arbitrary"),
        name="gla",
    )(z, z, z, z, z, wg, bg, on)


def _group_mean_sq(x, bd):
    sq = x * x
    hi = sq.astype(BF16)
    lo = (sq - hi.astype(F32)).astype(BF16)
    return _dot(hi, bd) + _dot(lo, bd)


def _norm_rope(x, gain, cos, sin_a, sin_b, bd):
    nbat, rows, width = x.shape
    x2 = x.reshape(nbat * rows, width)
    parts = []
    for s in range(width // LANES):
        xs = x2[:, s * LANES:(s + 1) * LANES]
        ms = _group_mean_sq(xs, bd) * (1.0 / SWA_HEAD_DIM)
        parts.append(xs * lax.rsqrt(ms + EPS))
    xn = (parts[0] if len(parts) == 1 else jnp.concatenate(parts, axis=-1)) * gain
    half = SWA_HEAD_DIM // 2
    back = lambda t: t.reshape(nbat, rows, width)
    return (back(xn) * cos + back(pltpu.roll(xn, half, 1)) * sin_a
            + back(pltpu.roll(xn, width - half, 1)) * sin_b)


def _swa_kernel(sinks_ref, q_ref, kc_ref, kp_ref, vc_ref, vp_ref, f1_ref, f3_ref, f2_ref,
                o_ref, f1b_ref, f3b_ref, f2b_ref, i1_sc, i3_sc, i2_sc, o1_sc, o3_sc, o2_sc,
                in_sem, out_sem, *, cast_steps):
    n = pl.program_id(1)
    nbat = q_ref.shape[0]

    cast_step = pl.program_id(0) * pl.num_programs(1) + n

    @pl.when(cast_step < cast_steps)
    def _():
        _cast_stream(cast_step, cast_steps, (f1_ref, f3_ref, f2_ref),
                     (f1b_ref, f3b_ref, f2b_ref), (i1_sc, i3_sc, i2_sc), (o1_sc, o3_sc, o2_sc),
                     in_sem, out_sem)

    q = q_ref[...]
    k_c = kc_ref[...]
    k_p = kp_ref[...]

    gq = SWA_GROUP * BLOCK
    r = lax.broadcasted_iota(jnp.int32, (gq, 2 * BLOCK), 0) & (BLOCK - 1)
    cidx = lax.broadcasted_iota(jnp.int32, (gq, 2 * BLOCK), 1)
    allowed = (cidx > r) & (cidx <= r + SWA_WINDOW) & ((n - 1) * BLOCK + cidx >= FRONT)
    rgroup = lax.broadcasted_iota(jnp.int32, (gq, 1), 0) // BLOCK
    items = [(bi, hk) for bi in range(nbat) for hk in range(SWA_KV_HEADS)]

    s, sink = {}, {}
    for hk in range(SWA_KV_HEADS):
        sk = jnp.zeros((gq, 1), F32)
        for g in range(SWA_GROUP):
            sk = jnp.where(rgroup == g, sinks_ref[hk * SWA_GROUP + g] * LOG2E, sk)
        sink[hk] = sk
    for bi, hk in items:
        ks = slice(hk * SWA_HEAD_DIM, (hk + 1) * SWA_HEAD_DIM)
        qg = jnp.concatenate(
            [q[bi][:, (hk * SWA_GROUP + g) * SWA_HEAD_DIM:(hk * SWA_GROUP + g + 1) * SWA_HEAD_DIM]
             for g in range(SWA_GROUP)], axis=0)
        k = jnp.concatenate([k_p[bi][:, ks], k_c[bi][:, ks]], axis=0)
        s[bi, hk] = jnp.where(allowed, _dot_nt(qg, k), NEG_INF)

    p, denom = {}, {}
    for bi, hk in items:
        m = jnp.maximum(jnp.max(s[bi, hk], axis=-1, keepdims=True), sink[hk])
        e = jnp.exp2(s[bi, hk] - m)
        denom[bi, hk] = jnp.sum(e, axis=-1, keepdims=True) + jnp.exp2(sink[hk] - m)
        p[bi, hk] = e.astype(BF16)

    for bi, hk in items:
        ks = slice(hk * SWA_HEAD_DIM, (hk + 1) * SWA_HEAD_DIM)
        v = jnp.concatenate([vp_ref[bi, :, ks], vc_ref[bi, :, ks]], axis=0)
        o = _dot(p[bi, hk], v) / denom[bi, hk]
        for g in range(SWA_GROUP):
            hq = hk * SWA_GROUP + g
            o_ref[bi, :, hq * SWA_HEAD_DIM:(hq + 1) * SWA_HEAD_DIM] = (
                o[g * BLOCK:(g + 1) * BLOCK].astype(o_ref.dtype))


def _swa(z, sinks, ffn, bb):
    batch, lp, _ = z.shape
    steps = (batch // bb) * (lp // BLOCK)
    cast_steps = 1 << (steps.bit_length() - 1)
    assert all(w.shape[0] % (cast_steps * GROUP) == 0 for w in ffn)
    chunk = lambda w: (2, w.shape[0] // cast_steps, w.shape[1])
    any_spec = pl.BlockSpec(memory_space=pl.ANY)

    def cur(width, start):
        return pl.BlockSpec((bb, BLOCK, width),
                            lambda b, n, s: (b, _stored_block(n, lp), start // width))

    def prev(width, start):
        return pl.BlockSpec((bb, BLOCK, width),
                            lambda b, n, s: (b, _stored_block(jnp.maximum(n - 1, 0), lp),
                                             start // width))

    grid_spec = pltpu.PrefetchScalarGridSpec(
        num_scalar_prefetch=1,
        grid=(batch // bb, lp // BLOCK),
        in_specs=[cur(SWA_Q, Z_SQ), cur(SWA_KV, Z_SK), prev(SWA_KV, Z_SK),
                  cur(SWA_KV, Z_SV), prev(SWA_KV, Z_SV),
                  any_spec, any_spec, any_spec],
        out_specs=(pl.BlockSpec((bb, BLOCK, SWA_Q),
                                lambda b, n, s: (b, _stored_block(n, lp), 0)),
                   any_spec, any_spec, any_spec),
        scratch_shapes=[pltpu.VMEM(chunk(w), F32) for w in ffn]
        + [pltpu.VMEM(chunk(w), BF16) for w in ffn]
        + [pltpu.SemaphoreType.DMA((2,)), pltpu.SemaphoreType.DMA((2,))],
    )
    return pl.pallas_call(
        functools.partial(_swa_kernel, cast_steps=cast_steps),
        out_shape=(jax.ShapeDtypeStruct((batch, lp, SWA_Q), BF16),)
        + tuple(jax.ShapeDtypeStruct(w.shape, BF16) for w in ffn),
        grid_spec=grid_spec,
        compiler_params=_params("arbitrary", "arbitrary"),
        name="swa",
    )(sinks, z, z, z, z, z, *ffn)


def _ffn_kernel(h_ref, oa_ref, ob_ref, woa_ref, wob_ref, g_ref, w1_ref, w3_ref, w2_ref,
                *rest):
    out_ref, h1_sc, xn_sc, acc_sc = rest[-4:]
    j = pl.program_id(2)
    nj = pl.num_programs(2)

    @pl.when(j == 0)
    def _():
        h1 = h_ref[...] + _dot(oa_ref[...], woa_ref[...]) + _dot(ob_ref[...], wob_ref[...])
        h1_sc[...] = h1
        xn_sc[...] = _rms(h1, g_ref[...]).astype(BF16)

    x = xn_sc[...]
    a = _dot(x, w1_ref[...])
    b = _dot(x, w3_ref[...])
    hid = (a * jax.nn.sigmoid(a) * b).astype(BF16)
    acc = jnp.where(j == 0, h1_sc[...], acc_sc[...]) + _dot(hid, w2_ref[...])
    acc_sc[...] = acc
    out_ref[...] = acc


def _mix_out_ffn(h, oa, ob, w_out, g, w1, w3, w2, tm, tf, first_row, out=None):
    hb, rows, _ = h.shape
    batch, lp, _ = oa.shape
    dff = w1.shape[1]
    row0 = first_row // tm
    const = lambda shape: pl.BlockSpec(shape, lambda b, i, j: (0, 0))
    in_specs = [pl.BlockSpec((None, tm, D_MODEL),
                             lambda b, i, j: (jnp.minimum(b, hb - 1), i, 0)),
                pl.BlockSpec((None, tm, GLA_V), lambda b, i, j: (b, row0 + i, 0)),
                pl.BlockSpec((None, tm, SWA_Q), lambda b, i, j: (b, row0 + i, 0)),
                pl.BlockSpec((GLA_V, D_MODEL), lambda b, i, j: (0, 0)),
                pl.BlockSpec((SWA_Q, D_MODEL), lambda b, i, j: (GLA_V // SWA_Q, 0)),
                const((1, D_MODEL)),
                pl.BlockSpec((D_MODEL, tf), lambda b, i, j: (0, j)),
                pl.BlockSpec((D_MODEL, tf), lambda b, i, j: (0, j)),
                pl.BlockSpec((tf, D_MODEL), lambda b, i, j: (j, 0))]
    assert GLA_V % SWA_Q == 0
    args = (h, oa, ob, w_out, w_out, g, w1, w3, w2)
    if out is not None:
        in_specs.append(pl.BlockSpec(memory_space=pl.ANY))
        args += (out,)
    return pl.pallas_call(
        _ffn_kernel,
        out_shape=jax.ShapeDtypeStruct((batch, lp, D_MODEL), F32),
        grid=(hb, rows // tm, dff // tf),
        in_specs=in_specs,
        out_specs=pl.BlockSpec((None, tm, D_MODEL), lambda b, i, j: (b, row0 + i, 0)),
        scratch_shapes=[pltpu.VMEM((tm, D_MODEL), F32), pltpu.VMEM((tm, D_MODEL), BF16),
                        pltpu.VMEM((tm, D_MODEL), F32)],
        input_output_aliases={} if out is None else {9: 0},
        compiler_params=_params("parallel", "parallel", "arbitrary"),
        name="mix_out_ffn",
    )(*args)


CONV_BLOCK = BLOCK
CONV_HALO = 32
CONV_SPAN = CONV_HALO + CONV_BLOCK
CONV_ROWS = 32
SUBLANES = 8


def _cast_stream(step, n_steps, srcs, dsts, in_bufs, out_bufs, in_sem, out_sem):
    slot = lax.rem(step, 2)

    def copies(s, sl, fetch):
        out = []
        for src, dst, ibuf, obuf in zip(srcs, dsts, in_bufs, out_bufs):
            rows = ibuf.shape[1]
            at = pl.ds(pl.multiple_of(s * rows, rows), rows)
            out.append(pltpu.make_async_copy(src.at[at], ibuf.at[sl], in_sem.at[sl]) if fetch
                       else pltpu.make_async_copy(obuf.at[sl], dst.at[at], out_sem.at[sl]))
        return out

    @pl.when(step == 0)
    def _():
        for cp in copies(step, slot, True):
            cp.start()

    for cp in copies(step, slot, True):
        cp.wait()

    @pl.when(step + 1 < n_steps)
    def _():
        for cp in copies(step + 1, 1 - slot, True):
            cp.start()

    @pl.when(step >= 2)
    def _():
        for cp in copies(step - 2, slot, False):
            cp.wait()

    for ibuf, obuf in zip(in_bufs, out_bufs):
        obuf[slot] = ibuf[slot].astype(BF16)
    for cp in copies(step, slot, False):
        cp.start()

    @pl.when(step == n_steps - 1)
    def _():
        @pl.when(step >= 1)
        def _():
            for cp in copies(step - 1, 1 - slot, False):
                cp.wait()

        for cp in copies(step, slot, False):
            cp.wait()


def _conv_kernel(h_ref, hm_ref, g_ref, wa_ref, wg_ref, shift_ref, wdw_ref, bdw_ref, lg_ref, lb_ref,
                 w2_ref, e1_ref, e3_ref, e2_ref, out_ref, e1b_ref, e3b_ref, e2b_ref,
                 u_sc, xs_sc, c_sc, y_sc, i1_sc, i3_sc, i2_sc, o1_sc, o3_sc, o2_sc, in_sem,
                 out_sem):
    n = pl.program_id(1)
    _cast_stream(pl.program_id(0) * pl.num_programs(1) + n,
                 pl.num_programs(0) * pl.num_programs(1), (e1_ref, e3_ref, e2_ref),
                 (e1b_ref, e3b_ref, e2b_ref), (i1_sc, i3_sc, i2_sc), (o1_sc, o3_sc, o2_sc),
                 in_sem, out_sem)

    def glu(x):
        xn = _rms(x, g_ref[...]).astype(BF16)
        return (_dot(xn, wa_ref[...]) * jax.nn.sigmoid(_dot(xn, wg_ref[...]))).astype(BF16)

    @pl.when(n == 0)
    def _():
        u_sc[0:CONV_HALO, :] = glu(hm_ref[BLOCK - CONV_HALO:BLOCK, :])

    h = h_ref[...]
    u_sc[CONV_HALO:CONV_SPAN, :] = glu(h)
    span = u_sc[...]
    xs_sc[0] = span.astype(F32)
    shifted = _dot(shift_ref[...], span)
    for r in range(1, SUBLANES):
        xs_sc[r] = shifted[(r - 1) * CONV_SPAN:r * CONV_SPAN]
    first = CONV_HALO - (CONV_WIDTH - 1)
    for part in range(CONV_BLOCK // BLOCK):
        for cb in range(D_MODEL // LANES):
            cs = slice(cb * LANES, (cb + 1) * LANES)
            acc = jnp.zeros((BLOCK // SUBLANES, SUBLANES, LANES), F32)
            for j in range(CONV_WIDTH):
                r = (first + j) % SUBLANES
                base = first + j - r + part * BLOCK
                x = xs_sc[r, base:base + BLOCK, cs].reshape(BLOCK // SUBLANES, SUBLANES, LANES)
                acc = acc + x * wdw_ref[j, :, cs]
            c_sc[part * BLOCK:(part + 1) * BLOCK, cs] = acc.reshape(BLOCK, LANES)
    for rb in range(CONV_BLOCK // CONV_ROWS):
        acc = c_sc[rb * CONV_ROWS:(rb + 1) * CONV_ROWS, :] + bdw_ref[...]
        mu = jnp.mean(acc, axis=-1, keepdims=True)
        dev = acc - mu
        var = jnp.mean(dev * dev, axis=-1, keepdims=True)
        y = dev * lax.rsqrt(var + EPS) * lg_ref[...] + lb_ref[...]
        y_sc[rb * CONV_ROWS:(rb + 1) * CONV_ROWS, :] = (y * jax.nn.sigmoid(y)).astype(BF16)
    out_ref[...] = h + _dot(y_sc[...], w2_ref[...])
    u_sc[0:CONV_HALO, :] = u_sc[CONV_BLOCK:CONV_SPAN, :]


def _conv_module(h, g, w_pw1, shift, wdw, bdw, lg, lb, w2, experts):
    batch, lp, _ = h.shape
    seq = lp - BLOCK
    const = lambda shape: pl.BlockSpec(shape, lambda b, n: (0,) * len(shape))
    any_spec = pl.BlockSpec(memory_space=pl.ANY)
    n_chunks = batch * (seq // CONV_BLOCK)
    chunk = lambda w: (2, w.shape[0] // n_chunks, w.shape[1])
    assert seq % CONV_BLOCK == 0
    assert all(w.shape[0] % (n_chunks * GROUP) == 0 for w in experts)
    return pl.pallas_call(
        _conv_kernel,
        out_shape=(jax.ShapeDtypeStruct((batch, seq, D_MODEL), F32),)
        + tuple(jax.ShapeDtypeStruct(w.shape, BF16) for w in experts),
        grid=(batch, seq // CONV_BLOCK),
        in_specs=[pl.BlockSpec((None, CONV_BLOCK, D_MODEL), lambda b, n: (b, n, 0)),
                  pl.BlockSpec((None, BLOCK, D_MODEL), lambda b, n: (0, seq // BLOCK, 0)),
                  const((1, D_MODEL)),
                  pl.BlockSpec((D_MODEL, D_MODEL), lambda b, n: (0, 0)),
                  pl.BlockSpec((D_MODEL, D_MODEL), lambda b, n: (0, 1)),
                  const(((SUBLANES - 1) * CONV_SPAN, CONV_SPAN)),
                  const((CONV_WIDTH, SUBLANES, D_MODEL)), const((1, D_MODEL)), const((1, D_MODEL)),
                  const((1, D_MODEL)), const((D_MODEL, D_MODEL)),
                  any_spec, any_spec, any_spec],
        out_specs=(pl.BlockSpec((None, CONV_BLOCK, D_MODEL), lambda b, n: (b, n, 0)),
                   any_spec, any_spec, any_spec),
        scratch_shapes=[pltpu.VMEM((CONV_SPAN, D_MODEL), BF16),
                        pltpu.VMEM((SUBLANES, CONV_SPAN, D_MODEL), F32),
                        pltpu.VMEM((CONV_BLOCK, D_MODEL), F32),
                        pltpu.VMEM((CONV_BLOCK, D_MODEL), BF16)]
        + [pltpu.VMEM(chunk(w), F32) for w in experts]
        + [pltpu.VMEM(chunk(w), BF16) for w in experts]
        + [pltpu.SemaphoreType.DMA((2,)), pltpu.SemaphoreType.DMA((2,))],
        compiler_params=_params("arbitrary", "arbitrary"),
        name="conv_module",
    )(h, h, g, w_pw1, w_pw1, shift, wdw, bdw, lg, lb, w2, *experts)


SORT_TOKENS = 512
GROUP = 16
SORT_CAP = 2 * SORT_TOKENS + N_EXPERTS * GROUP
SORT_COLS = D_MODEL + LANES
MOE_ROWS = 512
GROUPS_PER_TILE = MOE_ROWS // GROUP


def _one_hot_rows(pos1, pos2, width):
    r = lax.broadcasted_iota(jnp.int32, (pos1.shape[0], width), 1).astype(F32)
    return jnp.where(r == pos1, 1.0, jnp.where(r == pos2, 1.0, 0.0)).astype(BF16)


SORT_TILES_PER_STEP = 4
SPARE_ZERO = 0
SPARE_SINK = 64
assert SPARE_SINK + 2 * MOE_ROWS <= SORT_CAP


def _route_sort_kernel(h_ref, g_ref, wr_hi_ref, wr_lo_ref, xs_ref, pos_ref, ng_ref):
    is_spare = pl.program_id(0) == pl.num_programs(0) - 1

    @pl.when(is_spare)
    def _():
        xs_ref[...] = jnp.zeros_like(xs_ref)

    @pl.when(jnp.logical_not(is_spare))
    def _():
        tiles = [_route_sort_tile(h_ref.at[pl.ds(t * SORT_TOKENS, SORT_TOKENS)], g_ref,
                                  wr_hi_ref, wr_lo_ref,
                                  xs_ref.at[pl.ds(t * SORT_CAP, SORT_CAP)],
                                  pos_ref.at[pl.ds(t * SORT_TOKENS, SORT_TOKENS)], ng_ref.at[t])
                 for t in range(h_ref.shape[0] // SORT_TOKENS)]
        while tiles:
            tiles = [t for t in tiles if next(t, "done") != "done"]


def _route_sort_tile(h_ref, g_ref, wr_hi_ref, wr_lo_ref, xs_ref, pos_ref, ng_ref):
    s = SORT_TOKENS
    xn = _rms(h_ref[...], g_ref[...])
    x_hi = xn.astype(BF16)
    x_lo = (xn - x_hi.astype(F32)).astype(BF16)
    logits = (_dot(x_hi, wr_hi_ref[...]) + _dot(x_lo, wr_hi_ref[...])
              + _dot(x_hi, wr_lo_ref[...]))
    yield
    lane = lax.broadcasted_iota(jnp.int32, logits.shape, 1)
    logits = jnp.where(lane < N_EXPERTS, logits, -jnp.inf)
    m1 = jnp.max(logits, axis=-1, keepdims=True)
    i1 = jnp.min(jnp.where(logits == m1, lane, LANES), axis=-1, keepdims=True)
    rest = jnp.where(lane == i1, -jnp.inf, logits)
    m2 = jnp.max(rest, axis=-1, keepdims=True)
    i2 = jnp.min(jnp.where(rest == m2, lane, LANES), axis=-1, keepdims=True)
    e2 = jnp.exp(m2 - m1)
    g1 = 1.0 / (1.0 + e2)
    g2 = e2 / (1.0 + e2)
    yield

    oh1 = lane == i1
    oh2 = lane == i2
    oh = jnp.where(oh1, 1.0, jnp.where(oh2, 1.0, 0.0))
    cnt = jnp.sum(oh, axis=0, keepdims=True)
    ngr = jnp.floor((cnt + (GROUP - 1)) * (1.0 / GROUP))
    ri = lax.broadcasted_iota(jnp.int32, (s, s), 0)
    ci = lax.broadcasted_iota(jnp.int32, (s, s), 1)
    rank = _dot(jnp.where(ri > ci, 1.0, 0.0).astype(BF16), oh.astype(BF16))
    li = lax.broadcasted_iota(jnp.int32, (LANES, LANES), 0)
    lj = lax.broadcasted_iota(jnp.int32, (LANES, LANES), 1)
    before = _dot(jnp.broadcast_to(ngr, (8, LANES)).astype(BF16),
                  jnp.where(li < lj, 1.0, 0.0).astype(BF16))[0:1]
    yield
    base = before * GROUP + rank
    base1 = jnp.where(oh1, base, 0.0)
    base2 = jnp.where(oh2, base, 0.0)
    pos1 = jnp.sum(base1, axis=-1, keepdims=True)
    pos2 = jnp.sum(base2, axis=-1, keepdims=True)

    def as_row(v):
        ones = jnp.ones((8, LANES), BF16)
        hi = jnp.floor(v * (1.0 / 256.0))
        return (_dot_nt(ones, hi.astype(BF16)) * 256.0
                + _dot_nt(ones, (v - hi * 256.0).astype(BF16)))[0:1]

    row1, row2 = as_row(base1), as_row(base2)
    yield
    r = lax.broadcasted_iota(jnp.int32, (SORT_CAP, s), 0).astype(F32)
    p1 = jnp.where(r == row1, 1.0, 0.0).astype(BF16)
    p2 = jnp.where(r == row2, 1.0, 0.0).astype(BF16)
    xs_ref[:, 0:D_MODEL] = _dot(p1 + p2, x_hi).astype(BF16)

    def gate_lanes(gv):
        hi = gv.astype(BF16).astype(F32)
        return jnp.where(lane == 0, hi, jnp.where(lane == 1, gv - hi, 0.0)).astype(BF16)

    xs_ref[:, D_MODEL:SORT_COLS] = (_dot(p1, gate_lanes(g1))
                                    + _dot(p2, gate_lanes(g2))).astype(BF16)
    pos_ref[...] = jnp.where(lane == 0, pos1, jnp.where(lane == 1, pos2, 0.0))
    ng_ref[...] = jnp.broadcast_to(ngr, (8, LANES)).astype(jnp.int32)


def _route_sort(h, g, wr_hi, wr_lo):
    tokens = h.shape[0]
    nt = tokens // SORT_TOKENS
    tps = math.gcd(nt, SORT_TILES_PER_STEP)
    steps = nt // tps
    last = lambda i: jnp.minimum(i, steps - 1)
    return pl.pallas_call(
        _route_sort_kernel,
        out_shape=(jax.ShapeDtypeStruct(((nt + tps) * SORT_CAP, SORT_COLS), BF16),
                   jax.ShapeDtypeStruct((tokens, LANES), F32),
                   jax.ShapeDtypeStruct((nt, 8, LANES), jnp.int32)),
        grid=(steps + 1,),
        in_specs=[pl.BlockSpec((tps * SORT_TOKENS, D_MODEL), lambda i: (last(i), 0)),
                  pl.BlockSpec((1, D_MODEL), lambda i: (0, 0)),
                  pl.BlockSpec((D_MODEL, LANES), lambda i: (0, 0)),
                  pl.BlockSpec((D_MODEL, LANES), lambda i: (0, 0))],
        out_specs=(pl.BlockSpec((tps * SORT_CAP, SORT_COLS), lambda i: (i, 0)),
                   pl.BlockSpec((tps * SORT_TOKENS, LANES), lambda i: (last(i), 0)),
                   pl.BlockSpec((tps, 8, LANES), lambda i: (last(i), 0, 0))),
        compiler_params=_params("arbitrary"),
        name="route_sort",
    )(h, g, wr_hi, wr_lo)


def _group_tables(ng, n_row_tiles):
    nt, ne = ng.shape
    gpt = GROUPS_PER_TILE
    n_e = jnp.sum(ng, axis=0)
    rt_e = (n_e + gpt - 1) // gpt
    rt_end = jnp.cumsum(rt_e)
    n_rt = rt_end[-1]
    r = jnp.arange(n_row_tiles, dtype=jnp.int32)
    valid_r = r < n_rt
    e_r = jnp.minimum(jnp.sum((r[:, None] >= rt_end[None, :]).astype(jnp.int32), axis=1), ne - 1)
    e_last = jnp.sum(jnp.where(r == n_rt - 1, e_r, 0))
    e_r = jnp.where(valid_r, e_r, e_last)
    sel = (e_r[:, None] == jnp.arange(ne, dtype=jnp.int32)[None, :]).astype(jnp.int32)
    pick = lambda v: jnp.sum(sel * v[None, :], axis=1)
    k0 = (r - pick(rt_end - rt_e)) * gpt
    gidx = k0[:, None] + jnp.arange(gpt, dtype=jnp.int32)[None, :]
    valid = valid_r[:, None] & (gidx < pick(n_e)[:, None])
    incl = jnp.cumsum(ng, axis=0)
    incl_r = jnp.sum(sel[:, None, :] * incl[None, :, :], axis=2)
    ng_r = jnp.sum(sel[:, None, :] * ng[None, :, :], axis=2)
    lo = jnp.cumsum(ng, axis=1) - ng
    lo_r = jnp.sum(sel[:, None, :] * lo[None, :, :], axis=2)
    ti = jnp.sum((incl_r[:, None, :] <= gidx[:, :, None]).astype(jnp.int32), axis=2)
    ti = jnp.minimum(ti, nt - 1)
    tsel = (ti[:, :, None] == jnp.arange(nt, dtype=jnp.int32)[None, None, :]).astype(jnp.int32)
    tpick = lambda v: jnp.sum(tsel * v[:, None, :], axis=2)
    within = gidx - tpick(incl_r - ng_r)
    rows = ti * SORT_CAP + GROUP * (tpick(lo_r) + within)
    spare = nt * SORT_CAP
    sink = (spare + SPARE_SINK + (r % 2)[:, None] * MOE_ROWS
            + GROUP * jnp.arange(gpt, dtype=jnp.int32)[None, :])
    src = jnp.where(valid, rows, spare + SPARE_ZERO).reshape(-1)
    dst = jnp.where(valid, rows, sink).reshape(-1)
    return e_r, jnp.sum(valid.astype(jnp.int32), axis=1), src, dst


def _group_copies(rows_ref, tile, hbm, buf, slot, sem, to_hbm, start):
    for k in range(GROUPS_PER_TILE):
        off = pl.multiple_of(rows_ref[tile * GROUPS_PER_TILE + k], GROUP)
        buf_rows = buf.at[slot, pl.ds(k * GROUP, GROUP), :]
        hbm_rows = hbm.at[pl.ds(off, GROUP), pl.ds(0, buf_rows.shape[-1])]
        cp = (pltpu.make_async_copy(buf_rows, hbm_rows, sem.at[slot]) if to_hbm
              else pltpu.make_async_copy(hbm_rows, buf_rows, sem.at[slot]))
        if start:
            cp.start()
        else:
            cp.wait()


def _moe_group_kernel(te_ref, tv_ref, src_ref, dst_ref, xs_hbm, w1_ref, w3_ref, w2_ref, ys_hbm,
                      xbuf, ybuf, acc_sc, xsem, ysem):
    r = pl.program_id(0)
    j = pl.program_id(1)
    nr = pl.num_programs(0)
    nj = pl.num_programs(1)
    slot = lax.rem(r, 2)
    valid = lambda t: tv_ref[jnp.clip(t, 0, nr - 1)] > 0
    gather = functools.partial(_group_copies, src_ref, hbm=xs_hbm, buf=xbuf, sem=xsem,
                               to_hbm=False)
    scatter = functools.partial(_group_copies, dst_ref, hbm=ys_hbm, buf=ybuf, sem=ysem,
                                to_hbm=True)

    @pl.when(j == 0)
    def _():
        @pl.when((r == 0) & valid(r))
        def _():
            gather(tile=r, slot=slot, start=True)

        @pl.when(valid(r))
        def _():
            gather(tile=r, slot=slot, start=False)

        @pl.when((r + 1 < nr) & valid(r + 1))
        def _():
            gather(tile=r + 1, slot=1 - slot, start=True)

        @pl.when((r >= 2) & valid(r - 2))
        def _():
            scatter(tile=r - 2, slot=slot, start=False)

    def compute(rows):
        x = xbuf[slot, rows, 0:D_MODEL]
        a = _dot(x, w1_ref[0])
        b = _dot(x, w3_ref[0])
        hid = (a * jax.nn.sigmoid(a) * b).astype(BF16)
        acc = jnp.where(j == 0, 0.0, acc_sc[rows, :]) + _dot(hid, w2_ref[0])
        acc_sc[rows, :] = acc
        gl = xbuf[slot, rows, D_MODEL:SORT_COLS].astype(F32)
        ybuf[slot, rows, :] = (acc * (gl[:, 0:1] + gl[:, 1:2])).astype(BF16)

    half = MOE_ROWS // 2
    full = tv_ref[r] > GROUPS_PER_TILE // 2

    @pl.when(full)
    def _():
        compute(slice(0, MOE_ROWS))

    @pl.when(valid(r) & jnp.logical_not(full))
    def _():
        compute(slice(0, half))

        @pl.when(j == nj - 1)
        def _():
            ybuf[slot, half:MOE_ROWS, :] = jnp.zeros((half, D_MODEL), BF16)

    @pl.when(valid(r) & (j == nj - 1))
    def _():
        scatter(tile=r, slot=slot, start=True)

    @pl.when((r == nr - 1) & (j == nj - 1))
    def _():
        @pl.when((r >= 1) & valid(r - 1))
        def _():
            scatter(tile=r - 1, slot=1 - slot, start=False)

        @pl.when(valid(r))
        def _():
            scatter(tile=r, slot=slot, start=False)


def _moe_group(xs, tile_expert, tile_valid, src, dst, w1, w3, w2, n_row_tiles, tf):
    ne, _, dff = w1.shape
    nj = dff // tf
    assert nj >= 2

    def w_in(r, j, te, tv, s, d):
        return (te[r], 0, jnp.where(tv[r] > 0, j, nj - 1))

    def w_out(r, j, te, tv, s, d):
        return (te[r], jnp.where(tv[r] > 0, j, nj - 1), 0)

    grid_spec = pltpu.PrefetchScalarGridSpec(
        num_scalar_prefetch=4,
        grid=(n_row_tiles, nj),
        in_specs=[pl.BlockSpec(memory_space=pl.ANY),
                  pl.BlockSpec((1, D_MODEL, tf), w_in),
                  pl.BlockSpec((1, D_MODEL, tf), w_in),
                  pl.BlockSpec((1, tf, D_MODEL), w_out)],
        out_specs=pl.BlockSpec(memory_space=pl.ANY),
        scratch_shapes=[pltpu.VMEM((2, MOE_ROWS, SORT_COLS), BF16),
                        pltpu.VMEM((2, MOE_ROWS, D_MODEL), BF16),
                        pltpu.VMEM((MOE_ROWS, D_MODEL), F32),
                        pltpu.SemaphoreType.DMA((2,)),
                        pltpu.SemaphoreType.DMA((2,))],
    )
    return pl.pallas_call(
        _moe_group_kernel,
        out_shape=jax.ShapeDtypeStruct(xs.shape, xs.dtype),
        grid_spec=grid_spec,
        input_output_aliases={4: 0},
        compiler_params=_params("arbitrary", "arbitrary"),
        name="moe_group",
    )(tile_expert, tile_valid, src, dst, xs, w1, w3, w2)


def _combine_kernel(h_ref, pos_ref, ys_ref, out_ref):
    pos = pos_ref[...]
    pt = _one_hot_rows(pos[:, 0:1], pos[:, 1:2], SORT_CAP)
    out_ref[...] = h_ref[...] + _dot(pt, ys_ref[...])


def _combine(h, pos, ys):
    tokens = h.shape[0]
    return pl.pallas_call(
        _combine_kernel,
        out_shape=jax.ShapeDtypeStruct((tokens, D_MODEL), F32),
        grid=(tokens // SORT_TOKENS,),
        in_specs=[pl.BlockSpec((SORT_TOKENS, D_MODEL), lambda i: (i, 0)),
                  pl.BlockSpec((SORT_TOKENS, LANES), lambda i: (i, 0)),
                  pl.BlockSpec((SORT_CAP, D_MODEL), lambda i: (i, 0))],
        out_specs=pl.BlockSpec((SORT_TOKENS, D_MODEL), lambda i: (i, 0)),
        compiler_params=_params("parallel"),
        name="combine",
    )(h, pos, ys)


def _row_tile(rows, want):
    tm = want
    while rows % tm:
        tm //= 2
    return tm


def _rope_tables(lp):
    inv_freq = 1.0 / (ROPE_THETA ** (jnp.arange(0, SWA_HEAD_DIM, 2, dtype=F32) / SWA_HEAD_DIM))
    row = jnp.arange(lp)
    seq = lp - BLOCK
    pos = jnp.where(row < seq, row + N_META, row - seq - FRONT).astype(F32)
    ang = pos[:, None] * inv_freq[None, :]
    cos, sin = jnp.cos(ang), jnp.sin(ang)
    zero = jnp.zeros_like(sin)
    rep = LANES // SWA_HEAD_DIM
    cos_t = jnp.tile(jnp.concatenate([cos, cos], axis=-1), (1, rep))
    sin_a = jnp.tile(jnp.concatenate([zero, sin], axis=-1), (1, rep))
    sin_b = jnp.tile(jnp.concatenate([-sin, zero], axis=-1), (1, rep))
    return cos_t, sin_a, sin_b


def kernel(x, meta, a_norm, a_w_in, a_w_gate2, a_b_gate, a_q_norm, a_k_norm, a_sinks, a_o_norm,
           a_w_out, f_norm, f_w1, f_w3, f_w2, c_norm, c_w_pw1, c_w_dw, c_b_dw, c_ln_g, c_ln_b,
           c_w_pw2, m_norm, m_w_router, m_w1, m_w3, m_w2):
    batch, seq, _ = x.shape
    assert seq % BLOCK == 0
    lp = BLOCK + seq
    nb = lp // BLOCK
    rows = batch * lp

    head = jnp.concatenate([jnp.zeros((FRONT, D_MODEL), x.dtype), meta.astype(x.dtype)],
                           axis=0)[None]
    tm = _row_tile(seq, 512)

    w_in = a_w_in[0]
    gq, gk, gv, gr, glr, sq, sk, sv = jnp.split(
        w_in, [256, 512, 1024, 1536, 1552, 2064, 2192], axis=-1)
    w_in_r = jnp.concatenate(
        [gq, gk, gv, gr, sq, sk, sv, glr,
         jnp.zeros((D_MODEL, Z_COLS - Z_LR - GLA_LOWRANK), w_in.dtype)], axis=-1).astype(BF16)
    lane = jnp.arange(LANES)
    bd = (lane[:, None] // SWA_HEAD_DIM == lane[None, :] // SWA_HEAD_DIM).astype(BF16)
    rope = _rope_tables(lp) + (jnp.tile(a_q_norm[0], SWA_Q_HEADS)[None],
                               jnp.tile(a_k_norm[0], SWA_KV_HEADS)[None], bd)
    z = _in_proj(x, a_norm[0][None], w_in_r, rope, batch, lp, _row_tile(seq, 1024), 0)
    z = _in_proj(head, a_norm[0][None], w_in_r, rope, batch, lp, BLOCK, seq, z)

    wg = jnp.zeros((LANES, GLA_QK), F32).at[:GLA_LOWRANK].set(a_w_gate2[0]).astype(BF16)
    bb = math.gcd(batch, MIX_BATCH)
    o_a = _gla(z, wg, a_b_gate[0][None], a_o_norm[0][None], bb)
    o_b, d_w1, d_w3, d_w2 = _swa(z, a_sinks[0], (f_w1[0], f_w3[0], f_w2[0]), bb)

    w_out = a_w_out[0].astype(BF16)
    ffn = (w_out, f_norm[0][None], d_w1, d_w3, d_w2)
    h = _mix_out_ffn(x, o_a, o_b, *ffn, tm, FF_TILE, 0)
    h = _mix_out_ffn(head, o_a, o_b, *ffn, BLOCK, FF_TILE, seq, h)

    si = jnp.arange((SUBLANES - 1) * CONV_SPAN)
    shift = (jnp.arange(CONV_SPAN)[None, :]
             == (si % CONV_SPAN + si // CONV_SPAN + 1)[:, None]).astype(BF16)
    ne, _, dff = m_w1[0].shape
    experts = (m_w1[0].reshape(ne * D_MODEL, dff), m_w3[0].reshape(ne * D_MODEL, dff),
               m_w2[0].reshape(ne * dff, D_MODEL))
    h, e_w1, e_w3, e_w2 = _conv_module(
        h, c_norm[0][None], c_w_pw1[0].astype(BF16), shift,
        jnp.broadcast_to(c_w_dw[0][:, None, :], (CONV_WIDTH, SUBLANES, D_MODEL)),
        c_b_dw[0][None], c_ln_g[0][None], c_ln_b[0][None], c_w_pw2[0].astype(BF16), experts)
    h = h.reshape(batch * seq, D_MODEL)

    wr = jnp.zeros((D_MODEL, LANES), F32).at[:, :N_EXPERTS].set(m_w_router[0])
    wr_hi = wr.astype(BF16)
    wr_lo = (wr - wr_hi.astype(F32)).astype(BF16)
    tokens = batch * seq
    assert tokens % SORT_TOKENS == 0
    nt = tokens // SORT_TOKENS
    xs, pos, ng = _route_sort(h, m_norm[0][None], wr_hi, wr_lo)
    ng = ng[:, 0, :N_EXPERTS]
    n_row_tiles = (nt * (SORT_CAP // GROUP)) // GROUPS_PER_TILE + N_EXPERTS
    tile_expert, tile_valid, src, dst = _group_tables(ng, n_row_tiles)
    ys = _moe_group(xs, tile_expert, tile_valid, src, dst, e_w1.reshape(ne, D_MODEL, dff),
                    e_w3.reshape(ne, D_MODEL, dff), e_w2.reshape(ne, dff, D_MODEL), n_row_tiles,
                    FF_TILE)
    out = _combine(h, pos, ys)
    return out.reshape(batch, seq, D_MODEL)
```

```python
import functools
import math

import jax
import jax.numpy as jnp
from jax import lax
from jax.experimental import pallas as pl
from jax.experimental.pallas import tpu as pltpu

F32 = jnp.float32
BF16 = jnp.bfloat16

D_MODEL = 1024
N_META = 16
EPS = 1e-6
ROPE_THETA = 10000.0
NEG_INF = -1e30
LOG2E = math.log2(math.e)
GLA_HEADS = 4
GLA_DK = 64
GLA_DV = 128
GLA_LOWRANK = 16
GLA_TAU = 16.0
GLA_CHUNK = 64
SWA_Q_HEADS = 8
SWA_KV_HEADS = 2
SWA_HEAD_DIM = 64
SWA_WINDOW = 128
CONV_WIDTH = 31
D_FF = 3584
N_EXPERTS = 8

GLA_QK = GLA_HEADS * GLA_DK
GLA_V = GLA_HEADS * GLA_DV
SWA_Q = SWA_Q_HEADS * SWA_HEAD_DIM
SWA_KV = SWA_KV_HEADS * SWA_HEAD_DIM
SWA_GROUP = SWA_Q_HEADS // SWA_KV_HEADS

LANES = 128
FF_TILE = 1792
BLOCK = 128
FRONT = BLOCK - N_META
VMEM_LIMIT = 56 * 1024 * 1024

Z_Q, Z_K, Z_V, Z_R = 0, 256, 512, 1024
Z_SQ, Z_SK, Z_SV, Z_LR = 1536, 2048, 2176, 2304
Z_COLS = 2432


def _rms(x, g):
    return x * lax.rsqrt(jnp.mean(x * x, axis=-1, keepdims=True) + EPS) * g


def _dot(a, b):
    return jnp.dot(a, b, preferred_element_type=F32)


def _dot_nt(a, b):
    return lax.dot_general(a, b, (((1,), (1,)), ((), ())), preferred_element_type=F32)


def _dot_tn(a, b):
    return lax.dot_general(a, b, (((0,), (0,)), ((), ())), preferred_element_type=F32)


def _stored_block(n, lp):
    return jnp.where(n == 0, lp // BLOCK - 1, n - 1)


def _params(*sem):
    return pltpu.CompilerParams(dimension_semantics=sem, vmem_limit_bytes=VMEM_LIMIT)


IN_PROJ_ROWS = 256


def _in_proj_kernel(h_ref, g_ref, w_ref, cos_ref, sa_ref, sb_ref, qn_ref, kn_ref, bd_ref,
                    *rest):
    z_ref = rest[-1]
    tile = lambda t: jnp.concatenate([t] * (SWA_Q // LANES), axis=-1)
    tm = h_ref.shape[0]
    chunks = [slice(r, min(r + IN_PROJ_ROWS, tm)) for r in range(0, tm, IN_PROJ_ROWS)]
    project = lambda rs: _dot(_rms(h_ref[rs, :], g_ref[...]).astype(BF16), w_ref[...])
    zs = {0: project(chunks[0])}
    for i, rs in enumerate(chunks):
        if i + 1 < len(chunks):
            zs[i + 1] = project(chunks[i + 1])
        z = zs.pop(i)
        cos, sa, sb = cos_ref[rs, :], sa_ref[rs, :], sb_ref[rs, :]
        q = _norm_rope(z[None, :, Z_SQ:Z_SQ + SWA_Q], qn_ref[...], tile(cos), tile(sa),
                       tile(sb), bd_ref[...])[0] * (SWA_HEAD_DIM ** -0.5 * LOG2E)
        k = _norm_rope(z[None, :, Z_SK:Z_SK + SWA_KV], kn_ref[...], cos, sa, sb, bd_ref[...])[0]
        z = jnp.concatenate([z[:, :Z_SQ], q, k, z[:, Z_SV:]], axis=-1).astype(z_ref.dtype)
        if z_ref.ndim == 2:
            z_ref[rs, :] = z
        else:
            z_ref[:, rs, :] = jnp.broadcast_to(z[None], (z_ref.shape[0],) + z.shape)


def _in_proj(h, g, w, rope, batch, lp, tm, first_row, z=None):
    hb, rows, _ = h.shape
    n = w.shape[1]
    shared = hb == 1 and batch > 1
    const = lambda a: pl.BlockSpec(a.shape, lambda b, j: (0,) * a.ndim)
    table = pl.BlockSpec((tm, LANES), lambda b, j: (first_row // tm + j, 0))
    in_specs = [pl.BlockSpec((None, tm, D_MODEL), lambda b, j: (b, j, 0)),
                pl.BlockSpec((1, D_MODEL), lambda b, j: (0, 0)),
                pl.BlockSpec((D_MODEL, n), lambda b, j: (0, 0)),
                table, table, table, const(rope[3]), const(rope[4]), const(rope[5])]
    args = (h, g, w) + tuple(rope)
    if z is not None:
        in_specs.append(pl.BlockSpec(memory_space=pl.ANY))
        args += (z,)
    return pl.pallas_call(
        _in_proj_kernel,
        out_shape=jax.ShapeDtypeStruct((batch, lp, n), BF16),
        grid=(hb, rows // tm),
        in_specs=in_specs,
        out_specs=pl.BlockSpec((batch if shared else None, tm, n),
                               lambda b, j: (b, first_row // tm + j, 0)),
        input_output_aliases={} if z is None else {len(args) - 1: 0},
        compiler_params=_params("parallel", "parallel"),
        name="in_proj",
    )(*args)


MIX_BATCH = 8


def _gla_kernel(q_ref, k_ref, v_ref, r_ref, lr_ref, wg_ref, bg_ref, on_ref, o_ref, st_ref):
    n = pl.program_id(1)

    @pl.when(n == 0)
    def _():
        st_ref[...] = jnp.zeros_like(st_ref)

    c = GLA_CHUNK
    items = [(bi, ch) for bi in range(q_ref.shape[0]) for ch in range(BLOCK // c)]
    ri = lax.broadcasted_iota(jnp.int32, (BLOCK, BLOCK), 0)
    ci = lax.broadcasted_iota(jnp.int32, (BLOCK, BLOCK), 1)
    tri = jnp.where((ri >= ci) & (ri // c == ci // c), 1.0, 0.0).astype(BF16)
    causal = (lax.broadcasted_iota(jnp.int32, (c, c), 0)
              >= lax.broadcasted_iota(jnp.int32, (c, c), 1))
    row = lax.broadcasted_iota(jnp.int32, (BLOCK, GLA_QK), 0) + n * BLOCK
    on = on_ref[...]

    b_all = []
    for bi in range(q_ref.shape[0]):
        gp = _dot(lr_ref[bi], wg_ref[...]) + bg_ref[...]
        g = (jnp.minimum(gp, 0.0) - jnp.log(1.0 + jnp.exp(-jnp.abs(gp)))) * (1.0 / GLA_TAU)
        g = jnp.where(row >= FRONT, g, 0.0)
        g_hi = g.astype(BF16)
        g_lo = (g - g_hi.astype(F32)).astype(BF16)
        b_all.append(_dot(tri, g_hi) + _dot(tri, g_lo))

    q_t, k_t, k_d, dec = {}, {}, {}, {}
    for bi, ch in items:
        rs = slice(ch * c, (ch + 1) * c)
        b = b_all[bi][rs]
        b_last = b_all[bi][(ch + 1) * c - 1:(ch + 1) * c, :]
        qf = q_ref[bi, rs, :].astype(F32)
        kf = k_ref[bi, rs, :].astype(F32)
        q_t[bi, ch] = (qf * jnp.exp(b) * (GLA_DK ** -0.5)).astype(BF16)
        k_t[bi, ch] = (kf * jnp.exp(-b)).astype(BF16)
        k_d[bi, ch] = (kf * jnp.exp(b_last - b)).astype(BF16)
        dec[bi, ch] = jnp.exp(b_last)

    att, kv = {}, {}
    for bi, ch in items:
        rs = slice(ch * c, (ch + 1) * c)
        for hd in range(GLA_HEADS):
            ks = slice(hd * GLA_DK, (hd + 1) * GLA_DK)
            vh = v_ref[bi, rs, hd * GLA_DV:(hd + 1) * GLA_DV]
            att[bi, ch, hd] = jnp.where(
                causal, _dot_nt(q_t[bi, ch][:, ks], k_t[bi, ch][:, ks]), 0.0).astype(BF16)
            kv[bi, ch, hd] = _dot_tn(vh, k_d[bi, ch][:, ks])

    for bi in range(q_ref.shape[0]):
        for hd in range(GLA_HEADS):
            ks = slice(hd * GLA_DK, (hd + 1) * GLA_DK)
            vs = slice(hd * GLA_DV, (hd + 1) * GLA_DV)
            st = st_ref[bi, hd]
            for ch in range(BLOCK // c):
                rs = slice(ch * c, (ch + 1) * c)
                o = (_dot(att[bi, ch, hd], v_ref[bi, rs, vs])
                     + _dot_nt(q_t[bi, ch][:, ks], st.astype(BF16)))
                st = st * dec[bi, ch][:, ks] + kv[bi, ch, hd]
                o = _rms(o, on)
                r = r_ref[bi, rs, vs].astype(F32)
                o_ref[bi, rs, vs] = (o * (r * jax.nn.sigmoid(r))).astype(o_ref.dtype)
            st_ref[bi, hd] = st


def _gla(z, wg, bg, on, bb):
    batch, lp, _ = z.shape

    def col(width, start):
        return pl.BlockSpec((bb, BLOCK, width), lambda b, n: (b, _stored_block(n, lp), start // width))

    const = lambda shape: pl.BlockSpec(shape, lambda b, n: (0,) * len(shape))
    return pl.pallas_call(
        _gla_kernel,
        out_shape=jax.ShapeDtypeStruct((batch, lp, GLA_V), BF16),
        grid=(batch // bb, lp // BLOCK),
        in_specs=[col(GLA_QK, Z_Q), col(GLA_QK, Z_K), col(GLA_V, Z_V), col(GLA_V, Z_R),
                  col(LANES, Z_LR), const((LANES, GLA_QK)), const((1, GLA_QK)),
                  const((1, GLA_DV))],
        out_specs=pl.BlockSpec((bb, BLOCK, GLA_V), lambda b, n: (b, _stored_block(n, lp), 0)),
        scratch_shapes=[pltpu.VMEM((bb, GLA_HEADS, GLA_DV, GLA_DK), F32)],
        compiler_params=_params("parallel", "arbitrary"),
        name="gla",
    )(z, z, z, z, z, wg, bg, on)


def _group_mean_sq(x, bd):
    sq = x * x
    hi = sq.astype(BF16)
    lo = (sq - hi.astype(F32)).astype(BF16)
    return _dot(hi, bd) + _dot(lo, bd)


def _norm_rope(x, gain, cos, sin_a, sin_b, bd):
    nbat, rows, width = x.shape
    x2 = x.reshape(nbat * rows, width)
    parts = []
    for s in range(width // LANES):
        xs = x2[:, s * LANES:(s + 1) * LANES]
        ms = _group_mean_sq(xs, bd) * (1.0 / SWA_HEAD_DIM)
        parts.append(xs * lax.rsqrt(ms + EPS))
    xn = (parts[0] if len(parts) == 1 else jnp.concatenate(parts, axis=-1)) * gain
    half = SWA_HEAD_DIM // 2
    back = lambda t: t.reshape(nbat, rows, width)
    return (back(xn) * cos + back(pltpu.roll(xn, half, 1)) * sin_a
            + back(pltpu.roll(xn, width - half, 1)) * sin_b)


def _swa_kernel(sinks_ref, q_ref, kc_ref, kp_ref, vc_ref, vp_ref, f1_ref, f3_ref, f2_ref,
                o_ref, f1b_ref, f3b_ref, f2b_ref, i1_sc, i3_sc, i2_sc, o1_sc, o3_sc, o2_sc,
                in_sem, out_sem, *, cast_steps):
    n = pl.program_id(1)
    nbat = q_ref.shape[0]

    cast_step = pl.program_id(0) * pl.num_programs(1) + n

    @pl.when(cast_step < cast_steps)
    def _():
        _cast_stream(cast_step, cast_steps, (f1_ref, f3_ref, f2_ref),
                     (f1b_ref, f3b_ref, f2b_ref), (i1_sc, i3_sc, i2_sc), (o1_sc, o3_sc, o2_sc),
                     in_sem, out_sem)

    q = q_ref[...]
    k_c = kc_ref[...]
    k_p = kp_ref[...]

    gq = SWA_GROUP * BLOCK
    r = lax.broadcasted_iota(jnp.int32, (gq, 2 * BLOCK), 0) & (BLOCK - 1)
    cidx = lax.broadcasted_iota(jnp.int32, (gq, 2 * BLOCK), 1)
    allowed = (cidx > r) & (cidx <= r + SWA_WINDOW) & ((n - 1) * BLOCK + cidx >= FRONT)
    rgroup = lax.broadcasted_iota(jnp.int32, (gq, 1), 0) // BLOCK
    items = [(bi, hk) for bi in range(nbat) for hk in range(SWA_KV_HEADS)]

    s, sink = {}, {}
    for hk in range(SWA_KV_HEADS):
        sk = jnp.zeros((gq, 1), F32)
        for g in range(SWA_GROUP):
            sk = jnp.where(rgroup == g, sinks_ref[hk * SWA_GROUP + g] * LOG2E, sk)
        sink[hk] = sk
    for bi, hk in items:
        ks = slice(hk * SWA_HEAD_DIM, (hk + 1) * SWA_HEAD_DIM)
        qg = jnp.concatenate(
            [q[bi][:, (hk * SWA_GROUP + g) * SWA_HEAD_DIM:(hk * SWA_GROUP + g + 1) * SWA_HEAD_DIM]
             for g in range(SWA_GROUP)], axis=0)
        k = jnp.concatenate([k_p[bi][:, ks], k_c[bi][:, ks]], axis=0)
        s[bi, hk] = jnp.where(allowed, _dot_nt(qg, k), NEG_INF)

    p, denom = {}, {}
    for bi, hk in items:
        m = jnp.maximum(jnp.max(s[bi, hk], axis=-1, keepdims=True), sink[hk])
        e = jnp.exp2(s[bi, hk] - m)
        denom[bi, hk] = jnp.sum(e, axis=-1, keepdims=True) + jnp.exp2(sink[hk] - m)
        p[bi, hk] = e.astype(BF16)

    for bi, hk in items:
        ks = slice(hk * SWA_HEAD_DIM, (hk + 1) * SWA_HEAD_DIM)
        v = jnp.concatenate([vp_ref[bi, :, ks], vc_ref[bi, :, ks]], axis=0)
        o = _dot(p[bi, hk], v) / denom[bi, hk]
        for g in range(SWA_GROUP):
            hq = hk * SWA_GROUP + g
            o_ref[bi, :, hq * SWA_HEAD_DIM:(hq + 1) * SWA_HEAD_DIM] = (
                o[g * BLOCK:(g + 1) * BLOCK].astype(o_ref.dtype))


def _swa(z, sinks, ffn, bb):
    batch, lp, _ = z.shape
    steps = (batch // bb) * (lp // BLOCK)
    cast_steps = 1 << (steps.bit_length() - 1)
    assert all(w.shape[0] % (cast_steps * GROUP) == 0 for w in ffn)
    chunk = lambda w: (2, w.shape[0] // cast_steps, w.shape[1])
    any_spec = pl.BlockSpec(memory_space=pl.ANY)

    def cur(width, start):
        return pl.BlockSpec((bb, BLOCK, width),
                            lambda b, n, s: (b, _stored_block(n, lp), start // width))

    def prev(width, start):
        return pl.BlockSpec((bb, BLOCK, width),
                            lambda b, n, s: (b, _stored_block(jnp.maximum(n - 1, 0), lp),
                                             start // width))

    grid_spec = pltpu.PrefetchScalarGridSpec(
        num_scalar_prefetch=1,
        grid=(batch // bb, lp // BLOCK),
        in_specs=[cur(SWA_Q, Z_SQ), cur(SWA_KV, Z_SK), prev(SWA_KV, Z_SK),
                  cur(SWA_KV, Z_SV), prev(SWA_KV, Z_SV),
                  any_spec, any_spec, any_spec],
        out_specs=(pl.BlockSpec((bb, BLOCK, SWA_Q),
                                lambda b, n, s: (b, _stored_block(n, lp), 0)),
                   any_spec, any_spec, any_spec),
        scratch_shapes=[pltpu.VMEM(chunk(w), F32) for w in ffn]
        + [pltpu.VMEM(chunk(w), BF16) for w in ffn]
        + [pltpu.SemaphoreType.DMA((2,)), pltpu.SemaphoreType.DMA((2,))],
    )
    return pl.pallas_call(
        functools.partial(_swa_kernel, cast_steps=cast_steps),
        out_shape=(jax.ShapeDtypeStruct((batch, lp, SWA_Q), BF16),)
        + tuple(jax.ShapeDtypeStruct(w.shape, BF16) for w in ffn),
        grid_spec=grid_spec,
        compiler_params=_params("arbitrary", "arbitrary"),
        name="swa",
    )(sinks, z, z, z, z, z, *ffn)


def _ffn_kernel(h_ref, oa_ref, ob_ref, woa_ref, wob_ref, g_ref, w1_ref, w3_ref, w2_ref,
                *rest):
    out_ref, h1_sc, xn_sc, acc_sc = rest[-4:]
    j = pl.program_id(2)

    @pl.when(j == 0)
    def _():
        h1 = h_ref[...] + _dot(oa_ref[...], woa_ref[...]) + _dot(ob_ref[...], wob_ref[...])
        h1_sc[...] = h1
        xn_sc[...] = _rms(h1, g_ref[...]).astype(BF16)

    x = xn_sc[...]
    a = _dot(x, w1_ref[...])
    b = _dot(x, w3_ref[...])
    hid = (a * jax.nn.sigmoid(a) * b).astype(BF16)
    acc = jnp.where(j == 0, h1_sc[...], acc_sc[...]) + _dot(hid, w2_ref[...])
    acc_sc[...] = acc
    out_ref[...] = acc


def _mix_out_ffn(h, oa, ob, w_out, g, w1, w3, w2, tm, tf, first_row, out=None):
    hb, rows, _ = h.shape
    batch, lp, _ = oa.shape
    dff = w1.shape[1]
    row0 = first_row // tm
    const = lambda shape: pl.BlockSpec(shape, lambda b, i, j: (0, 0))
    in_specs = [pl.BlockSpec((None, tm, D_MODEL),
                             lambda b, i, j: (jnp.minimum(b, hb - 1), i, 0)),
                pl.BlockSpec((None, tm, GLA_V), lambda b, i, j: (b, row0 + i, 0)),
                pl.BlockSpec((None, tm, SWA_Q), lambda b, i, j: (b, row0 + i, 0)),
                pl.BlockSpec((GLA_V, D_MODEL), lambda b, i, j: (0, 0)),
                pl.BlockSpec((SWA_Q, D_MODEL), lambda b, i, j: (GLA_V // SWA_Q, 0)),
                const((1, D_MODEL)),
                pl.BlockSpec((D_MODEL, tf), lambda b, i, j: (0, j)),
                pl.BlockSpec((D_MODEL, tf), lambda b, i, j: (0, j)),
                pl.BlockSpec((tf, D_MODEL), lambda b, i, j: (j, 0))]
    assert GLA_V % SWA_Q == 0
    args = (h, oa, ob, w_out, w_out, g, w1, w3, w2)
    if out is not None:
        in_specs.append(pl.BlockSpec(memory_space=pl.ANY))
        args += (out,)
    return pl.pallas_call(
        _ffn_kernel,
        out_shape=jax.ShapeDtypeStruct((batch, lp, D_MODEL), F32),
        grid=(hb, rows // tm, dff // tf),
        in_specs=in_specs,
        out_specs=pl.BlockSpec((None, tm, D_MODEL), lambda b, i, j: (b, row0 + i, 0)),
        scratch_shapes=[pltpu.VMEM((tm, D_MODEL), F32), pltpu.VMEM((tm, D_MODEL), BF16),
                        pltpu.VMEM((tm, D_MODEL), F32)],
        input_output_aliases={} if out is None else {9: 0},
        compiler_params=_params("parallel", "parallel", "arbitrary"),
        name="mix_out_ffn",
    )(*args)


CONV_BLOCK = BLOCK
CONV_HALO = 32
CONV_SPAN = CONV_HALO + CONV_BLOCK
CONV_ROWS = 32
SUBLANES = 8


def _cast_stream(step, n_steps, srcs, dsts, in_bufs, out_bufs, in_sem, out_sem):
    slot = lax.rem(step, 2)

    def copies(s, sl, fetch):
        out = []
        for src, dst, ibuf, obuf in zip(srcs, dsts, in_bufs, out_bufs):
            rows = ibuf.shape[1]
            at = pl.ds(pl.multiple_of(s * rows, rows), rows)
            out.append(pltpu.make_async_copy(src.at[at], ibuf.at[sl], in_sem.at[sl]) if fetch
                       else pltpu.make_async_copy(obuf.at[sl], dst.at[at], out_sem.at[sl]))
        return out

    @pl.when(step == 0)
    def _():
        for cp in copies(step, slot, True):
            cp.start()

    for cp in copies(step, slot, True):
        cp.wait()

    @pl.when(step + 1 < n_steps)
    def _():
        for cp in copies(step + 1, 1 - slot, True):
            cp.start()

    @pl.when(step >= 2)
    def _():
        for cp in copies(step - 2, slot, False):
            cp.wait()

    for ibuf, obuf in zip(in_bufs, out_bufs):
        obuf[slot] = ibuf[slot].astype(BF16)
    for cp in copies(step, slot, False):
        cp.start()

    @pl.when(step == n_steps - 1)
    def _():
        @pl.when(step >= 1)
        def _():
            for cp in copies(step - 1, 1 - slot, False):
                cp.wait()

        for cp in copies(step, slot, False):
            cp.wait()


def _conv_kernel(h_ref, hm_ref, g_ref, wa_ref, wg_ref, shift_ref, wdw_ref, bdw_ref, lg_ref, lb_ref,
                 w2_ref, e1_ref, e3_ref, e2_ref, out_ref, e1b_ref, e3b_ref, e2b_ref,
                 u_sc, xs_sc, c_sc, y_sc, i1_sc, i3_sc, i2_sc, o1_sc, o3_sc, o2_sc, in_sem,
                 out_sem):
    n = pl.program_id(1)
    _cast_stream(pl.program_id(0) * pl.num_programs(1) + n,
                 pl.num_programs(0) * pl.num_programs(1), (e1_ref, e3_ref, e2_ref),
                 (e1b_ref, e3b_ref, e2b_ref), (i1_sc, i3_sc, i2_sc), (o1_sc, o3_sc, o2_sc),
                 in_sem, out_sem)

    def glu(x):
        xn = _rms(x, g_ref[...]).astype(BF16)
        return (_dot(xn, wa_ref[...]) * jax.nn.sigmoid(_dot(xn, wg_ref[...]))).astype(BF16)

    @pl.when(n == 0)
    def _():
        u_sc[0:CONV_HALO, :] = glu(hm_ref[BLOCK - CONV_HALO:BLOCK, :])

    h = h_ref[...]
    u_sc[CONV_HALO:CONV_SPAN, :] = glu(h)
    span = u_sc[...]
    xs_sc[0] = span.astype(F32)
    shifted = _dot(shift_ref[...], span)
    for r in range(1, SUBLANES):
        xs_sc[r] = shifted[(r - 1) * CONV_SPAN:r * CONV_SPAN]
    first = CONV_HALO - (CONV_WIDTH - 1)
    for part in range(CONV_BLOCK // BLOCK):
        for cb in range(D_MODEL // LANES):
            cs = slice(cb * LANES, (cb + 1) * LANES)
            acc = jnp.zeros((BLOCK // SUBLANES, SUBLANES, LANES), F32)
            for j in range(CONV_WIDTH):
                r = (first + j) % SUBLANES
                base = first + j - r + part * BLOCK
                x = xs_sc[r, base:base + BLOCK, cs].reshape(BLOCK // SUBLANES, SUBLANES, LANES)
                acc = acc + x * wdw_ref[j, :, cs]
            c_sc[part * BLOCK:(part + 1) * BLOCK, cs] = acc.reshape(BLOCK, LANES)
    for rb in range(CONV_BLOCK // CONV_ROWS):
        acc = c_sc[rb * CONV_ROWS:(rb + 1) * CONV_ROWS, :] + bdw_ref[...]
        mu = jnp.mean(acc, axis=-1, keepdims=True)
        dev = acc - mu
        var = jnp.mean(dev * dev, axis=-1, keepdims=True)
        y = dev * lax.rsqrt(var + EPS) * lg_ref[...] + lb_ref[...]
        y_sc[rb * CONV_ROWS:(rb + 1) * CONV_ROWS, :] = (y * jax.nn.sigmoid(y)).astype(BF16)
    out_ref[...] = h + _dot(y_sc[...], w2_ref[...])
    u_sc[0:CONV_HALO, :] = u_sc[CONV_BLOCK:CONV_SPAN, :]


def _conv_module(h, g, w_pw1, shift, wdw, bdw, lg, lb, w2, experts):
    batch, lp, _ = h.shape
    seq = lp - BLOCK
    const = lambda shape: pl.BlockSpec(shape, lambda b, n: (0,) * len(shape))
    any_spec = pl.BlockSpec(memory_space=pl.ANY)
    n_chunks = batch * (seq // CONV_BLOCK)
    chunk = lambda w: (2, w.shape[0] // n_chunks, w.shape[1])
    assert seq % CONV_BLOCK == 0
    assert all(w.shape[0] % (n_chunks * GROUP) == 0 for w in experts)
    return pl.pallas_call(
        _conv_kernel,
        out_shape=(jax.ShapeDtypeStruct((batch, seq, D_MODEL), F32),)
        + tuple(jax.ShapeDtypeStruct(w.shape, BF16) for w in experts),
        grid=(batch, seq // CONV_BLOCK),
        in_specs=[pl.BlockSpec((None, CONV_BLOCK, D_MODEL), lambda b, n: (b, n, 0)),
                  pl.BlockSpec((None, BLOCK, D_MODEL), lambda b, n: (0, seq // BLOCK, 0)),
                  const((1, D_MODEL)),
                  pl.BlockSpec((D_MODEL, D_MODEL), lambda b, n: (0, 0)),
                  pl.BlockSpec((D_MODEL, D_MODEL), lambda b, n: (0, 1)),
                  const(((SUBLANES - 1) * CONV_SPAN, CONV_SPAN)),
                  const((CONV_WIDTH, SUBLANES, D_MODEL)), const((1, D_MODEL)), const((1, D_MODEL)),
                  const((1, D_MODEL)), const((D_MODEL, D_MODEL)),
                  any_spec, any_spec, any_spec],
        out_specs=(pl.BlockSpec((None, CONV_BLOCK, D_MODEL), lambda b, n: (b, n, 0)),
                   any_spec, any_spec, any_spec),
        scratch_shapes=[pltpu.VMEM((CONV_SPAN, D_MODEL), BF16),
                        pltpu.VMEM((SUBLANES, CONV_SPAN, D_MODEL), F32),
                        pltpu.VMEM((CONV_BLOCK, D_MODEL), F32),
                        pltpu.VMEM((CONV_BLOCK, D_MODEL), BF16)]
        + [pltpu.VMEM(chunk(w), F32) for w in experts]
        + [pltpu.VMEM(chunk(w), BF16) for w in experts]
        + [pltpu.SemaphoreType.DMA((2,)), pltpu.SemaphoreType.DMA((2,))],
        compiler_params=_params("arbitrary", "arbitrary"),
        name="conv_module",
    )(h, h, g, w_pw1, w_pw1, shift, wdw, bdw, lg, lb, w2, *experts)


SORT_TOKENS = 512
GROUP = 16
SORT_CAP = 2 * SORT_TOKENS + N_EXPERTS * GROUP
SORT_COLS = D_MODEL + LANES
MOE_ROWS = 512
GROUPS_PER_TILE = MOE_ROWS // GROUP


def _one_hot_rows(pos1, pos2, width):
    r = lax.broadcasted_iota(jnp.int32, (pos1.shape[0], width), 1).astype(F32)
    return jnp.where(r == pos1, 1.0, jnp.where(r == pos2, 1.0, 0.0)).astype(BF16)


SORT_TILES_PER_STEP = 4
SPARE_ZERO = 0
SPARE_SINK = 64
assert SPARE_SINK + 2 * MOE_ROWS <= SORT_CAP


def _route_sort_kernel(h_ref, g_ref, wr_ref, xs_ref, pos_ref, ng_ref):
    is_spare = pl.program_id(0) == pl.num_programs(0) - 1

    @pl.when(is_spare)
    def _():
        xs_ref[...] = jnp.zeros_like(xs_ref)

    @pl.when(jnp.logical_not(is_spare))
    def _():
        tiles = [_route_sort_tile(h_ref.at[pl.ds(t * SORT_TOKENS, SORT_TOKENS)], g_ref, wr_ref,
                                  xs_ref.at[pl.ds(t * SORT_CAP, SORT_CAP)],
                                  pos_ref.at[pl.ds(t * SORT_TOKENS, SORT_TOKENS)], ng_ref.at[t])
                 for t in range(h_ref.shape[0] // SORT_TOKENS)]
        while tiles:
            tiles = [t for t in tiles if next(t, "done") != "done"]


def _route_sort_tile(h_ref, g_ref, wr_ref, xs_ref, pos_ref, ng_ref):
    s = SORT_TOKENS
    xn = _rms(h_ref[...], g_ref[...])
    x_hi = xn.astype(BF16)
    x_lo = (xn - x_hi.astype(F32)).astype(BF16)
    both = _dot(x_hi, wr_ref[...])
    logits = (both[:, 0:LANES] + both[:, LANES:2 * LANES]
              + _dot(x_lo, wr_ref[:, 0:LANES]))
    yield
    lane = lax.broadcasted_iota(jnp.int32, logits.shape, 1)
    logits = jnp.where(lane < N_EXPERTS, logits, -jnp.inf)
    m1 = jnp.max(logits, axis=-1, keepdims=True)
    i1 = jnp.min(jnp.where(logits == m1, lane, LANES), axis=-1, keepdims=True)
    rest = jnp.where(lane == i1, -jnp.inf, logits)
    m2 = jnp.max(rest, axis=-1, keepdims=True)
    i2 = jnp.min(jnp.where(rest == m2, lane, LANES), axis=-1, keepdims=True)
    e2 = jnp.exp(m2 - m1)
    g1 = 1.0 / (1.0 + e2)
    g2 = e2 / (1.0 + e2)
    yield

    oh1 = lane == i1
    oh2 = lane == i2
    oh = jnp.where(oh1, 1.0, jnp.where(oh2, 1.0, 0.0))
    cnt = jnp.sum(oh, axis=0, keepdims=True)
    ngr = jnp.floor((cnt + (GROUP - 1)) * (1.0 / GROUP))
    ri = lax.broadcasted_iota(jnp.int32, (s, s), 0)
    ci = lax.broadcasted_iota(jnp.int32, (s, s), 1)
    rank = _dot(jnp.where(ri > ci, 1.0, 0.0).astype(BF16), oh.astype(BF16))
    li = lax.broadcasted_iota(jnp.int32, (LANES, LANES), 0)
    lj = lax.broadcasted_iota(jnp.int32, (LANES, LANES), 1)
    before = _dot(jnp.broadcast_to(ngr, (8, LANES)).astype(BF16),
                  jnp.where(li < lj, 1.0, 0.0).astype(BF16))[0:1]
    yield
    base = before * GROUP + rank
    base1 = jnp.where(oh1, base, 0.0)
    base2 = jnp.where(oh2, base, 0.0)
    pos1 = jnp.sum(base1, axis=-1, keepdims=True)
    pos2 = jnp.sum(base2, axis=-1, keepdims=True)

    def as_row(v):
        ones = jnp.ones((8, LANES), BF16)
        hi = jnp.floor(v * (1.0 / 256.0))
        return (_dot_nt(ones, hi.astype(BF16)) * 256.0
                + _dot_nt(ones, (v - hi * 256.0).astype(BF16)))[0:1]

    row1, row2 = as_row(base1), as_row(base2)
    yield
    r = lax.broadcasted_iota(jnp.int32, (SORT_CAP, s), 0).astype(F32)
    p1 = jnp.where(r == row1, 1.0, 0.0).astype(BF16)
    p2 = jnp.where(r == row2, 1.0, 0.0).astype(BF16)
    xs_ref[:, 0:D_MODEL] = _dot(p1 + p2, x_hi).astype(BF16)

    def gate_lanes(gv):
        hi = gv.astype(BF16).astype(F32)
        return jnp.where(lane == 0, hi, jnp.where(lane == 1, gv - hi, 0.0)).astype(BF16)

    xs_ref[:, D_MODEL:SORT_COLS] = (_dot(p1, gate_lanes(g1))
                                    + _dot(p2, gate_lanes(g2))).astype(BF16)
    pos_ref[...] = jnp.where(lane == 0, pos1, jnp.where(lane == 1, pos2, 0.0))
    ng_ref[...] = jnp.broadcast_to(ngr, (8, LANES)).astype(jnp.int32)


def _route_sort(h, g, wr):
    tokens = h.shape[0]
    nt = tokens // SORT_TOKENS
    tps = math.gcd(nt, SORT_TILES_PER_STEP)
    steps = nt // tps
    last = lambda i: jnp.minimum(i, steps - 1)
    return pl.pallas_call(
        _route_sort_kernel,
        out_shape=(jax.ShapeDtypeStruct(((nt + tps) * SORT_CAP, SORT_COLS), BF16),
                   jax.ShapeDtypeStruct((tokens, LANES), F32),
                   jax.ShapeDtypeStruct((nt, 8, LANES), jnp.int32)),
        grid=(steps + 1,),
        in_specs=[pl.BlockSpec((tps * SORT_TOKENS, D_MODEL), lambda i: (last(i), 0)),
                  pl.BlockSpec((1, D_MODEL), lambda i: (0, 0)),
                  pl.BlockSpec((D_MODEL, 2 * LANES), lambda i: (0, 0))],
        out_specs=(pl.BlockSpec((tps * SORT_CAP, SORT_COLS), lambda i: (i, 0)),
                   pl.BlockSpec((tps * SORT_TOKENS, LANES), lambda i: (last(i), 0)),
                   pl.BlockSpec((tps, 8, LANES), lambda i: (last(i), 0, 0))),
        compiler_params=_params("arbitrary"),
        name="route_sort",
    )(h, g, wr)


def _group_tables(ng, n_row_tiles):
    nt, ne = ng.shape
    gpt = GROUPS_PER_TILE
    n_e = jnp.sum(ng, axis=0)
    rt_e = (n_e + gpt - 1) // gpt
    rt_end = jnp.cumsum(rt_e)
    n_rt = rt_end[-1]
    r = jnp.arange(n_row_tiles, dtype=jnp.int32)
    valid_r = r < n_rt
    e_r = jnp.minimum(jnp.sum((r[:, None] >= rt_end[None, :]).astype(jnp.int32), axis=1), ne - 1)
    e_last = jnp.sum(jnp.where(r == n_rt - 1, e_r, 0))
    e_r = jnp.where(valid_r, e_r, e_last)
    sel = (e_r[:, None] == jnp.arange(ne, dtype=jnp.int32)[None, :]).astype(jnp.int32)
    pick = lambda v: jnp.sum(sel * v[None, :], axis=1)
    k0 = (r - pick(rt_end - rt_e)) * gpt
    gidx = k0[:, None] + jnp.arange(gpt, dtype=jnp.int32)[None, :]
    valid = valid_r[:, None] & (gidx < pick(n_e)[:, None])
    incl = jnp.cumsum(ng, axis=0)
    incl_r = jnp.sum(sel[:, None, :] * incl[None, :, :], axis=2)
    ng_r = jnp.sum(sel[:, None, :] * ng[None, :, :], axis=2)
    lo = jnp.cumsum(ng, axis=1) - ng
    lo_r = jnp.sum(sel[:, None, :] * lo[None, :, :], axis=2)
    ti = jnp.sum((incl_r[:, None, :] <= gidx[:, :, None]).astype(jnp.int32), axis=2)
    ti = jnp.minimum(ti, nt - 1)
    tsel = (ti[:, :, None] == jnp.arange(nt, dtype=jnp.int32)[None, None, :]).astype(jnp.int32)
    tpick = lambda v: jnp.sum(tsel * v[:, None, :], axis=2)
    within = gidx - tpick(incl_r - ng_r)
    rows = ti * SORT_CAP + GROUP * (tpick(lo_r) + within)
    spare = nt * SORT_CAP
    sink = (spare + SPARE_SINK + (r % 2)[:, None] * MOE_ROWS
            + GROUP * jnp.arange(gpt, dtype=jnp.int32)[None, :])
    src = jnp.where(valid, rows, spare + SPARE_ZERO).reshape(-1)
    dst = jnp.where(valid, rows, sink).reshape(-1)
    return e_r, jnp.sum(valid.astype(jnp.int32), axis=1), src, dst


def _group_copies(rows_ref, tile, hbm, buf, slot, sem, to_hbm, start):
    for k in range(GROUPS_PER_TILE):
        off = pl.multiple_of(rows_ref[tile * GROUPS_PER_TILE + k], GROUP)
        buf_rows = buf.at[slot, pl.ds(k * GROUP, GROUP), :]
        hbm_rows = hbm.at[pl.ds(off, GROUP), pl.ds(0, buf_rows.shape[-1])]
        cp = (pltpu.make_async_copy(buf_rows, hbm_rows, sem.at[slot]) if to_hbm
              else pltpu.make_async_copy(hbm_rows, buf_rows, sem.at[slot]))
        if start:
            cp.start()
        else:
            cp.wait()


def _moe_group_kernel(te_ref, tv_ref, src_ref, dst_ref, xs_hbm, w1_ref, w3_ref, w2_ref, ys_hbm,
                      xbuf, ybuf, acc_sc, xsem, ysem):
    r = pl.program_id(0)
    j = pl.program_id(1)
    nr = pl.num_programs(0)
    nj = pl.num_programs(1)
    slot = lax.rem(r, 2)
    valid = lambda t: tv_ref[jnp.clip(t, 0, nr - 1)] > 0
    gather = functools.partial(_group_copies, src_ref, hbm=xs_hbm, buf=xbuf, sem=xsem,
                               to_hbm=False)
    scatter = functools.partial(_group_copies, dst_ref, hbm=ys_hbm, buf=ybuf, sem=ysem,
                                to_hbm=True)

    @pl.when(j == 0)
    def _():
        @pl.when((r == 0) & valid(r))
        def _():
            gather(tile=r, slot=slot, start=True)

        @pl.when(valid(r))
        def _():
            gather(tile=r, slot=slot, start=False)

        @pl.when((r + 1 < nr) & valid(r + 1))
        def _():
            gather(tile=r + 1, slot=1 - slot, start=True)

        @pl.when((r >= 2) & valid(r - 2))
        def _():
            scatter(tile=r - 2, slot=slot, start=False)

    def compute(rows):
        x = xbuf[slot, rows, 0:D_MODEL]
        a = _dot(x, w1_ref[0])
        b = _dot(x, w3_ref[0])
        hid = (a * jax.nn.sigmoid(a) * b).astype(BF16)
        acc = jnp.where(j == 0, 0.0, acc_sc[rows, :]) + _dot(hid, w2_ref[0])
        acc_sc[rows, :] = acc
        gl = xbuf[slot, rows, D_MODEL:SORT_COLS].astype(F32)
        ybuf[slot, rows, :] = (acc * (gl[:, 0:1] + gl[:, 1:2])).astype(BF16)

    half = MOE_ROWS // 2
    full = tv_ref[r] > GROUPS_PER_TILE // 2

    @pl.when(full)
    def _():
        compute(slice(0, MOE_ROWS))

    @pl.when(valid(r) & jnp.logical_not(full))
    def _():
        compute(slice(0, half))

        @pl.when(j == nj - 1)
        def _():
            ybuf[slot, half:MOE_ROWS, :] = jnp.zeros((half, D_MODEL), BF16)

    @pl.when(valid(r) & (j == nj - 1))
    def _():
        scatter(tile=r, slot=slot, start=True)

    @pl.when((r == nr - 1) & (j == nj - 1))
    def _():
        @pl.when((r >= 1) & valid(r - 1))
        def _():
            scatter(tile=r - 1, slot=1 - slot, start=False)

        @pl.when(valid(r))
        def _():
            scatter(tile=r, slot=slot, start=False)


def _moe_group(xs, tile_expert, tile_valid, src, dst, w1, w3, w2, n_row_tiles, tf):
    ne, _, dff = w1.shape
    nj = dff // tf
    assert nj >= 2

    def w_in(r, j, te, tv, s, d):
        return (te[r], 0, jnp.where(tv[r] > 0, j, nj - 1))

    def w_out(r, j, te, tv, s, d):
        return (te[r], jnp.where(tv[r] > 0, j, nj - 1), 0)

    grid_spec = pltpu.PrefetchScalarGridSpec(
        num_scalar_prefetch=4,
        grid=(n_row_tiles, nj),
        in_specs=[pl.BlockSpec(memory_space=pl.ANY),
                  pl.BlockSpec((1, D_MODEL, tf), w_in),
                  pl.BlockSpec((1, D_MODEL, tf), w_in),
                  pl.BlockSpec((1, tf, D_MODEL), w_out)],
        out_specs=pl.BlockSpec(memory_space=pl.ANY),
        scratch_shapes=[pltpu.VMEM((2, MOE_ROWS, SORT_COLS), BF16),
                        pltpu.VMEM((2, MOE_ROWS, D_MODEL), BF16),
                        pltpu.VMEM((MOE_ROWS, D_MODEL), F32),
                        pltpu.SemaphoreType.DMA((2,)),
                        pltpu.SemaphoreType.DMA((2,))],
    )
    return pl.pallas_call(
        _moe_group_kernel,
        out_shape=jax.ShapeDtypeStruct(xs.shape, xs.dtype),
        grid_spec=grid_spec,
        input_output_aliases={4: 0},
        compiler_params=_params("arbitrary", "arbitrary"),
        name="moe_group",
    )(tile_expert, tile_valid, src, dst, xs, w1, w3, w2)


def _combine_kernel(h_ref, pos_ref, ys_ref, out_ref):
    pos = pos_ref[...]
    pt = _one_hot_rows(pos[:, 0:1], pos[:, 1:2], SORT_CAP)
    out_ref[...] = h_ref[...] + _dot(pt, ys_ref[...])


def _combine(h, pos, ys):
    tokens = h.shape[0]
    return pl.pallas_call(
        _combine_kernel,
        out_shape=jax.ShapeDtypeStruct((tokens, D_MODEL), F32),
        grid=(tokens // SORT_TOKENS,),
        in_specs=[pl.BlockSpec((SORT_TOKENS, D_MODEL), lambda i: (i, 0)),
                  pl.BlockSpec((SORT_TOKENS, LANES), lambda i: (i, 0)),
                  pl.BlockSpec((SORT_CAP, D_MODEL), lambda i: (i, 0))],
        out_specs=pl.BlockSpec((SORT_TOKENS, D_MODEL), lambda i: (i, 0)),
        compiler_params=_params("parallel"),
        name="combine",
    )(h, pos, ys)


def _row_tile(rows, want):
    tm = want
    while rows % tm:
        tm //= 2
    return tm


def _rope_tables(lp):
    inv_freq = 1.0 / (ROPE_THETA ** (jnp.arange(0, SWA_HEAD_DIM, 2, dtype=F32) / SWA_HEAD_DIM))
    row = jnp.arange(lp)
    seq = lp - BLOCK
    pos = jnp.where(row < seq, row + N_META, row - seq - FRONT).astype(F32)
    ang = pos[:, None] * inv_freq[None, :]
    cos, sin = jnp.cos(ang), jnp.sin(ang)
    zero = jnp.zeros_like(sin)
    rep = LANES // SWA_HEAD_DIM
    cos_t = jnp.tile(jnp.concatenate([cos, cos], axis=-1), (1, rep))
    sin_a = jnp.tile(jnp.concatenate([zero, sin], axis=-1), (1, rep))
    sin_b = jnp.tile(jnp.concatenate([-sin, zero], axis=-1), (1, rep))
    return cos_t, sin_a, sin_b


def kernel(x, meta, a_norm, a_w_in, a_w_gate2, a_b_gate, a_q_norm, a_k_norm, a_sinks, a_o_norm,
           a_w_out, f_norm, f_w1, f_w3, f_w2, c_norm, c_w_pw1, c_w_dw, c_b_dw, c_ln_g, c_ln_b,
           c_w_pw2, m_norm, m_w_router, m_w1, m_w3, m_w2):
    batch, seq, _ = x.shape
    assert seq % BLOCK == 0
    lp = BLOCK + seq

    head =jnp.concatenate([jnp.zeros((FRONT, D_MODEL), x.dtype), meta.astype(x.dtype)],
                           axis=0)[None]
    tm = _row_tile(seq, 512)

    w_in = a_w_in[0]
    gq, gk, gv, gr, glr, sq, sk, sv = jnp.split(
        w_in, [256, 512, 1024, 1536, 1552, 2064, 2192], axis=-1)
    w_in_r = jnp.concatenate(
        [gq, gk, gv, gr, sq, sk, sv, glr,
         jnp.zeros((D_MODEL, Z_COLS - Z_LR - GLA_LOWRANK), w_in.dtype)], axis=-1).astype(BF16)
    lane = jnp.arange(LANES)
    bd = (lane[:, None] // SWA_HEAD_DIM == lane[None, :] // SWA_HEAD_DIM).astype(BF16)
    rope = _rope_tables(lp) + (jnp.tile(a_q_norm[0], SWA_Q_HEADS)[None],
                               jnp.tile(a_k_norm[0], SWA_KV_HEADS)[None], bd)
    z = _in_proj(x, a_norm[0][None], w_in_r, rope, batch, lp, _row_tile(seq, 1024), 0)
    z = _in_proj(head, a_norm[0][None], w_in_r, rope, batch, lp, BLOCK, seq, z)

    wg = jnp.zeros((LANES, GLA_QK), F32).at[:GLA_LOWRANK].set(a_w_gate2[0]).astype(BF16)
    bb = math.gcd(batch, MIX_BATCH)
    o_a = _gla(z, wg, a_b_gate[0][None], a_o_norm[0][None], bb)
    o_b, d_w1, d_w3, d_w2 = _swa(z, a_sinks[0], (f_w1[0], f_w3[0], f_w2[0]), bb)

    w_out = a_w_out[0].astype(BF16)
    ffn = (w_out, f_norm[0][None], d_w1, d_w3, d_w2)
    h = _mix_out_ffn(x, o_a, o_b, *ffn, tm, FF_TILE, 0)
    h = _mix_out_ffn(head, o_a, o_b, *ffn, BLOCK, FF_TILE, seq, h)

    si = jnp.arange((SUBLANES - 1) * CONV_SPAN)
    shift = (jnp.arange(CONV_SPAN)[None, :]
             == (si % CONV_SPAN + si // CONV_SPAN + 1)[:, None]).astype(BF16)
    ne, _, dff = m_w1[0].shape
    experts = (m_w1[0].reshape(ne * D_MODEL, dff), m_w3[0].reshape(ne * D_MODEL, dff),
               m_w2[0].reshape(ne * dff, D_MODEL))
    h, e_w1, e_w3, e_w2 = _conv_module(
        h, c_norm[0][None], c_w_pw1[0].astype(BF16), shift,
        jnp.broadcast_to(c_w_dw[0][:, None, :], (CONV_WIDTH, SUBLANES, D_MODEL)),
        c_b_dw[0][None], c_ln_g[0][None], c_ln_b[0][None], c_w_pw2[0].astype(BF16), experts)
    h = h.reshape(batch * seq, D_MODEL)

    wr = jnp.zeros((D_MODEL, LANES), F32).at[:, :N_EXPERTS].set(m_w_router[0])
    wr_hi = wr.astype(BF16)
    wr = jnp.concatenate([wr_hi, (wr - wr_hi.astype(F32)).astype(BF16)], axis=1)
    tokens = batch * seq
    assert tokens % SORT_TOKENS == 0
    nt = tokens // SORT_TOKENS
    xs, pos, ng = _route_sort(h, m_norm[0][None], wr)
    ng = ng[:, 0, :N_EXPERTS]
    n_row_tiles = (nt * (SORT_CAP // GROUP)) // GROUPS_PER_TILE + N_EXPERTS
    tile_expert, tile_valid, src, dst = _group_tables(ng, n_row_tiles)
    ys = _moe_group(xs, tile_expert, tile_valid, src, dst, e_w1.reshape(ne, D_MODEL, dff),
                    e_w3.reshape(ne, D_MODEL, dff), e_w2.reshape(ne, dff, D_MODEL), n_row_tiles,
                    FF_TILE)
    out = _combine(h, pos, ys)
    return out.reshape(batch, seq, D_MODEL)
```

```python
import functools
import math

import jax
import jax.numpy as jnp
from jax import lax
from jax.experimental import pallas as pl
from jax.experimental.pallas import tpu as pltpu

F32 = jnp.float32
BF16 = jnp.bfloat16

D_MODEL = 1024
N_META = 16
EPS = 1e-6
ROPE_THETA = 10000.0
NEG_INF = -1e30
LOG2E = math.log2(math.e)
GLA_HEADS = 4
GLA_DK = 64
GLA_DV = 128
GLA_LOWRANK = 16
GLA_TAU = 16.0
GLA_CHUNK = 64
SWA_Q_HEADS = 8
SWA_KV_HEADS = 2
SWA_HEAD_DIM = 64
SWA_WINDOW = 128
CONV_WIDTH = 31
D_FF = 3584
N_EXPERTS = 8

GLA_QK = GLA_HEADS * GLA_DK
GLA_V = GLA_HEADS * GLA_DV
SWA_Q = SWA_Q_HEADS * SWA_HEAD_DIM
SWA_KV = SWA_KV_HEADS * SWA_HEAD_DIM
SWA_GROUP = SWA_Q_HEADS // SWA_KV_HEADS

LANES = 128
FF_TILE = 1792
BLOCK = 128
FRONT = BLOCK - N_META
VMEM_LIMIT = 56 * 1024 * 1024

Z_Q, Z_K, Z_V, Z_R = 0, 256, 512, 1024
Z_SQ, Z_SK, Z_SV, Z_LR = 1536, 2048, 2176, 2304
Z_COLS = 2432


def _rms(x, g):
    return x * lax.rsqrt(jnp.mean(x * x, axis=-1, keepdims=True) + EPS) * g


def _dot(a, b):
    return jnp.dot(a, b, preferred_element_type=F32)


def _dot_nt(a, b):
    return lax.dot_general(a, b, (((1,), (1,)), ((), ())), preferred_element_type=F32)


def _dot_tn(a, b):
    return lax.dot_general(a, b, (((0,), (0,)), ((), ())), preferred_element_type=F32)


def _stored_block(n, lp):
    return jnp.where(n == 0, lp // BLOCK - 1, n - 1)


def _params(*sem):
    return pltpu.CompilerParams(dimension_semantics=sem, vmem_limit_bytes=VMEM_LIMIT)


IN_PROJ_ROWS = 256


def _in_proj_kernel(h_ref, g_ref, w_ref, cos_ref, sa_ref, sb_ref, qn_ref, kn_ref, bd_ref,
                    *rest):
    z_ref = rest[-1]
    tile = lambda t: jnp.concatenate([t] * (SWA_Q // LANES), axis=-1)
    tm = h_ref.shape[0]
    chunks = [slice(r, min(r + IN_PROJ_ROWS, tm)) for r in range(0, tm, IN_PROJ_ROWS)]
    project = lambda rs: _dot(_rms(h_ref[rs, :], g_ref[...]).astype(BF16), w_ref[...])
    zs = {0: project(chunks[0])}
    for i, rs in enumerate(chunks):
        if i + 1 < len(chunks):
            zs[i + 1] = project(chunks[i + 1])
        z = zs.pop(i)
        cos, sa, sb = cos_ref[rs, :], sa_ref[rs, :], sb_ref[rs, :]
        q = _norm_rope(z[None, :, Z_SQ:Z_SQ + SWA_Q], qn_ref[...], tile(cos), tile(sa),
                       tile(sb), bd_ref[...])[0] * (SWA_HEAD_DIM ** -0.5 * LOG2E)
        k = _norm_rope(z[None, :, Z_SK:Z_SK + SWA_KV], kn_ref[...], cos, sa, sb, bd_ref[...])[0]
        z = jnp.concatenate([z[:, :Z_SQ], q, k, z[:, Z_SV:]], axis=-1).astype(z_ref.dtype)
        if z_ref.ndim == 2:
            z_ref[rs, :] = z
        else:
            z_ref[:, rs, :] = jnp.broadcast_to(z[None], (z_ref.shape[0],) + z.shape)


def _in_proj(h, g, w, rope, batch, lp, tm, first_row, z=None):
    hb, rows, _ = h.shape
    n = w.shape[1]
    shared = hb == 1 and batch > 1
    const = lambda a: pl.BlockSpec(a.shape, lambda b, j: (0,) * a.ndim)
    table = pl.BlockSpec((tm, LANES), lambda b, j: (first_row // tm + j, 0))
    in_specs = [pl.BlockSpec((None, tm, D_MODEL), lambda b, j: (b, j, 0)),
                pl.BlockSpec((1, D_MODEL), lambda b, j: (0, 0)),
                pl.BlockSpec((D_MODEL, n), lambda b, j: (0, 0)),
                table, table, table, const(rope[3]), const(rope[4]), const(rope[5])]
    args = (h, g, w) + tuple(rope)
    if z is not None:
        in_specs.append(pl.BlockSpec(memory_space=pl.ANY))
        args += (z,)
    return pl.pallas_call(
        _in_proj_kernel,
        out_shape=jax.ShapeDtypeStruct((batch, lp, n), BF16),
        grid=(hb, rows // tm),
        in_specs=in_specs,
        out_specs=pl.BlockSpec((batch if shared else None, tm, n),
                               lambda b, j: (b, first_row // tm + j, 0)),
        input_output_aliases={} if z is None else {len(args) - 1: 0},
        compiler_params=_params("parallel", "parallel"),
        name="in_proj",
    )(*args)


MIX_BATCH = 8


def _gla_kernel(q_ref, k_ref, v_ref, r_ref, lr_ref, wg_ref, bg_ref, on_ref, o_ref, st_ref):
    n = pl.program_id(1)

    @pl.when(n == 0)
    def _():
        st_ref[...] = jnp.zeros_like(st_ref)

    c = GLA_CHUNK
    items = [(bi, ch) for bi in range(q_ref.shape[0]) for ch in range(BLOCK // c)]
    ri = lax.broadcasted_iota(jnp.int32, (BLOCK, BLOCK), 0)
    ci = lax.broadcasted_iota(jnp.int32, (BLOCK, BLOCK), 1)
    tri = jnp.where((ri >= ci) & (ri // c == ci // c), 1.0, 0.0).astype(BF16)
    causal = (lax.broadcasted_iota(jnp.int32, (c, c), 0)
              >= lax.broadcasted_iota(jnp.int32, (c, c), 1))
    row = lax.broadcasted_iota(jnp.int32, (BLOCK, GLA_QK), 0) + n * BLOCK
    on = on_ref[...]

    b_all = []
    for bi in range(q_ref.shape[0]):
        gp = _dot(lr_ref[bi], wg_ref[...]) + bg_ref[...]
        g = (jnp.minimum(gp, 0.0) - jnp.log(1.0 + jnp.exp(-jnp.abs(gp)))) * (1.0 / GLA_TAU)
        g = jnp.where(row >= FRONT, g, 0.0)
        g_hi = g.astype(BF16)
        g_lo = (g - g_hi.astype(F32)).astype(BF16)
        b_all.append(_dot(tri, g_hi) + _dot(tri, g_lo))

    q_t, k_t, k_d, dec = {}, {}, {}, {}
    for bi, ch in items:
        rs = slice(ch * c, (ch + 1) * c)
        b = b_all[bi][rs]
        b_last = b_all[bi][(ch + 1) * c - 1:(ch + 1) * c, :]
        qf = q_ref[bi, rs, :].astype(F32)
        kf = k_ref[bi, rs, :].astype(F32)
        q_t[bi, ch] = (qf * jnp.exp(b) * (GLA_DK ** -0.5)).astype(BF16)
        k_t[bi, ch] = (kf * jnp.exp(-b)).astype(BF16)
        k_d[bi, ch] = (kf * jnp.exp(b_last - b)).astype(BF16)
        dec[bi, ch] = jnp.exp(b_last)

    att, kv = {}, {}
    for bi, ch in items:
        rs = slice(ch * c, (ch + 1) * c)
        for hd in range(GLA_HEADS):
            ks = slice(hd * GLA_DK, (hd + 1) * GLA_DK)
            vh = v_ref[bi, rs, hd * GLA_DV:(hd + 1) * GLA_DV]
            att[bi, ch, hd] = jnp.where(
                causal, _dot_nt(q_t[bi, ch][:, ks], k_t[bi, ch][:, ks]), 0.0).astype(BF16)
            kv[bi, ch, hd] = _dot_tn(vh, k_d[bi, ch][:, ks])

    for bi in range(q_ref.shape[0]):
        for hd in range(GLA_HEADS):
            ks = slice(hd * GLA_DK, (hd + 1) * GLA_DK)
            vs = slice(hd * GLA_DV, (hd + 1) * GLA_DV)
            st = st_ref[bi, hd]
            for ch in range(BLOCK // c):
                rs = slice(ch * c, (ch + 1) * c)
                o = (_dot(att[bi, ch, hd], v_ref[bi, rs, vs])
                     + _dot_nt(q_t[bi, ch][:, ks], st.astype(BF16)))
                st = st * dec[bi, ch][:, ks] + kv[bi, ch, hd]
                o = _rms(o, on)
                r = r_ref[bi, rs, vs].astype(F32)
                o_ref[bi, rs, vs] = (o * (r * jax.nn.sigmoid(r))).astype(o_ref.dtype)
            st_ref[bi, hd] = st


def _gla(z, wg, bg, on, bb):
    batch, lp, _ = z.shape

    def col(width, start):
        return pl.BlockSpec((bb, BLOCK, width), lambda b, n: (b, _stored_block(n, lp), start // width))

    const = lambda shape: pl.BlockSpec(shape, lambda b, n: (0,) * len(shape))
    return pl.pallas_call(
        _gla_kernel,
        out_shape=jax.ShapeDtypeStruct((batch, lp, GLA_V), BF16),
        grid=(batch // bb, lp // BLOCK),
        in_specs=[col(GLA_QK, Z_Q), col(GLA_QK, Z_K), col(GLA_V, Z_V), col(GLA_V, Z_R),
                  col(LANES, Z_LR), const((LANES, GLA_QK)), const((1, GLA_QK)),
                  const((1, GLA_DV))],
        out_specs=pl.BlockSpec((bb, BLOCK, GLA_V), lambda b, n: (b, _stored_block(n, lp), 0)),
        scratch_shapes=[pltpu.VMEM((bb, GLA_HEADS, GLA_DV, GLA_DK), F32)],
        compiler_params=_params("parallel", "arbitrary"),
        name="gla",
    )(z, z, z, z, z, wg, bg, on)


def _group_mean_sq(x, bd):
    sq = x * x
    hi = sq.astype(BF16)
    lo = (sq - hi.astype(F32)).astype(BF16)
    return _dot(hi, bd) + _dot(lo, bd)


def _norm_rope(x, gain, cos, sin_a, sin_b, bd):
    nbat, rows, width = x.shape
    x2 = x.reshape(nbat * rows, width)
    parts = []
    for s in range(width // LANES):
        xs = x2[:, s * LANES:(s + 1) * LANES]
        ms = _group_mean_sq(xs, bd) * (1.0 / SWA_HEAD_DIM)
        parts.append(xs * lax.rsqrt(ms + EPS))
    xn = (parts[0] if len(parts) == 1 else jnp.concatenate(parts, axis=-1)) * gain
    half = SWA_HEAD_DIM // 2
    back = lambda t: t.reshape(nbat, rows, width)
    return (back(xn) * cos + back(pltpu.roll(xn, half, 1)) * sin_a
            + back(pltpu.roll(xn, width - half, 1)) * sin_b)


def _swa_kernel(sinks_ref, q_ref, kc_ref, kp_ref, vc_ref, vp_ref, f1_ref, f3_ref, f2_ref,
                o_ref, f1b_ref, f3b_ref, f2b_ref, i1_sc, i3_sc, i2_sc, o1_sc, o3_sc, o2_sc,
                in_sem, out_sem, *, cast_steps):
    n = pl.program_id(1)
    nbat = q_ref.shape[0]

    cast_step = pl.program_id(0) * pl.num_programs(1) + n

    @pl.when(cast_step < cast_steps)
    def _():
        _cast_stream(cast_step, cast_steps, (f1_ref, f3_ref, f2_ref),
                     (f1b_ref, f3b_ref, f2b_ref), (i1_sc, i3_sc, i2_sc), (o1_sc, o3_sc, o2_sc),
                     in_sem, out_sem)

    q = q_ref[...]
    k_c = kc_ref[...]
    k_p = kp_ref[...]

    gq = SWA_GROUP * BLOCK
    r = lax.broadcasted_iota(jnp.int32, (gq, 2 * BLOCK), 0) & (BLOCK - 1)
    cidx = lax.broadcasted_iota(jnp.int32, (gq, 2 * BLOCK), 1)
    allowed = (cidx > r) & (cidx <= r + SWA_WINDOW) & ((n - 1) * BLOCK + cidx >= FRONT)
    rgroup = lax.broadcasted_iota(jnp.int32, (gq, 1), 0) // BLOCK
    items = [(bi, hk) for bi in range(nbat) for hk in range(SWA_KV_HEADS)]

    s, sink = {}, {}
    for hk in range(SWA_KV_HEADS):
        sk = jnp.zeros((gq, 1), F32)
        for g in range(SWA_GROUP):
            sk = jnp.where(rgroup == g, sinks_ref[hk * SWA_GROUP + g] * LOG2E, sk)
        sink[hk] = sk
    for bi, hk in items:
        ks = slice(hk * SWA_HEAD_DIM, (hk + 1) * SWA_HEAD_DIM)
        qg = jnp.concatenate(
            [q[bi][:, (hk * SWA_GROUP + g) * SWA_HEAD_DIM:(hk * SWA_GROUP + g + 1) * SWA_HEAD_DIM]
             for g in range(SWA_GROUP)], axis=0)
        k = jnp.concatenate([k_p[bi][:, ks], k_c[bi][:, ks]], axis=0)
        s[bi, hk] = jnp.where(allowed, _dot_nt(qg, k), NEG_INF)

    p, denom = {}, {}
    for bi, hk in items:
        m = jnp.maximum(jnp.max(s[bi, hk], axis=-1, keepdims=True), sink[hk])
        e = jnp.exp2(s[bi, hk] - m)
        denom[bi, hk] = jnp.sum(e, axis=-1, keepdims=True) + jnp.exp2(sink[hk] - m)
        p[bi, hk] = e.astype(BF16)

    for bi, hk in items:
        ks = slice(hk * SWA_HEAD_DIM, (hk + 1) * SWA_HEAD_DIM)
        v = jnp.concatenate([vp_ref[bi, :, ks], vc_ref[bi, :, ks]], axis=0)
        o = _dot(p[bi, hk], v) / denom[bi, hk]
        for g in range(SWA_GROUP):
            hq = hk * SWA_GROUP + g
            o_ref[bi, :, hq * SWA_HEAD_DIM:(hq + 1) * SWA_HEAD_DIM] = (
                o[g * BLOCK:(g + 1) * BLOCK].astype(o_ref.dtype))


def _swa(z, sinks, ffn, bb):
    batch, lp, _ = z.shape
    steps = (batch // bb) * (lp // BLOCK)
    cast_steps = 1 << (steps.bit_length() - 1)
    assert all(w.shape[0] % (cast_steps * GROUP) == 0 for w in ffn)
    chunk = lambda w: (2, w.shape[0] // cast_steps, w.shape[1])
    any_spec = pl.BlockSpec(memory_space=pl.ANY)

    def cur(width, start):
        return pl.BlockSpec((bb, BLOCK, width),
                            lambda b, n, s: (b, _stored_block(n, lp), start // width))

    def prev(width, start):
        return pl.BlockSpec((bb, BLOCK, width),
                            lambda b, n, s: (b, _stored_block(jnp.maximum(n - 1, 0), lp),
                                             start // width))

    grid_spec = pltpu.PrefetchScalarGridSpec(
        num_scalar_prefetch=1,
        grid=(batch // bb, lp // BLOCK),
        in_specs=[cur(SWA_Q, Z_SQ), cur(SWA_KV, Z_SK), prev(SWA_KV, Z_SK),
                  cur(SWA_KV, Z_SV), prev(SWA_KV, Z_SV),
                  any_spec, any_spec, any_spec],
        out_specs=(pl.BlockSpec((bb, BLOCK, SWA_Q),
                                lambda b, n, s: (b, _stored_block(n, lp), 0)),
                   any_spec, any_spec, any_spec),
        scratch_shapes=[pltpu.VMEM(chunk(w), F32) for w in ffn]
        + [pltpu.VMEM(chunk(w), BF16) for w in ffn]
        + [pltpu.SemaphoreType.DMA((2,)), pltpu.SemaphoreType.DMA((2,))],
    )
    return pl.pallas_call(
        functools.partial(_swa_kernel, cast_steps=cast_steps),
        out_shape=(jax.ShapeDtypeStruct((batch, lp, SWA_Q), BF16),)
        + tuple(jax.ShapeDtypeStruct(w.shape, BF16) for w in ffn),
        grid_spec=grid_spec,
        compiler_params=_params("arbitrary", "arbitrary"),
        name="swa",
    )(sinks, z, z, z, z, z, *ffn)


def _ffn_kernel(h_ref, oa_ref, ob_ref, woa_ref, wob_ref, g_ref, w1_ref, w3_ref, w2_ref,
                *rest):
    out_ref, h1_sc, xn_sc, acc_sc = rest[-4:]
    j = pl.program_id(2)

    @pl.when(j == 0)
    def _():
        h1 = h_ref[...] + _dot(oa_ref[...], woa_ref[...]) + _dot(ob_ref[...], wob_ref[...])
        h1_sc[...] = h1
        xn_sc[...] = _rms(h1, g_ref[...]).astype(BF16)

    x = xn_sc[...]
    a = _dot(x, w1_ref[...])
    b = _dot(x, w3_ref[...])
    hid = (a * jax.nn.sigmoid(a) * b).astype(BF16)
    acc = jnp.where(j == 0, h1_sc[...], acc_sc[...]) + _dot(hid, w2_ref[...])
    acc_sc[...] = acc
    out_ref[...] = acc


def _mix_out_ffn(h, oa, ob, w_out, g, w1, w3, w2, tm, tf, first_row, out=None):
    hb, rows, _ = h.shape
    batch, lp, _ = oa.shape
    dff = w1.shape[1]
    row0 = first_row // tm
    const = lambda shape: pl.BlockSpec(shape, lambda b, i, j: (0, 0))
    in_specs = [pl.BlockSpec((None, tm, D_MODEL),
                             lambda b, i, j: (jnp.minimum(b, hb - 1), i, 0)),
                pl.BlockSpec((None, tm, GLA_V), lambda b, i, j: (b, row0 + i, 0)),
                pl.BlockSpec((None, tm, SWA_Q), lambda b, i, j: (b, row0 + i, 0)),
                pl.BlockSpec((GLA_V, D_MODEL), lambda b, i, j: (0, 0)),
                pl.BlockSpec((SWA_Q, D_MODEL), lambda b, i, j: (GLA_V // SWA_Q, 0)),
                const((1, D_MODEL)),
                pl.BlockSpec((D_MODEL, tf), lambda b, i, j: (0, j)),
                pl.BlockSpec((D_MODEL, tf), lambda b, i, j: (0, j)),
                pl.BlockSpec((tf, D_MODEL), lambda b, i, j: (j, 0))]
    assert GLA_V % SWA_Q == 0
    args = (h, oa, ob, w_out, w_out, g, w1, w3, w2)
    if out is not None:
        in_specs.append(pl.BlockSpec(memory_space=pl.ANY))
        args += (out,)
    return pl.pallas_call(
        _ffn_kernel,
        out_shape=jax.ShapeDtypeStruct((batch, lp, D_MODEL), F32),
        grid=(hb, rows // tm, dff // tf),
        in_specs=in_specs,
        out_specs=pl.BlockSpec((None, tm, D_MODEL), lambda b, i, j: (b, row0 + i, 0)),
        scratch_shapes=[pltpu.VMEM((tm, D_MODEL), F32), pltpu.VMEM((tm, D_MODEL), BF16),
                        pltpu.VMEM((tm, D_MODEL), F32)],
        input_output_aliases={} if out is None else {9: 0},
        compiler_params=_params("parallel", "parallel", "arbitrary"),
        name="mix_out_ffn",
    )(*args)


CONV_BLOCK = BLOCK
CONV_HALO = 32
CONV_SPAN = CONV_HALO + CONV_BLOCK
CONV_ROWS = 32
SUBLANES = 8


def _cast_stream(step, n_steps, srcs, dsts, in_bufs, out_bufs, in_sem, out_sem):
    slot = lax.rem(step, 2)

    def copies(s, sl, fetch):
        out = []
        for src, dst, ibuf, obuf in zip(srcs, dsts, in_bufs, out_bufs):
            rows = ibuf.shape[1]
            at = pl.ds(pl.multiple_of(s * rows, rows), rows)
            out.append(pltpu.make_async_copy(src.at[at], ibuf.at[sl], in_sem.at[sl]) if fetch
                       else pltpu.make_async_copy(obuf.at[sl], dst.at[at], out_sem.at[sl]))
        return out

    @pl.when(step == 0)
    def _():
        for cp in copies(step, slot, True):
            cp.start()

    for cp in copies(step, slot, True):
        cp.wait()

    @pl.when(step + 1 < n_steps)
    def _():
        for cp in copies(step + 1, 1 - slot, True):
            cp.start()

    @pl.when(step >= 2)
    def _():
        for cp in copies(step - 2, slot, False):
            cp.wait()

    for ibuf, obuf in zip(in_bufs, out_bufs):
        obuf[slot] = ibuf[slot].astype(BF16)
    for cp in copies(step, slot, False):
        cp.start()

    @pl.when(step == n_steps - 1)
    def _():
        @pl.when(step >= 1)
        def _():
            for cp in copies(step - 1, 1 - slot, False):
                cp.wait()

        for cp in copies(step, slot, False):
            cp.wait()


def _conv_kernel(h_ref, hm_ref, g_ref, wa_ref, wg_ref, shift_ref, wdw_ref, bdw_ref, lg_ref, lb_ref,
                 w2_ref, e1_ref, e3_ref, e2_ref, out_ref, e1b_ref, e3b_ref, e2b_ref,
                 u_sc, xs_sc, c_sc, y_sc, i1_sc, i3_sc, i2_sc, o1_sc, o3_sc, o2_sc, in_sem,
                 out_sem):
    n = pl.program_id(1)

    def glu(x):
        xn = _rms(x, g_ref[...]).astype(BF16)
        return (_dot(xn, wa_ref[...]) * jax.nn.sigmoid(_dot(xn, wg_ref[...]))).astype(BF16)

    @pl.when(n == 0)
    def _():
        u_sc[0:CONV_HALO, :] = glu(hm_ref[BLOCK - CONV_HALO:BLOCK, :])

    h = h_ref[...]
    u_sc[CONV_HALO:CONV_SPAN, :] = glu(h)
    span = u_sc[...]
    xs_sc[0] = span.astype(F32)
    shifted = _dot(shift_ref[...], span)
    for r in range(1, SUBLANES):
        xs_sc[r] = shifted[(r - 1) * CONV_SPAN:r * CONV_SPAN]
    first = CONV_HALO - (CONV_WIDTH - 1)
    for part in range(CONV_BLOCK // BLOCK):
        for cb in range(D_MODEL // LANES):
            cs = slice(cb * LANES, (cb + 1) * LANES)
            acc = jnp.zeros((BLOCK // SUBLANES, SUBLANES, LANES), F32)
            for j in range(CONV_WIDTH):
                r = (first + j) % SUBLANES
                base = first + j - r + part * BLOCK
                x = xs_sc[r, base:base + BLOCK, cs].reshape(BLOCK // SUBLANES, SUBLANES, LANES)
                acc = acc + x * wdw_ref[j, :, cs]
            c_sc[part * BLOCK:(part + 1) * BLOCK, cs] = acc.reshape(BLOCK, LANES)
    for rb in range(CONV_BLOCK // CONV_ROWS):
        acc = c_sc[rb * CONV_ROWS:(rb + 1) * CONV_ROWS, :] + bdw_ref[...]
        mu = jnp.mean(acc, axis=-1, keepdims=True)
        dev = acc - mu
        var = jnp.mean(dev * dev, axis=-1, keepdims=True)
        y = dev * lax.rsqrt(var + EPS) * lg_ref[...] + lb_ref[...]
        y_sc[rb * CONV_ROWS:(rb + 1) * CONV_ROWS, :] = (y * jax.nn.sigmoid(y)).astype(BF16)
    out_ref[...] = h + _dot(y_sc[...], w2_ref[...])
    u_sc[0:CONV_HALO, :] = u_sc[CONV_BLOCK:CONV_SPAN, :]
    _cast_stream(pl.program_id(0) * pl.num_programs(1) + n,
                 pl.num_programs(0) * pl.num_programs(1), (e1_ref, e3_ref, e2_ref),
                 (e1b_ref, e3b_ref, e2b_ref), (i1_sc, i3_sc, i2_sc), (o1_sc, o3_sc, o2_sc),
                 in_sem, out_sem)


def _conv_module(h, g, w_pw1, shift, wdw, bdw, lg, lb, w2, experts):
    batch, lp, _ = h.shape
    seq = lp - BLOCK
    const = lambda shape: pl.BlockSpec(shape, lambda b, n: (0,) * len(shape))
    any_spec = pl.BlockSpec(memory_space=pl.ANY)
    n_chunks = batch * (seq // CONV_BLOCK)
    chunk = lambda w: (2, w.shape[0] // n_chunks, w.shape[1])
    assert seq % CONV_BLOCK == 0
    assert all(w.shape[0] % (n_chunks * GROUP) == 0 for w in experts)
    return pl.pallas_call(
        _conv_kernel,
        out_shape=(jax.ShapeDtypeStruct((batch, seq, D_MODEL), F32),)
        + tuple(jax.ShapeDtypeStruct(w.shape, BF16) for w in experts),
        grid=(batch, seq // CONV_BLOCK),
        in_specs=[pl.BlockSpec((None, CONV_BLOCK, D_MODEL), lambda b, n: (b, n, 0)),
                  pl.BlockSpec((None, BLOCK, D_MODEL), lambda b, n: (0, seq // BLOCK, 0)),
                  const((1, D_MODEL)),
                  pl.BlockSpec((D_MODEL, D_MODEL), lambda b, n: (0, 0)),
                  pl.BlockSpec((D_MODEL, D_MODEL), lambda b, n: (0, 1)),
                  const(((SUBLANES - 1) * CONV_SPAN, CONV_SPAN)),
                  const((CONV_WIDTH, SUBLANES, D_MODEL)), const((1, D_MODEL)), const((1, D_MODEL)),
                  const((1, D_MODEL)), const((D_MODEL, D_MODEL)),
                  any_spec, any_spec, any_spec],
        out_specs=(pl.BlockSpec((None, CONV_BLOCK, D_MODEL), lambda b, n: (b, n, 0)),
                   any_spec, any_spec, any_spec),
        scratch_shapes=[pltpu.VMEM((CONV_SPAN, D_MODEL), BF16),
                        pltpu.VMEM((SUBLANES, CONV_SPAN, D_MODEL), F32),
                        pltpu.VMEM((CONV_BLOCK, D_MODEL), F32),
                        pltpu.VMEM((CONV_BLOCK, D_MODEL), BF16)]
        + [pltpu.VMEM(chunk(w), F32) for w in experts]
        + [pltpu.VMEM(chunk(w), BF16) for w in experts]
        + [pltpu.SemaphoreType.DMA((2,)), pltpu.SemaphoreType.DMA((2,))],
        compiler_params=_params("arbitrary", "arbitrary"),
        name="conv_module",
    )(h, h, g, w_pw1, w_pw1, shift, wdw, bdw, lg, lb, w2, *experts)


SORT_TOKENS = 512
GROUP = 16
SORT_CAP = 2 * SORT_TOKENS + N_EXPERTS * GROUP
SORT_COLS = D_MODEL + LANES
MOE_ROWS = 512
GROUPS_PER_TILE = MOE_ROWS // GROUP


def _one_hot_rows(pos1, pos2, width):
    r = lax.broadcasted_iota(jnp.int32, (pos1.shape[0], width), 1).astype(F32)
    return jnp.where(r == pos1, 1.0, jnp.where(r == pos2, 1.0, 0.0)).astype(BF16)


SORT_TILES_PER_STEP = 4
SPARE_ZERO = 0
SPARE_SINK = 64
assert SPARE_SINK + 2 * MOE_ROWS <= SORT_CAP


def _route_sort_kernel(h_ref, g_ref, wr_ref, xs_ref, pos_ref, ng_ref):
    is_spare = pl.program_id(0) == pl.num_programs(0) - 1

    @pl.when(is_spare)
    def _():
        xs_ref[...] = jnp.zeros_like(xs_ref)

    @pl.when(jnp.logical_not(is_spare))
    def _():
        tiles = [_route_sort_tile(h_ref.at[pl.ds(t * SORT_TOKENS, SORT_TOKENS)], g_ref, wr_ref,
                                  xs_ref.at[pl.ds(t * SORT_CAP, SORT_CAP)],
                                  pos_ref.at[pl.ds(t * SORT_TOKENS, SORT_TOKENS)], ng_ref.at[t])
                 for t in range(h_ref.shape[0] // SORT_TOKENS)]
        while tiles:
            tiles = [t for t in tiles if next(t, "done") != "done"]


def _route_sort_tile(h_ref, g_ref, wr_ref, xs_ref, pos_ref, ng_ref):
    s = SORT_TOKENS
    xn = _rms(h_ref[...], g_ref[...])
    x_hi = xn.astype(BF16)
    x_lo = (xn - x_hi.astype(F32)).astype(BF16)
    both = _dot(x_hi, wr_ref[...])
    logits = (both[:, 0:LANES] + both[:, LANES:2 * LANES]
              + _dot(x_lo, wr_ref[:, 0:LANES]))
    yield
    lane = lax.broadcasted_iota(jnp.int32, logits.shape, 1)
    logits = jnp.where(lane < N_EXPERTS, logits, -jnp.inf)
    m1 = jnp.max(logits, axis=-1, keepdims=True)
    i1 = jnp.min(jnp.where(logits == m1, lane, LANES), axis=-1, keepdims=True)
    rest = jnp.where(lane == i1, -jnp.inf, logits)
    m2 = jnp.max(rest, axis=-1, keepdims=True)
    i2 = jnp.min(jnp.where(rest == m2, lane, LANES), axis=-1, keepdims=True)
    e2 = jnp.exp(m2 - m1)
    g1 = 1.0 / (1.0 + e2)
    g2 = e2 / (1.0 + e2)
    yield

    oh1 = lane == i1
    oh2 = lane == i2
    oh = jnp.where(oh1, 1.0, jnp.where(oh2, 1.0, 0.0))
    cnt = jnp.sum(oh, axis=0, keepdims=True)
    ngr = jnp.floor((cnt + (GROUP - 1)) * (1.0 / GROUP))
    ri = lax.broadcasted_iota(jnp.int32, (s, s), 0)
    ci = lax.broadcasted_iota(jnp.int32, (s, s), 1)
    rank = _dot(jnp.where(ri > ci, 1.0, 0.0).astype(BF16), oh.astype(BF16))
    li = lax.broadcasted_iota(jnp.int32, (LANES, LANES), 0)
    lj = lax.broadcasted_iota(jnp.int32, (LANES, LANES), 1)
    before = _dot(jnp.broadcast_to(ngr, (8, LANES)).astype(BF16),
                  jnp.where(li < lj, 1.0, 0.0).astype(BF16))[0:1]
    yield
    base = before * GROUP + rank
    base1 = jnp.where(oh1, base, 0.0)
    base2 = jnp.where(oh2, base, 0.0)
    pos1 = jnp.sum(base1, axis=-1, keepdims=True)
    pos2 = jnp.sum(base2, axis=-1, keepdims=True)

    def as_row(v):
        ones = jnp.ones((8, LANES), BF16)
        hi = jnp.floor(v * (1.0 / 256.0))
        return (_dot_nt(ones, hi.astype(BF16)) * 256.0
                + _dot_nt(ones, (v - hi * 256.0).astype(BF16)))[0:1]

    row1, row2 = as_row(base1), as_row(base2)
    yield
    r = lax.broadcasted_iota(jnp.int32, (SORT_CAP, s), 0).astype(F32)
    p1 = jnp.where(r == row1, 1.0, 0.0).astype(BF16)
    p2 = jnp.where(r == row2, 1.0, 0.0).astype(BF16)
    xs_ref[:, 0:D_MODEL] = _dot(p1 + p2, x_hi).astype(BF16)

    def gate_lanes(gv):
        hi = gv.astype(BF16).astype(F32)
        return jnp.where(lane == 0, hi, jnp.where(lane == 1, gv - hi, 0.0)).astype(BF16)

    xs_ref[:, D_MODEL:SORT_COLS] = (_dot(p1, gate_lanes(g1))
                                    + _dot(p2, gate_lanes(g2))).astype(BF16)
    pos_ref[...] = jnp.where(lane == 0, pos1, jnp.where(lane == 1, pos2, 0.0))
    ng_ref[...] = jnp.broadcast_to(ngr, (8, LANES)).astype(jnp.int32)


def _route_sort(h, g, wr):
    tokens = h.shape[0]
    nt = tokens // SORT_TOKENS
    tps = math.gcd(nt, SORT_TILES_PER_STEP)
    steps = nt // tps
    last = lambda i: jnp.minimum(i, steps - 1)
    return pl.pallas_call(
        _route_sort_kernel,
        out_shape=(jax.ShapeDtypeStruct(((nt + tps) * SORT_CAP, SORT_COLS), BF16),
                   jax.ShapeDtypeStruct((tokens, LANES), F32),
                   jax.ShapeDtypeStruct((nt, 8, LANES), jnp.int32)),
        grid=(steps + 1,),
        in_specs=[pl.BlockSpec((tps * SORT_TOKENS, D_MODEL), lambda i: (last(i), 0)),
                  pl.BlockSpec((1, D_MODEL), lambda i: (0, 0)),
                  pl.BlockSpec((D_MODEL, 2 * LANES), lambda i: (0, 0))],
        out_specs=(pl.BlockSpec((tps * SORT_CAP, SORT_COLS), lambda i: (i, 0)),
                   pl.BlockSpec((tps * SORT_TOKENS, LANES), lambda i: (last(i), 0)),
                   pl.BlockSpec((tps, 8, LANES), lambda i: (last(i), 0, 0))),
        compiler_params=_params("arbitrary"),
        name="route_sort",
    )(h, g, wr)


def _group_tables(ng, n_row_tiles):
    nt, ne = ng.shape
    gpt = GROUPS_PER_TILE
    n_e = jnp.sum(ng, axis=0)
    rt_e = (n_e + gpt - 1) // gpt
    rt_end = jnp.cumsum(rt_e)
    n_rt = rt_end[-1]
    r = jnp.arange(n_row_tiles, dtype=jnp.int32)
    valid_r = r < n_rt
    e_r = jnp.minimum(jnp.sum((r[:, None] >= rt_end[None, :]).astype(jnp.int32), axis=1), ne - 1)
    e_last = jnp.sum(jnp.where(r == n_rt - 1, e_r, 0))
    e_r = jnp.where(valid_r, e_r, e_last)
    sel = (e_r[:, None] == jnp.arange(ne, dtype=jnp.int32)[None, :]).astype(jnp.int32)
    pick = lambda v: jnp.sum(sel * v[None, :], axis=1)
    k0 = (r - pick(rt_end - rt_e)) * gpt
    gidx = k0[:, None] + jnp.arange(gpt, dtype=jnp.int32)[None, :]
    valid = valid_r[:, None] & (gidx < pick(n_e)[:, None])
    incl = jnp.cumsum(ng, axis=0)
    incl_r = jnp.sum(sel[:, None, :] * incl[None, :, :], axis=2)
    ng_r = jnp.sum(sel[:, None, :] * ng[None, :, :], axis=2)
    lo = jnp.cumsum(ng, axis=1) - ng
    lo_r = jnp.sum(sel[:, None, :] * lo[None, :, :], axis=2)
    ti = jnp.sum((incl_r[:, None, :] <= gidx[:, :, None]).astype(jnp.int32), axis=2)
    ti = jnp.minimum(ti, nt - 1)
    tsel = (ti[:, :, None] == jnp.arange(nt, dtype=jnp.int32)[None, None, :]).astype(jnp.int32)
    tpick = lambda v: jnp.sum(tsel * v[:, None, :], axis=2)
    within = gidx - tpick(incl_r - ng_r)
    rows = ti * SORT_CAP + GROUP * (tpick(lo_r) + within)
    spare = nt * SORT_CAP
    sink = (spare + SPARE_SINK + (r % 2)[:, None] * MOE_ROWS
            + GROUP * jnp.arange(gpt, dtype=jnp.int32)[None, :])
    src = jnp.where(valid, rows, spare + SPARE_ZERO).reshape(-1)
    dst = jnp.where(valid, rows, sink).reshape(-1)
    return e_r, jnp.sum(valid.astype(jnp.int32), axis=1), src, dst


def _group_copies(rows_ref, tile, hbm, buf, slot, sem, to_hbm, start):
    for k in range(GROUPS_PER_TILE):
        off = pl.multiple_of(rows_ref[tile * GROUPS_PER_TILE + k], GROUP)
        buf_rows = buf.at[slot, pl.ds(k * GROUP, GROUP), :]
        hbm_rows = hbm.at[pl.ds(off, GROUP), pl.ds(0, buf_rows.shape[-1])]
        cp = (pltpu.make_async_copy(buf_rows, hbm_rows, sem.at[slot]) if to_hbm
              else pltpu.make_async_copy(hbm_rows, buf_rows, sem.at[slot]))
        if start:
            cp.start()
        else:
            cp.wait()


def _moe_group_kernel(te_ref, tv_ref, src_ref, dst_ref, xs_hbm, w1_ref, w3_ref, w2_ref, ys_hbm,
                      xbuf, ybuf, acc_sc, xsem, ysem):
    r = pl.program_id(0)
    j = pl.program_id(1)
    nr = pl.num_programs(0)
    nj = pl.num_programs(1)
    slot = lax.rem(r, 2)
    valid = lambda t: tv_ref[jnp.clip(t, 0, nr - 1)] > 0
    gather = functools.partial(_group_copies, src_ref, hbm=xs_hbm, buf=xbuf, sem=xsem,
                               to_hbm=False)
    scatter = functools.partial(_group_copies, dst_ref, hbm=ys_hbm, buf=ybuf, sem=ysem,
                                to_hbm=True)

    @pl.when(j == 0)
    def _():
        @pl.when((r == 0) & valid(r))
        def _():
            gather(tile=r, slot=slot, start=True)

        @pl.when(valid(r))
        def _():
            gather(tile=r, slot=slot, start=False)

        @pl.when((r + 1 < nr) & valid(r + 1))
        def _():
            gather(tile=r + 1, slot=1 - slot, start=True)

        @pl.when((r >= 2) & valid(r - 2))
        def _():
            scatter(tile=r - 2, slot=slot, start=False)

    def compute(rows):
        x = xbuf[slot, rows, 0:D_MODEL]
        a = _dot(x, w1_ref[0])
        b = _dot(x, w3_ref[0])
        hid = (a * jax.nn.sigmoid(a) * b).astype(BF16)
        acc = jnp.where(j == 0, 0.0, acc_sc[rows, :]) + _dot(hid, w2_ref[0])
        acc_sc[rows, :] = acc
        gl = xbuf[slot, rows, D_MODEL:SORT_COLS].astype(F32)
        ybuf[slot, rows, :] = (acc * (gl[:, 0:1] + gl[:, 1:2])).astype(BF16)

    half = MOE_ROWS // 2
    full = tv_ref[r] > GROUPS_PER_TILE // 2

    @pl.when(full)
    def _():
        compute(slice(0, MOE_ROWS))

    @pl.when(valid(r) & jnp.logical_not(full))
    def _():
        compute(slice(0, half))

        @pl.when(j == nj - 1)
        def _():
            ybuf[slot, half:MOE_ROWS, :] = jnp.zeros((half, D_MODEL), BF16)

    @pl.when(valid(r) & (j == nj - 1))
    def _():
        scatter(tile=r, slot=slot, start=True)

    @pl.when((r == nr - 1) & (j == nj - 1))
    def _():
        @pl.when((r >= 1) & valid(r - 1))
        def _():
            scatter(tile=r - 1, slot=1 - slot, start=False)

        @pl.when(valid(r))
        def _():
            scatter(tile=r, slot=slot, start=False)


def _moe_group(xs, tile_expert, tile_valid, src, dst, w1, w3, w2, n_row_tiles, tf):
    ne, _, dff = w1.shape
    nj = dff // tf
    assert nj >= 2

    def w_in(r, j, te, tv, s, d):
        return (te[r], 0, jnp.where(tv[r] > 0, j, nj - 1))

    def w_out(r, j, te, tv, s, d):
        return (te[r], jnp.where(tv[r] > 0, j, nj - 1), 0)

    grid_spec = pltpu.PrefetchScalarGridSpec(
        num_scalar_prefetch=4,
        grid=(n_row_tiles, nj),
        in_specs=[pl.BlockSpec(memory_space=pl.ANY),
                  pl.BlockSpec((1, D_MODEL, tf), w_in),
                  pl.BlockSpec((1, D_MODEL, tf), w_in),
                  pl.BlockSpec((1, tf, D_MODEL), w_out)],
        out_specs=pl.BlockSpec(memory_space=pl.ANY),
        scratch_shapes=[pltpu.VMEM((2, MOE_ROWS, SORT_COLS), BF16),
                        pltpu.VMEM((2, MOE_ROWS, D_MODEL), BF16),
                        pltpu.VMEM((MOE_ROWS, D_MODEL), F32),
                        pltpu.SemaphoreType.DMA((2,)),
                        pltpu.SemaphoreType.DMA((2,))],
    )
    return pl.pallas_call(
        _moe_group_kernel,
        out_shape=jax.ShapeDtypeStruct(xs.shape, xs.dtype),
        grid_spec=grid_spec,
        input_output_aliases={4: 0},
        compiler_params=_params("arbitrary", "arbitrary"),
        name="moe_group",
    )(tile_expert, tile_valid, src, dst, xs, w1, w3, w2)


def _combine_kernel(h_ref, pos_ref, ys_ref, out_ref):
    pos = pos_ref[...]
    pt = _one_hot_rows(pos[:, 0:1], pos[:, 1:2], SORT_CAP)
    out_ref[...] = h_ref[...] + _dot(pt, ys_ref[...])


def _combine(h, pos, ys):
    tokens = h.shape[0]
    return pl.pallas_call(
        _combine_kernel,
        out_shape=jax.ShapeDtypeStruct((tokens, D_MODEL), F32),
        grid=(tokens // SORT_TOKENS,),
        in_specs=[pl.BlockSpec((SORT_TOKENS, D_MODEL), lambda i: (i, 0)),
                  pl.BlockSpec((SORT_TOKENS, LANES), lambda i: (i, 0)),
                  pl.BlockSpec((SORT_CAP, D_MODEL), lambda i: (i, 0))],
        out_specs=pl.BlockSpec((SORT_TOKENS, D_MODEL), lambda i: (i, 0)),
        compiler_params=_params("parallel"),
        name="combine",
    )(h, pos, ys)


def _row_tile(rows, want):
    tm = want
    while rows % tm:
        tm //= 2
    return tm


def _rope_tables(lp):
    inv_freq = 1.0 / (ROPE_THETA ** (jnp.arange(0, SWA_HEAD_DIM, 2, dtype=F32) / SWA_HEAD_DIM))
    row = jnp.arange(lp)
    seq = lp - BLOCK
    pos = jnp.where(row < seq, row + N_META, row - seq - FRONT).astype(F32)
    ang = pos[:, None] * inv_freq[None, :]
    cos, sin = jnp.cos(ang), jnp.sin(ang)
    zero = jnp.zeros_like(sin)
    rep = LANES // SWA_HEAD_DIM
    cos_t = jnp.tile(jnp.concatenate([cos, cos], axis=-1), (1, rep))
    sin_a = jnp.tile(jnp.concatenate([zero, sin], axis=-1), (1, rep))
    sin_b = jnp.tile(jnp.concatenate([-sin, zero], axis=-1), (1, rep))
    return cos_t, sin_a, sin_b


def kernel(x, meta, a_norm, a_w_in, a_w_gate2, a_b_gate, a_q_norm, a_k_norm, a_sinks, a_o_norm,
           a_w_out, f_norm, f_w1, f_w3, f_w2, c_norm, c_w_pw1, c_w_dw, c_b_dw, c_ln_g, c_ln_b,
           c_w_pw2, m_norm, m_w_router, m_w1, m_w3, m_w2):
    batch, seq, _ = x.shape
    assert seq % BLOCK == 0
    lp = BLOCK + seq

    head =jnp.concatenate([jnp.zeros((FRONT, D_MODEL), x.dtype), meta.astype(x.dtype)],
                           axis=0)[None]
    tm = _row_tile(seq, 512)

    w_in = a_w_in[0]
    gq, gk, gv, gr, glr, sq, sk, sv = jnp.split(
        w_in, [256, 512, 1024, 1536, 1552, 2064, 2192], axis=-1)
    w_in_r = jnp.concatenate(
        [gq, gk, gv, gr, sq, sk, sv, glr,
         jnp.zeros((D_MODEL, Z_COLS - Z_LR - GLA_LOWRANK), w_in.dtype)], axis=-1).astype(BF16)
    lane = jnp.arange(LANES)
    bd = (lane[:, None] // SWA_HEAD_DIM == lane[None, :] // SWA_HEAD_DIM).astype(BF16)
    rope = _rope_tables(lp) + (jnp.tile(a_q_norm[0], SWA_Q_HEADS)[None],
                               jnp.tile(a_k_norm[0], SWA_KV_HEADS)[None], bd)
    z = _in_proj(x, a_norm[0][None], w_in_r, rope, batch, lp, _row_tile(seq, 1024), 0)
    z = _in_proj(head, a_norm[0][None], w_in_r, rope, batch, lp, BLOCK, seq, z)

    wg = jnp.zeros((LANES, GLA_QK), F32).at[:GLA_LOWRANK].set(a_w_gate2[0]).astype(BF16)
    bb = math.gcd(batch, MIX_BATCH)
    o_a = _gla(z, wg, a_b_gate[0][None], a_o_norm[0][None], bb)
    o_b, d_w1, d_w3, d_w2 = _swa(z, a_sinks[0], (f_w1[0], f_w3[0], f_w2[0]), bb)

    w_out = a_w_out[0].astype(BF16)
    ffn = (w_out, f_norm[0][None], d_w1, d_w3, d_w2)
    h = _mix_out_ffn(x, o_a, o_b, *ffn, tm, FF_TILE, 0)
    h = _mix_out_ffn(head, o_a, o_b, *ffn, BLOCK, FF_TILE, seq, h)

    si = jnp.arange((SUBLANES - 1) * CONV_SPAN)
    shift = (jnp.arange(CONV_SPAN)[None, :]
             == (si % CONV_SPAN + si // CONV_SPAN + 1)[:, None]).astype(BF16)
    ne, _, dff = m_w1[0].shape
    experts = (m_w1[0].reshape(ne * D_MODEL, dff), m_w3[0].reshape(ne * D_MODEL, dff),
               m_w2[0].reshape(ne * dff, D_MODEL))
    h, e_w1, e_w3, e_w2 = _conv_module(
        h, c_norm[0][None], c_w_pw1[0].astype(BF16), shift,
        jnp.broadcast_to(c_w_dw[0][:, None, :], (CONV_WIDTH, SUBLANES, D_MODEL)),
        c_b_dw[0][None], c_ln_g[0][None], c_ln_b[0][None], c_w_pw2[0].astype(BF16), experts)
    h = h.reshape(batch * seq, D_MODEL)

    wr = jnp.zeros((D_MODEL, LANES), F32).at[:, :N_EXPERTS].set(m_w_router[0])
    wr_hi = wr.astype(BF16)
    wr = jnp.concatenate([wr_hi, (wr - wr_hi.astype(F32)).astype(BF16)], axis=1)
    tokens = batch * seq
    assert tokens % SORT_TOKENS == 0
    nt = tokens // SORT_TOKENS
    xs, pos, ng = _route_sort(h, m_norm[0][None], wr)
    ng = ng[:, 0, :N_EXPERTS]
    n_row_tiles = (nt * (SORT_CAP // GROUP)) // GROUPS_PER_TILE + N_EXPERTS
    tile_expert, tile_valid, src, dst = _group_tables(ng, n_row_tiles)
    ys = _moe_group(xs, tile_expert, tile_valid, src, dst, e_w1.reshape(ne, D_MODEL, dff),
                    e_w3.reshape(ne, D_MODEL, dff), e_w2.reshape(ne, dff, D_MODEL), n_row_tiles,
                    FF_TILE)
    out = _combine(h, pos, ys)
    return out.reshape(batch, seq, D_MODEL)
```
